```python
import math
import jax, jax.numpy as jnp
from jax import lax
import numpy as np

D_MODEL = 2048
BATCH = 2
SEQ = 4096
DEPTH = 1

GRID_W = 64
CTX_LEN = 256
F32 = jnp.float32

HY_WIDTH = D_MODEL
HY_ORDER = 2
HY_SHORT = 3
HY_BANDS = 16
HY_EMB = 2 * HY_BANDS + 1
HY_FFN = 64
HY_MIN_DECAY = math.log(1e-2) / 1.5
HY_MAX_DECAY = math.log(1e-2) / 0.3

SSD_INNER = 2 * D_MODEL
SSD_HEAD_DIM = 64
SSD_HEADS = SSD_INNER // SSD_HEAD_DIM
SSD_STATE = 128
SSD_GROUPS = 8
SSD_HPG = SSD_HEADS // SSD_GROUPS
SSD_CONV = 3
SSD_CHUNK = 128
SSD_CONV_DIM = SSD_INNER + 2 * SSD_GROUPS * SSD_STATE

OFF_X = HY_ORDER * HY_WIDTH + HY_WIDTH
OFF_B = OFF_X + SSD_INNER
OFF_C = OFF_B + SSD_GROUPS * SSD_STATE
OFF_DT = OFF_C + SSD_GROUPS * SSD_STATE
OFF_Z = OFF_DT + 2 * SSD_HEADS
OFF_GATE = OFF_Z + SSD_INNER
N_COLS = OFF_GATE + 2 * D_MODEL

PEER_HEADS = 8
PEER_NKEYS = 128
PEER_EXPERTS = PEER_NKEYS * PEER_NKEYS
PEER_TOPK = 16
PEER_DKEY = 256
PEER_BLOCK = 128

DEEPNORM_ALPHA = (2.0 * DEPTH) ** 0.25
DEEPNORM_BETA = (8.0 * DEPTH) ** -0.25

kernel_name = 'hybrid_hyena_ssd_peer_dit_block'


def _normalize(x, eps=1e-6):
    xf = x.astype(F32)
    mu = jnp.mean(xf, axis=-1, keepdims=True)
    var = jnp.mean(jnp.square(xf - mu), axis=-1, keepdims=True)
    return (xf - mu) * lax.rsqrt(var + eps)


def layer_norm(x, g, b):
    return (_normalize(x) * g + b).astype(x.dtype)


def modulate(h, shift, scale):
    return (_normalize(h) * (1.0 + scale) + shift).astype(h.dtype)


def centred_dwconv(x, w, b):
    k, ch = w.shape
    y = lax.conv_general_dilated(x, w[:, None, :].astype(x.dtype), window_strides=(1,),
                                 padding=[(k // 2, k // 2)],
                                 dimension_numbers=('NWC', 'WIO', 'NWC'),
                                 feature_group_count=ch)
    return y + b


def hyena_filters(length, p):
    t = jnp.linspace(0.0, 1.0, length, dtype=F32)[:, None]
    w = 2.0 * math.pi * jnp.arange(length, dtype=F32)[:, None] / length
    bands = jnp.linspace(1e-4, HY_BANDS - 1, HY_BANDS, dtype=F32)[None, :]
    feats = jnp.concatenate([t, jnp.cos(bands * w), -jnp.sin(bands * w)], axis=-1)
    h = jnp.sin(p['hy_freq1'] * (feats @ p['hy_ffn_w1'] + p['hy_ffn_b1']))
    h = jnp.sin(p['hy_freq2'] * (h @ p['hy_ffn_w2'] + p['hy_ffn_b2']))
    h = (h @ p['hy_ffn_w3']).astype(F32).reshape(length, HY_ORDER, HY_WIDTH)
    dist = jnp.abs(jnp.arange(length) - length // 2).astype(F32) / (length / 2.0)
    deltas = jnp.abs(jnp.linspace(HY_MIN_DECAY, HY_MAX_DECAY, HY_WIDTH, dtype=F32))
    h = h * jnp.exp(-dist[:, None, None] * deltas[None, None, :])
    return h / (jnp.sum(jnp.abs(h), axis=0, keepdims=True) + 1e-6)


def long_conv(z, h, skip):
    length = z.shape[1]
    n = 2 * length
    zf = jnp.fft.rfft(z.astype(F32), n=n, axis=1)
    hf = jnp.fft.rfft(h, n=n, axis=0)
    y = jnp.fft.irfft(zf * hf[None], n=n, axis=1)[:, length // 2: length // 2 + length]
    return (y + z.astype(F32) * skip.astype(F32)).astype(z.dtype)


def hyena_branch(proj_hy, p):
    length = proj_hy.shape[1]
    u = centred_dwconv(proj_hy, p['hy_conv_w'], p['hy_conv_b'])
    v, g1, g2 = jnp.split(u, 3, axis=-1)
    filt = hyena_filters(length, p)
    z = g1 * long_conv(v, filt[:, 0], p['hy_skip'][0])
    return g2 * long_conv(z, filt[:, 1], p['hy_skip'][1])


def _ssd_decay(dt_proj, p):
    b, length, _ = dt_proj.shape
    a_neg = -jnp.exp(p['ssd_a_log'].astype(F32))
    dt = jax.nn.softplus(dt_proj.astype(F32).reshape(b, length, 2, SSD_HEADS)
                         + p['ssd_dt_bias'].astype(F32))
    return a_neg, dt


def _ssd_prepare(x, dt, a_neg, bm):
    b, length, _, _ = x.shape
    nc = length // SSD_CHUNK
    a_cum = jnp.cumsum((dt * a_neg).reshape(b, nc, SSD_CHUNK, SSD_GROUPS, SSD_HPG), axis=2)
    xdt = (x.astype(F32) * dt[..., None]).reshape(b, nc, SSD_CHUNK, SSD_GROUPS, SSD_HPG, SSD_HEAD_DIM)
    bc = bm.reshape(b, nc, SSD_CHUNK, SSD_GROUPS, SSD_STATE)
    return a_cum, xdt, bc


def _ssd_chunk_states(a_cum, xdt, bc):
    decay_to_end = jnp.exp(a_cum[:, :, -1:] - a_cum)
    states = jnp.einsum('bcsgn,bcsgk,bcsgkp->bcgkpn', bc, decay_to_end, xdt)
    chunk_decay = jnp.exp(a_cum[:, :, -1])
    return states, chunk_decay


def _ssd_recurrence(states, chunk_decay, init):
    def step(s, inp):
        cs, cd = inp
        return s * cd[..., None, None] + cs, s
    final, prev = lax.scan(step, init, (jnp.moveaxis(states, 1, 0), jnp.moveaxis(chunk_decay, 1, 0)))
    return jnp.moveaxis(prev, 0, 1), final


def ssd_scan(x, dt, a_neg, bm, cm, init):
    b, length, _, _ = x.shape
    nc = length // SSD_CHUNK
    a_cum, xdt, bc = _ssd_prepare(x, dt, a_neg, bm)
    cc = cm.reshape(b, nc, SSD_CHUNK, SSD_GROUPS, SSD_STATE)
    states, chunk_decay = _ssd_chunk_states(a_cum, xdt, bc)
    prev, final = _ssd_recurrence(states, chunk_decay, init)
    lower = jnp.tril(jnp.ones((SSD_CHUNK, SSD_CHUNK), dtype=bool))[:, :, None, None]
    seg = a_cum[:, :, :, None] - a_cum[:, :, None]
    decay = jnp.where(lower, jnp.exp(jnp.where(lower, seg, 0.0)), 0.0)
    cb = jnp.einsum('bclgn,bcsgn->bclsg', cc, bc)
    y_diag = jnp.einsum('bclsgk,bcsgkp->bclgkp', cb[..., None] * decay, xdt)
    y_off = jnp.einsum('bclgn,bcgkpn->bclgkp', cc, prev) * jnp.exp(a_cum)[..., None]
    return (y_diag + y_off).reshape(b, length, SSD_HEADS, SSD_HEAD_DIM), final


def ssd_final_state(x, dt, a_neg, bm, init):
    a_cum, xdt, bc = _ssd_prepare(x, dt, a_neg, bm)
    states, chunk_decay = _ssd_chunk_states(a_cum, xdt, bc)
    _, final = _ssd_recurrence(states, chunk_decay, init)
    return final


def _flip(t):
    return jnp.flip(t, axis=1)


def _zero_state(b):
    return jnp.zeros((b, SSD_GROUPS, SSD_HPG, SSD_HEAD_DIM, SSD_STATE), F32)


def bidir_ssd(xs, bm, cm, dt_proj, p, init_f, init_b):
    a_neg, dt = _ssd_decay(dt_proj, p)
    y_f, s_f = ssd_scan(xs, dt[:, :, 0], a_neg[0], bm, cm, init_f)
    y_b, s_b = ssd_scan(_flip(xs), _flip(dt[:, :, 1]), a_neg[1], _flip(bm), _flip(cm), init_b)
    d_skip = (p['ssd_d'][0] + p['ssd_d'][1]).astype(F32)
    y = y_f + _flip(y_b) + xs.astype(F32) * d_skip[:, None]
    return y.astype(xs.dtype), s_f, s_b


def gated_rmsnorm(y, z, w, eps=1e-5):
    g = (y * jax.nn.silu(z)).astype(F32)
    gg = g.reshape(g.shape[:-1] + (SSD_GROUPS, SSD_INNER // SSD_GROUPS))
    gg = gg * lax.rsqrt(jnp.mean(jnp.square(gg), axis=-1, keepdims=True) + eps)
    return (gg.reshape(g.shape) * w).astype(y.dtype)


def ssd_context_states(m_ctx, p):
    b, lc, _ = m_ctx.shape
    w = p['w_in']
    n_xb = SSD_INNER + SSD_GROUPS * SSD_STATE
    xb = jax.nn.silu(centred_dwconv(m_ctx @ w[:, OFF_X:OFF_C], p['ssd_conv_w'][:, :n_xb], p['ssd_conv_b'][:n_xb]))
    xs = xb[..., :SSD_INNER].reshape(b, lc, SSD_HEADS, SSD_HEAD_DIM)
    bm = xb[..., SSD_INNER:].reshape(b, lc, SSD_GROUPS, SSD_STATE)
    a_neg, dt = _ssd_decay(m_ctx @ w[:, OFF_DT:OFF_Z], p)
    s_f = ssd_final_state(xs, dt[:, :, 0], a_neg[0], bm, _zero_state(b))
    s_b = ssd_final_state(_flip(xs), _flip(dt[:, :, 1]), a_neg[1], _flip(bm), _zero_state(b))
    return s_f, s_b


def token_mixer(m, p, init_f, init_b):
    b, length, _ = m.shape
    proj = m @ p['w_in']
    y_hy = hyena_branch(proj[..., :OFF_X], p)
    xbc = jax.nn.silu(centred_dwconv(proj[..., OFF_X:OFF_DT], p['ssd_conv_w'], p['ssd_conv_b']))
    xs = xbc[..., :SSD_INNER].reshape(b, length, SSD_HEADS, SSD_HEAD_DIM)
    bm = xbc[..., SSD_INNER:SSD_INNER + SSD_GROUPS * SSD_STATE].reshape(b, length, SSD_GROUPS, SSD_STATE)
    cm = xbc[..., SSD_INNER + SSD_GROUPS * SSD_STATE:].reshape(b, length, SSD_GROUPS, SSD_STATE)
    y, s_f, s_b = bidir_ssd(xs, bm, cm, proj[..., OFF_DT:OFF_Z], p, init_f, init_b)
    y_ss = gated_rmsnorm(y.reshape(b, length, SSD_INNER), proj[..., OFF_Z:OFF_GATE], p['ssd_norm_w'])
    gates = jax.nn.sigmoid(proj[..., OFF_GATE:])
    merged = (gates[..., :D_MODEL] * (y_hy @ p['w_branch_hy'])
              + gates[..., D_MODEL:] * (y_ss @ p['w_branch_ssd']))
    return merged @ p['w_out'], s_f, s_b


def peer_ffn(h, p):
    b, length, d = h.shape
    t = b * length
    hf = h.reshape(t, d)
    q = _normalize((hf @ p['peer_wq']).reshape(t, PEER_HEADS, PEER_DKEY)).astype(h.dtype)
    half = PEER_DKEY // 2
    s1 = jnp.einsum('thd,hkd->thk', q[..., :half], p['peer_subkeys'][:, 0])
    s2 = jnp.einsum('thd,hkd->thk', q[..., half:], p['peer_subkeys'][:, 1])
    v1, i1 = lax.top_k(s1, PEER_TOPK)
    v2, i2 = lax.top_k(s2, PEER_TOPK)
    cand_s = (v1[..., :, None] + v2[..., None, :]).reshape(t, PEER_HEADS, PEER_TOPK * PEER_TOPK)
    cand_i = (i1[..., :, None] * PEER_NKEYS + i2[..., None, :]).reshape(t, PEER_HEADS, PEER_TOPK * PEER_TOPK)
    best_s, pos = lax.top_k(cand_s, PEER_TOPK)
    idx = jnp.take_along_axis(cand_i, pos, axis=-1).reshape(t, PEER_HEADS * PEER_TOPK)
    gate = jax.nn.softmax(best_s.astype(F32), axis=-1).astype(h.dtype).reshape(t, PEER_HEADS * PEER_TOPK)
    nb = t // PEER_BLOCK

    def block(args):
        hb, ib, gb = args
        act = jax.nn.gelu(jnp.einsum('td,tkd->tk', hb, p['peer_u'][ib]))
        return jnp.einsum('tk,tkd->td', act * gb, p['peer_v'][ib])

    out = lax.map(block, (hf.reshape(nb, PEER_BLOCK, d),
                          idx.reshape(nb, PEER_BLOCK, PEER_HEADS * PEER_TOPK),
                          gate.reshape(nb, PEER_BLOCK, PEER_HEADS * PEER_TOPK)))
    return out.reshape(b, length, d)


def layer_forward(h, h_ctx, c, c_ctx, p, need_ctx_out):
    b = h.shape[0]
    mod = jax.nn.silu(c) @ p['ada_w'] + p['ada_b']
    mod_ctx = jax.nn.silu(c_ctx) @ p['ada_w'] + p['ada_b']
    sh1, sc1, gt1, sh2, sc2, gt2 = jnp.split(mod[:, None, :], 6, axis=-1)
    csh1, csc1, cgt1, csh2, csc2, cgt2 = jnp.split(mod_ctx, 6, axis=-1)
    m_ctx = modulate(h_ctx, csh1, csc1)
    if need_ctx_out:
        mix_ctx, s_f, s_b = token_mixer(m_ctx, p, _zero_state(b), _zero_state(b))
        h_ctx = layer_norm(DEEPNORM_ALPHA * h_ctx + cgt1 * mix_ctx, p['ln1_g'], p['ln1_b'])
        ffn_ctx = peer_ffn(modulate(h_ctx, csh2, csc2), p)
        h_ctx = layer_norm(DEEPNORM_ALPHA * h_ctx + cgt2 * ffn_ctx, p['ln2_g'], p['ln2_b'])
    else:
        s_f, s_b = ssd_context_states(m_ctx, p)
    mix, _, _ = token_mixer(modulate(h, sh1, sc1), p, s_f, s_b)
    h = layer_norm(DEEPNORM_ALPHA * h + gt1 * mix, p['ln1_g'], p['ln1_b'])
    ffn = peer_ffn(modulate(h, sh2, sc2), p)
    h = layer_norm(DEEPNORM_ALPHA * h + gt2 * ffn, p['ln2_g'], p['ln2_b'])
    return h, h_ctx


def setup_inputs(seed: int = 0) -> dict:
    key = jax.random.key(seed)
    ks = iter(jax.random.split(key, 40))

    def nrm(shape, scale):
        return jax.random.normal(next(ks), shape, F32) * scale

    L = DEPTH
    dt_init = jnp.exp(jax.random.uniform(next(ks), (L, 2, SSD_HEADS), F32, math.log(1e-3), math.log(1e-1)))
    inp = {
        'x': nrm((BATCH, SEQ, D_MODEL), 1.0),
        'c': nrm((BATCH, D_MODEL), 1.0),
        'ctx': nrm((BATCH, CTX_LEN, D_MODEL), 1.0),
        'c_ctx': nrm((D_MODEL,), 1.0),
        'ada_w': nrm((L, D_MODEL, 6 * D_MODEL), 0.5 * D_MODEL ** -0.5),
        'ada_b': nrm((L, 6 * D_MODEL), 0.02),
        'w_in': nrm((L, D_MODEL, N_COLS), D_MODEL ** -0.5),
        'hy_conv_w': nrm((L, HY_SHORT, 3 * HY_WIDTH), HY_SHORT ** -0.5),
        'hy_conv_b': nrm((L, 3 * HY_WIDTH), 0.02),
        'hy_ffn_w1': nrm((L, HY_EMB, HY_FFN), HY_EMB ** -0.5),
        'hy_ffn_b1': nrm((L, HY_FFN), 0.1),
        'hy_freq1': 1.0 + nrm((L, HY_FFN), 0.1),
        'hy_ffn_w2': nrm((L, HY_FFN, HY_FFN), HY_FFN ** -0.5),
        'hy_ffn_b2': nrm((L, HY_FFN), 0.1),
        'hy_freq2': 1.0 + nrm((L, HY_FFN), 0.1),
        'hy_ffn_w3': nrm((L, HY_FFN, HY_ORDER * HY_WIDTH), HY_FFN ** -0.5),
        'hy_skip': nrm((L, HY_ORDER, HY_WIDTH), 0.2),
        'ssd_conv_w': nrm((L, SSD_CONV, SSD_CONV_DIM), SSD_CONV ** -0.5),
        'ssd_conv_b': nrm((L, SSD_CONV_DIM), 0.02),
        'ssd_a_log': jnp.log(jax.random.uniform(next(ks), (L, 2, SSD_HEADS), F32, 1.0, 16.0)),
        'ssd_dt_bias': dt_init + jnp.log(-jnp.expm1(-dt_init)),
        'ssd_d': 1.0 + nrm((L, 2, SSD_HEADS), 0.1),
        'ssd_norm_w': 1.0 + nrm((L, SSD_INNER), 0.1),
        'w_branch_hy': nrm((L, HY_WIDTH, D_MODEL), DEEPNORM_BETA * HY_WIDTH ** -0.5),
        'w_branch_ssd': nrm((L, SSD_INNER, D_MODEL), DEEPNORM_BETA * SSD_INNER ** -0.5),
        'w_out': nrm((L, D_MODEL, D_MODEL), DEEPNORM_BETA * D_MODEL ** -0.5),
        'ln1_g': 1.0 + nrm((L, D_MODEL), 0.05),
        'ln1_b': nrm((L, D_MODEL), 0.02),
        'peer_wq': nrm((L, D_MODEL, PEER_HEADS * PEER_DKEY), D_MODEL ** -0.5),
        'peer_subkeys': nrm((L, PEER_HEADS, 2, PEER_NKEYS, PEER_DKEY // 2), (PEER_DKEY // 2) ** -0.5),
        'peer_u': nrm((L, PEER_EXPERTS, D_MODEL), D_MODEL ** -0.5),
        'peer_v': nrm((L, PEER_EXPERTS, D_MODEL), DEEPNORM_BETA),
        'ln2_g': 1.0 + nrm((L, D_MODEL), 0.05),
        'ln2_b': nrm((L, D_MODEL), 0.02),
    }
    return inp


def reference(x, c, ctx, c_ctx, ada_w, ada_b, w_in, hy_conv_w, hy_conv_b, hy_ffn_w1, hy_ffn_b1,
              hy_freq1, hy_ffn_w2, hy_ffn_b2, hy_freq2, hy_ffn_w3, hy_skip, ssd_conv_w, ssd_conv_b,
              ssd_a_log, ssd_dt_bias, ssd_d, ssd_norm_w, w_branch_hy, w_branch_ssd, w_out, ln1_g, ln1_b,
              peer_wq, peer_subkeys, peer_u, peer_v, ln2_g, ln2_b):
    h, h_ctx = x, ctx
    for layer in range(DEPTH):
        p = dict(ada_w=ada_w[layer], ada_b=ada_b[layer], w_in=w_in[layer],
                 hy_conv_w=hy_conv_w[layer], hy_conv_b=hy_conv_b[layer],
                 hy_ffn_w1=hy_ffn_w1[layer], hy_ffn_b1=hy_ffn_b1[layer], hy_freq1=hy_freq1[layer],
                 hy_ffn_w2=hy_ffn_w2[layer], hy_ffn_b2=hy_ffn_b2[layer], hy_freq2=hy_freq2[layer],
                 hy_ffn_w3=hy_ffn_w3[layer], hy_skip=hy_skip[layer],
                 ssd_conv_w=ssd_conv_w[layer], ssd_conv_b=ssd_conv_b[layer],
                 ssd_a_log=ssd_a_log[layer], ssd_dt_bias=ssd_dt_bias[layer], ssd_d=ssd_d[layer],
                 ssd_norm_w=ssd_norm_w[layer], w_branch_hy=w_branch_hy[layer],
                 w_branch_ssd=w_branch_ssd[layer], w_out=w_out[layer],
                 ln1_g=ln1_g[layer], ln1_b=ln1_b[layer], peer_wq=peer_wq[layer],
                 peer_subkeys=peer_subkeys[layer], peer_u=peer_u[layer], peer_v=peer_v[layer],
                 ln2_g=ln2_g[layer], ln2_b=ln2_b[layer])
        h, h_ctx = layer_forward(h, h_ctx, c, c_ctx, p, layer < DEPTH - 1)
    return h
```

```python
import functools
import math

import numpy as np
import jax
import jax.numpy as jnp
from jax import lax
from jax.experimental import pallas as pl
from jax.experimental.pallas import tpu as pltpu

F32 = jnp.float32
BF16 = jnp.bfloat16

D_MODEL = 2048
DEPTH = 1
CTX_LEN = 256
HY_WIDTH = D_MODEL
HY_BANDS = 16
HY_EMB = 2 * HY_BANDS + 1
HY_FFN = 64
HY_MIN_DECAY = math.log(1e-2) / 1.5
HY_MAX_DECAY = math.log(1e-2) / 0.3
SSD_INNER = 2 * D_MODEL
SSD_HEAD_DIM = 64
SSD_HEADS = SSD_INNER // SSD_HEAD_DIM
SSD_STATE = 128
SSD_GROUPS = 8
SSD_HPG = SSD_HEADS // SSD_GROUPS
SSD_CHUNK = 128
OFF_X = 3 * HY_WIDTH
OFF_B = OFF_X + SSD_INNER
OFF_C = OFF_B + SSD_GROUPS * SSD_STATE
OFF_DT = OFF_C + SSD_GROUPS * SSD_STATE
OFF_Z = OFF_DT + 2 * SSD_HEADS
OFF_GATE = OFF_Z + SSD_INNER
N_COLS = OFF_GATE + 2 * D_MODEL
PEER_HEADS = 8
PEER_NKEYS = 128
PEER_EXPERTS = PEER_NKEYS * PEER_NKEYS
PEER_TOPK = 16
PEER_DKEY = 256
DEEPNORM_ALPHA = (2.0 * DEPTH) ** 0.25

LANES = 128
VMEM_LIMIT = 56 * 1024 * 1024

FFT_N1 = 128
SA_PITCH_PAD = 8
SB_PITCH = FFT_N1 + 8


def _cparams(sem, vmem=VMEM_LIMIT):
    return pltpu.CompilerParams(dimension_semantics=sem, vmem_limit_bytes=vmem)


def _silu(x):
    return x * (1.0 / (1.0 + jnp.exp(-x)))


def _sigmoid(x):
    return 1.0 / (1.0 + jnp.exp(-x))


def _softplus(x):
    return jnp.maximum(x, 0.0) + jnp.log(1.0 + jnp.exp(-jnp.abs(x)))


def _gelu_tanh(x):
    return 0.5 * x * (1.0 + jnp.tanh(math.sqrt(2.0 / math.pi) * (x + 0.044715 * (x * x * x))))


def _normalize_rows(x, eps):
    mu = jnp.mean(x, axis=-1, keepdims=True)
    xc = x - mu
    var = jnp.mean(xc * xc, axis=-1, keepdims=True)
    return xc * lax.rsqrt(var + eps)


def _ada_kernel(c_ref, w_ref, b_ref, o_ref):
    a = _silu(c_ref[...]).astype(BF16)
    o_ref[...] = jnp.dot(a, w_ref[...].astype(BF16), preferred_element_type=F32) + b_ref[...]


def _ada(cc, ada_w, ada_b):
    rows, d = cc.shape
    n = ada_w.shape[1]
    tn = 1024
    return pl.pallas_call(
        _ada_kernel,
        grid=(n // tn,),
        in_specs=[pl.BlockSpec((rows, d), lambda j: (0, 0)),
                  pl.BlockSpec((d, tn), lambda j: (0, j)),
                  pl.BlockSpec((1, tn), lambda j: (0, j))],
        out_specs=pl.BlockSpec((rows, tn), lambda j: (0, j)),
        out_shape=jax.ShapeDtypeStruct((rows, n), F32),
        compiler_params=_cparams(("arbitrary",)),
    )(cc, ada_w, ada_b.reshape(1, n))


def _mod_kernel(x_ref, sc_ref, sh_ref, o_ref):
    xn = _normalize_rows(x_ref[0], 1e-6)
    o_ref[0] = (xn * (1.0 + sc_ref[0]) + sh_ref[0]).astype(o_ref.dtype)


def _modulate(x, scale, shift, tr):
    b, l, d = x.shape
    per_batch = scale.shape[0] == b
    smap = (lambda i, j: (i, 0, 0)) if per_batch else (lambda i, j: (0, 0, 0))
    return pl.pallas_call(
        _mod_kernel,
        grid=(b, l // tr),
        in_specs=[pl.BlockSpec((1, tr, d), lambda i, j: (i, j, 0)),
                  pl.BlockSpec((1, 1, d), smap),
                  pl.BlockSpec((1, 1, d), smap)],
        out_specs=pl.BlockSpec((1, tr, d), lambda i, j: (i, j, 0)),
        out_shape=jax.ShapeDtypeStruct((b, l, d), BF16),
        compiler_params=_cparams(("arbitrary", "arbitrary")),
    )(x, scale, shift)


def _mm_kernel(a_ref, b_ref, o_ref, *, slab):
    acc = jnp.dot(a_ref[...], b_ref[...], preferred_element_type=F32)
    if slab:
        for s in range(o_ref.shape[0]):
            o_ref[s] = acc[:, s * LANES:(s + 1) * LANES].astype(o_ref.dtype)
    else:
        o_ref[...] = acc.astype(o_ref.dtype)


def _matmul(a, b, tm, tn, out_dtype=F32, slab=False):
    m, k = a.shape
    n = b.shape[1]
    if slab:
        out_shape = jax.ShapeDtypeStruct((n // LANES, m, LANES), out_dtype)
        out_spec = pl.BlockSpec((tn // LANES, tm, LANES), lambda i, j: (j, i, 0))
    else:
        out_shape = jax.ShapeDtypeStruct((m, n), out_dtype)
        out_spec = pl.BlockSpec((tm, tn), lambda i, j: (i, j))
    return pl.pallas_call(
        functools.partial(_mm_kernel, slab=slab),
        grid=(m // tm, n // tn),
        in_specs=[pl.BlockSpec((tm, k), lambda i, j: (i, 0)),
                  pl.BlockSpec((k, tn), lambda i, j: (0, j))],
        out_specs=out_spec,
        out_shape=out_shape,
        compiler_params=_cparams(("arbitrary", "arbitrary")),
    )(a, b)


def _conv3_kernel(x_ref, w_ref, b_ref, o_ref, *, silu):
    x = x_ref[0]
    l = x.shape[0]
    row = lax.broadcasted_iota(jnp.int32, x.shape, 0)
    prev = jnp.where(row == 0, 0.0, pltpu.roll(x, 1, 0))
    nxt = jnp.where(row == l - 1, 0.0, pltpu.roll(x, l - 1, 0))
    w = w_ref[0]
    y = prev * w[0:1] + x * w[1:2] + nxt * w[2:3] + b_ref[0]
    if silu:
        y = _silu(y)
    o_ref[0] = y


def _conv3(p_slab, slab0, nslab, w, bias, seq, silu):
    rows = p_slab.shape[1]
    nb = rows // seq
    return pl.pallas_call(
        functools.partial(_conv3_kernel, silu=silu),
        grid=(nslab, nb),
        in_specs=[pl.BlockSpec((1, seq, LANES), lambda s, b: (slab0 + s, b, 0)),
                  pl.BlockSpec((1, 3, LANES), lambda s, b: (s, 0, 0)),
                  pl.BlockSpec((1, 1, LANES), lambda s, b: (s, 0, 0))],
        out_specs=pl.BlockSpec((1, seq, LANES), lambda s, b: (s, b, 0)),
        out_shape=jax.ShapeDtypeStruct((nslab, rows, LANES), F32),
        compiler_params=_cparams(("arbitrary", "arbitrary")),
    )(p_slab, w, bias)


def _hyconv_kernel(x_ref, w_ref, b_ref, o_ref, *, seq, nb):
    nj = seq // FFT_N1

    def gather(n1):
        return jnp.concatenate(
            [x_ref[0, pl.ds(n1 + b * seq, nj, stride=FFT_N1), :] for b in range(nb)], axis=0)

    w = w_ref[0]
    w0, w1, w2, bias = w[0:1], w[1:2], w[2:3], b_ref[0]
    jrow = lax.broadcasted_iota(jnp.int32, (nb * nj, LANES), 0) % nj

    def shift_down(g):
        return jnp.where(jrow == 0, 0.0, pltpu.roll(g, 1, 0))

    def shift_up(g):
        return jnp.where(jrow == nj - 1, 0.0, pltpu.roll(g, nb * nj - 1, 0))

    g_first = gather(0)
    g_second = gather(1)
    g_last = gather(FFT_N1 - 1)
    o_ref[0, 0] = shift_down(g_last) * w0 + g_first * w1 + g_second * w2 + bias

    def body(n1, carry):
        g_prev, g_cur = carry
        g_next = gather(n1 + 1)
        o_ref[0, n1] = g_prev * w0 + g_cur * w1 + g_next * w2 + bias
        return g_cur, g_next

    g_prev, g_cur = lax.fori_loop(1, FFT_N1 - 1, body, (g_first, g_second))
    o_ref[0, FFT_N1 - 1] = g_prev * w0 + g_cur * w1 + shift_up(g_first) * w2 + bias


def _hyconv(p_slab, nslab, w, bias, seq):
    rows = p_slab.shape[1]
    nb = rows // seq
    nj = seq // FFT_N1
    return pl.pallas_call(
        functools.partial(_hyconv_kernel, seq=seq, nb=nb),
        grid=(nslab,),
        in_specs=[pl.BlockSpec((1, rows, LANES), lambda s: (s, 0, 0)),
                  pl.BlockSpec((1, 3, LANES), lambda s: (s, 0, 0)),
                  pl.BlockSpec((1, 1, LANES), lambda s: (s, 0, 0))],
        out_specs=pl.BlockSpec((1, FFT_N1, nb * nj, LANES), lambda s: (s, 0, 0, 0)),
        out_shape=jax.ShapeDtypeStruct((nslab, FFT_N1, nb * nj, LANES), F32),
        compiler_params=_cparams(("arbitrary",)),
    )(p_slab, w, bias)


@functools.lru_cache(maxsize=None)
def _dft_tables(seq):
    nj = seq // FFT_N1
    n2 = nj + nj // 2
    n = FFT_N1 * n2
    out0 = nj // 2
    n1 = np.arange(FFT_N1, dtype=np.float64)
    k2 = np.arange(n2, dtype=np.float64)
    jn = np.arange(nj, dtype=np.float64)
    ang = -2.0 * np.pi * (k2[None, :, None] * jn[None, None, :] / n2 + n1[:, None, None] * k2[None, :, None] / n)
    cr, ci = np.cos(ang), np.sin(ang)
    fa = np.concatenate([np.concatenate([cr, -ci], axis=2), np.concatenate([ci, cr], axis=2)], axis=1)
    k1 = np.arange(FFT_N1, dtype=np.float64)
    angb = -2.0 * np.pi * np.outer(k1, n1) / FFT_N1
    dr, di = np.cos(angb), np.sin(angb)
    fb = np.block([[dr, -di], [di, dr]])
    fbi = np.block([[dr, di], [-di, dr]])
    angi = 2.0 * np.pi * (n1[:, None, None] * k2[None, None, :] / n + (out0 + jn)[None, :, None] * k2[None, None, :] / n2)
    er, ei = np.cos(angi) / n, np.sin(angi) / n
    fai = np.concatenate([np.concatenate([er, -ei], axis=2), np.concatenate([ei, er], axis=2)], axis=1)
    return (fa.astype(np.float32), fb.astype(np.float32), fbi.astype(np.float32), fai.astype(np.float32), n2)


@functools.lru_cache(maxsize=None)
def _filter_consts(seq):
    t = np.linspace(0.0, 1.0, seq, dtype=np.float32)[:, None].astype(np.float64)
    w = 2.0 * np.pi * np.arange(seq, dtype=np.float32)[:, None].astype(np.float64) / seq
    bands = np.linspace(1e-4, HY_BANDS - 1, HY_BANDS, dtype=np.float32)[None, :].astype(np.float64)
    feats = np.concatenate([t, np.cos(bands * w), -np.sin(bands * w)], axis=-1)
    feats_p = np.zeros((seq, LANES), np.float32)
    feats_p[:, :HY_EMB] = feats
    dist = (np.abs(np.arange(seq) - seq // 2).astype(np.float32) / np.float32(seq / 2.0)).reshape(seq, 1)
    deltas = np.abs(np.linspace(HY_MIN_DECAY, HY_MAX_DECAY, HY_WIDTH, dtype=np.float32))
    return feats_p, dist.astype(np.float32), deltas.astype(np.float32)


def _dot_f32(a, b):
    def split(x):
        hi = x.astype(BF16)
        r1 = x - hi.astype(F32)
        mid = r1.astype(BF16)
        lo = (r1 - mid.astype(F32)).astype(BF16)
        return hi, mid, lo
    a0, a1, a2 = split(a)
    b0, b1, b2 = split(b)
    d = functools.partial(jnp.dot, preferred_element_type=F32)
    return (d(a0, b0) + (d(a0, b1) + d(a1, b0)) + (d(a0, b2) + d(a1, b1) + d(a2, b0)))


def _filter_kernel(feats_ref, w1_ref, b1_ref, f1_ref, w2_ref, b2_ref, f2_ref, w3_ref, dist_ref, delta_ref,
                   fa_ref, fb_ref, o_ref, hid_ref, h_ref, sar_ref, sai_ref, *, seq, n2, pitch):
    nj = seq // FFT_N1

    @pl.when((pl.program_id(0) == 0) & (pl.program_id(1) == 0))
    def _():
        h1 = jnp.sin(f1_ref[...] * (_dot_f32(feats_ref[...], w1_ref[...]) + b1_ref[...]))
        hid_ref[...] = jnp.sin(f2_ref[...] * (_dot_f32(h1, w2_ref[...]) + b2_ref[...]))

    h = _dot_f32(hid_ref[...], w3_ref[0, 0])
    h = h * jnp.exp(-dist_ref[...] * delta_ref[0])
    h = h / (jnp.sum(jnp.abs(h), axis=0, keepdims=True) + 1e-6)
    h_ref[...] = h

    def stage_a(n1, c):
        g = h_ref[pl.ds(n1, nj, stride=FFT_N1), :].astype(BF16)
        y = jnp.dot(fa_ref[n1], g, preferred_element_type=F32)
        base = pl.multiple_of(n1 * pitch, 8)
        sar_ref[pl.ds(base, n2), :] = y[:n2]
        sai_ref[pl.ds(base, n2), :] = y[n2:]
        return c

    lax.fori_loop(0, FFT_N1, stage_a, 0)

    def stage_b(k2, c):
        yr = sar_ref[pl.ds(k2, FFT_N1, stride=pitch), :]
        yi = sai_ref[pl.ds(k2, FFT_N1, stride=pitch), :]
        op = jnp.concatenate([yr, yi], axis=0).astype(BF16)
        o_ref[0, 0, k2] = jnp.dot(fb_ref[...], op, preferred_element_type=F32).astype(o_ref.dtype)
        return c

    lax.fori_loop(0, n2, stage_b, 0)


def _hyena_filter_spectra(p, seq):
    fa, fb, _, _, n2 = _dft_tables(seq)
    nj = seq // FFT_N1
    pitch = n2 + SA_PITCH_PAD
    feats, dist, deltas = _filter_consts(seq)
    nsl = HY_WIDTH // LANES

    def padk(w, rows):
        return jnp.zeros((rows, w.shape[1]), F32).at[:w.shape[0]].set(w)

    def padn(v):
        return jnp.zeros((1, LANES), F32).at[0, :v.shape[0]].set(v)

    w1 = jnp.zeros((LANES, LANES), F32).at[:HY_EMB, :HY_FFN].set(p['hy_ffn_w1'])
    w2 = jnp.zeros((LANES, LANES), F32).at[:HY_FFN, :HY_FFN].set(p['hy_ffn_w2'])
    w3 = padk(p['hy_ffn_w3'], LANES).reshape(LANES, 2, nsl, LANES).transpose(1, 2, 0, 3)
    fa_real = jnp.asarray(fa[:, :, :nj]).astype(BF16)
    const = lambda shape: pl.BlockSpec(shape, lambda o, s: (0,) * len(shape))
    return pl.pallas_call(
        functools.partial(_filter_kernel, seq=seq, n2=n2, pitch=pitch),
        grid=(2, nsl),
        in_specs=[const((seq, LANES)), const((LANES, LANES)), const((1, LANES)), const((1, LANES)),
                  const((LANES, LANES)), const((1, LANES)), const((1, LANES)),
                  pl.BlockSpec((1, 1, LANES, LANES), lambda o, s: (o, s, 0, 0)),
                  const((seq, 1)),
                  pl.BlockSpec((1, 1, LANES), lambda o, s: (s, 0, 0)),
                  const((FFT_N1, 2 * n2, nj)), const((2 * FFT_N1, 2 * FFT_N1))],
        out_specs=pl.BlockSpec((1, 1, n2, 2 * FFT_N1, LANES), lambda o, s: (o, s, 0, 0, 0)),
        out_shape=jax.ShapeDtypeStruct((2, nsl, n2, 2 * FFT_N1, LANES), BF16),
        scratch_shapes=[pltpu.VMEM((seq, LANES), F32), pltpu.VMEM((seq, LANES), F32),
                        pltpu.VMEM((FFT_N1 * pitch, LANES), F32), pltpu.VMEM((FFT_N1 * pitch, LANES), F32)],
        compiler_params=_cparams(("arbitrary", "arbitrary")),
    )(jnp.asarray(feats), w1, padn(p['hy_ffn_b1']), padn(p['hy_freq1']), w2, padn(p['hy_ffn_b2']),
      padn(p['hy_freq2']), w3, jnp.asarray(dist), jnp.asarray(deltas).reshape(nsl, 1, LANES),
      fa_real, jnp.asarray(fb).astype(BF16))


def _longconv_kernel(x_ref, g_ref, skip_ref, h_ref, fa_ref, fb_ref, fbi_ref, fai_ref, o_ref,
                     sar_ref, sai_ref, sbr_ref, sbi_ref, *, seq, n2, pitch, natural_out):
    nj = seq // FFT_N1

    def stage_a(n1, c):
        y = jnp.dot(fa_ref[n1], x_ref[0, n1].astype(BF16), preferred_element_type=F32)
        base = pl.multiple_of(n1 * pitch, 8)
        sar_ref[pl.ds(base, n2), :] = y[:n2]
        sai_ref[pl.ds(base, n2), :] = y[n2:]
        return c

    lax.fori_loop(0, FFT_N1, stage_a, 0)

    def stage_b(k2, c):
        yr = sar_ref[pl.ds(k2, FFT_N1, stride=pitch), :]
        yi = sai_ref[pl.ds(k2, FFT_N1, stride=pitch), :]
        xf = jnp.dot(fb_ref[...], jnp.concatenate([yr, yi], axis=0).astype(BF16), preferred_element_type=F32)
        xr, xi = xf[:FFT_N1], xf[FFT_N1:]
        hf = h_ref[0, k2].astype(F32)
        hr, hi = hf[:FFT_N1], hf[FFT_N1:]
        z = jnp.concatenate([xr * hr - xi * hi, xr * hi + xi * hr], axis=0).astype(BF16)
        w = jnp.dot(fbi_ref[...], z, preferred_element_type=F32)
        base = pl.multiple_of(k2 * SB_PITCH, 8)
        sbr_ref[pl.ds(base, FFT_N1), :] = w[:FFT_N1]
        sbi_ref[pl.ds(base, FFT_N1), :] = w[FFT_N1:]
        return c

    lax.fori_loop(0, n2, stage_b, 0)

    skip = skip_ref[0]

    def stage_c(n1, c):
        wr = sbr_ref[pl.ds(n1, n2, stride=SB_PITCH), :]
        wi = sbi_ref[pl.ds(n1, n2, stride=SB_PITCH), :]
        y = jnp.dot(fai_ref[n1], jnp.concatenate([wr, wi], axis=0).astype(BF16), preferred_element_type=F32)
        res = g_ref[0, n1] * (y + x_ref[0, n1] * skip)
        if natural_out:
            for b in range(res.shape[0] // nj):
                o_ref[0, pl.ds(n1 + b * seq, nj, stride=FFT_N1), :] = res[b * nj:(b + 1) * nj]
        else:
            o_ref[0, n1] = res
        return c

    lax.fori_loop(0, FFT_N1, stage_c, 0)


def _longconv(xg, gg, slab_x, slab_g, skip, hspec, order, seq, natural_out):
    fa, fb, fbi, fai, n2 = _dft_tables(seq)
    pitch = n2 + SA_PITCH_PAD
    nsl = HY_WIDTH // LANES
    rows = xg.shape[2]
    nb = rows // (seq // FFT_N1)
    const = lambda shape: pl.BlockSpec(shape, lambda s: (0,) * len(shape))
    if natural_out:
        out_shape = jax.ShapeDtypeStruct((nsl, nb * seq, LANES), F32)
        out_spec = pl.BlockSpec((1, nb * seq, LANES), lambda s: (s, 0, 0))
    else:
        out_shape = jax.ShapeDtypeStruct((nsl, FFT_N1, rows, LANES), F32)
        out_spec = pl.BlockSpec((1, FFT_N1, rows, LANES), lambda s: (s, 0, 0, 0))
    return pl.pallas_call(
        functools.partial(_longconv_kernel, seq=seq, n2=n2, pitch=pitch, natural_out=natural_out),
        grid=(nsl,),
        in_specs=[pl.BlockSpec((1, FFT_N1, rows, LANES), lambda s: (slab_x + s, 0, 0, 0)),
                  pl.BlockSpec((1, FFT_N1, rows, LANES), lambda s: (slab_g + s, 0, 0, 0)),
                  pl.BlockSpec((1, 1, LANES), lambda s: (s, 0, 0)),
                  pl.BlockSpec((None, 1, n2, 2 * FFT_N1, LANES), lambda s: (order, s, 0, 0, 0)),
                  const(fa.shape), const(fb.shape), const(fbi.shape), const(fai.shape)],
        out_specs=out_spec,
        out_shape=out_shape,
        scratch_shapes=[pltpu.VMEM((FFT_N1 * pitch, LANES), F32), pltpu.VMEM((FFT_N1 * pitch, LANES), F32),
                        pltpu.VMEM((n2 * SB_PITCH, LANES), F32), pltpu.VMEM((n2 * SB_PITCH, LANES), F32)],
        compiler_params=_cparams(("arbitrary",)),
    )(xg, gg, skip.reshape(nsl, 1, LANES), hspec,
      jnp.asarray(fa).astype(BF16), jnp.asarray(fb).astype(BF16),
      jnp.asarray(fbi).astype(BF16), jnp.asarray(fai).astype(BF16))


def _dot_exact_lhs(lhs_bf16, x):
    hi = x.astype(BF16)
    r1 = x - hi.astype(F32)
    mid = r1.astype(BF16)
    lo = (r1 - mid.astype(F32)).astype(BF16)
    d = functools.partial(jnp.dot, preferred_element_type=F32)
    return d(lhs_bf16, hi) + (d(lhs_bf16, mid) + d(lhs_bf16, lo))


def _ssd_kernel(x_ref, b_ref, c_ref, dt_ref, bias_ref, alog_ref, init_ref, y_ref, fin_ref, st_ref, *, rev, nc):
    ci = pl.program_id(1)
    half = LANES // 2

    @pl.when(ci == 0)
    def _():
        st_ref[...] = init_ref[0]

    off = SSD_HEADS if rev else 0
    dt_all = _softplus(dt_ref[...] + bias_ref[...])
    a_all = dt_all * (-jnp.exp(alog_ref[...]))
    r_i = lax.broadcasted_iota(jnp.int32, (SSD_CHUNK, SSD_CHUNK), 0)
    c_i = lax.broadcasted_iota(jnp.int32, (SSD_CHUNK, SSD_CHUNK), 1)
    valid = (c_i >= r_i) if rev else (c_i <= r_i)
    tri = jnp.where(valid, 1.0, 0.0).astype(BF16)
    acum = _dot_exact_lhs(tri, a_all)
    acum_t = acum.T
    tot = acum[0:1, :] if rev else acum[SSD_CHUNK - 1:SSD_CHUNK, :]
    dte = jnp.exp(tot - acum)
    eac = jnp.exp(acum)
    cdr = jnp.exp(tot)
    lo_half = c_i < half
    lo_row = lo_half[0:1]

    for g in range(SSD_GROUPS):
        bt = b_ref[g].T.astype(BF16)
        cg = c_ref[g].astype(BF16)
        cb = jnp.dot(cg, bt, preferred_element_type=F32)
        sg = st_ref[g]
        yoff = jnp.dot(cg, sg.astype(BF16), preferred_element_type=F32)
        ys, ws, cds = [], [], []
        for sl in range(4):
            s = 4 * g + sl
            h0 = off + 2 * s

            def fac(arr):
                return jnp.where(lo_half, arr[:, h0:h0 + 1], arr[:, h0 + 1:h0 + 2])

            xdt = x_ref[s] * fac(dt_all)
            xdt_b = xdt.astype(BF16)
            ws.append((xdt * fac(dte)).astype(BF16))
            cds.append(jnp.where(lo_row, cdr[:, h0:h0 + 1], cdr[:, h0 + 1:h0 + 2]))
            yd = []
            for hh in range(2):
                h = h0 + hh
                seg = acum[:, h:h + 1] - acum_t[h:h + 1, :]
                lmat = jnp.where(valid, jnp.exp(jnp.minimum(seg, 0.0)), 0.0)
                m = (cb * lmat).astype(BF16)
                yd.append(jnp.dot(m, xdt_b[:, hh * half:(hh + 1) * half], preferred_element_type=F32))
            ys.append(jnp.concatenate(yd, axis=1) + yoff[:, sl * LANES:(sl + 1) * LANES] * fac(eac))
        y_ref[g] = jnp.concatenate(ys, axis=1)
        st_ref[g] = sg * jnp.concatenate(cds, axis=1) + jnp.dot(bt, jnp.concatenate(ws, axis=1),
                                                                  preferred_element_type=F32)

    @pl.when(ci == nc - 1)
    def _():
        fin_ref[0] = st_ref[...]


def _ssd_scan(xbc, slab_x, slab_b, slab_c, dt, dt_bias, a_log, init, seq, rev):
    rows = xbc.shape[1]
    nb = rows // seq
    nc = seq // SSD_CHUNK
    gw = SSD_HPG * SSD_HEAD_DIM
    xs = SSD_INNER // LANES
    cpos = (lambda b, c: b * nc + (nc - 1 - c)) if rev else (lambda b, c: b * nc + c)
    return pl.pallas_call(
        functools.partial(_ssd_kernel, rev=rev, nc=nc),
        grid=(nb, nc),
        in_specs=[pl.BlockSpec((xs, SSD_CHUNK, LANES), lambda b, c: (slab_x // xs, cpos(b, c), 0)),
                  pl.BlockSpec((SSD_GROUPS, SSD_CHUNK, LANES), lambda b, c: (slab_b // SSD_GROUPS, cpos(b, c), 0)),
                  pl.BlockSpec((SSD_GROUPS, SSD_CHUNK, LANES), lambda b, c: (slab_c // SSD_GROUPS, cpos(b, c), 0)),
                  pl.BlockSpec((SSD_CHUNK, LANES), lambda b, c: (cpos(b, c), 0)),
                  pl.BlockSpec((1, LANES), lambda b, c: (0, 0)),
                  pl.BlockSpec((1, LANES), lambda b, c: (0, 0)),
                  pl.BlockSpec((1, SSD_GROUPS, SSD_STATE, gw), lambda b, c: (b, 0, 0, 0))],
        out_specs=[pl.BlockSpec((SSD_GROUPS, SSD_CHUNK, gw), lambda b, c: (0, cpos(b, c), 0)),
                   pl.BlockSpec((1, SSD_GROUPS, SSD_STATE, gw), lambda b, c: (b, 0, 0, 0))],
        out_shape=[jax.ShapeDtypeStruct((SSD_GROUPS, rows, gw), F32),
                   jax.ShapeDtypeStruct((nb, SSD_GROUPS, SSD_STATE, gw), F32)],
        scratch_shapes=[pltpu.VMEM((SSD_GROUPS, SSD_STATE, gw), F32)],
        compiler_params=_cparams(("arbitrary", "arbitrary")),
    )(xbc, xbc, xbc, dt, dt_bias.reshape(1, LANES), a_log.reshape(1, LANES), init)


def _ssdnorm_kernel(yf_ref, yb_ref, x_ref, z_ref, d_ref, w_ref, o_ref):
    nsl = x_ref.shape[0]
    x = jnp.concatenate([x_ref[i] for i in range(nsl)], axis=1)
    z = jnp.concatenate([z_ref[i] for i in range(nsl)], axis=1)
    y = yf_ref[0] + yb_ref[0] + x * d_ref[0]
    gx = y * _silu(z)
    ms = jnp.mean(gx * gx, axis=-1, keepdims=True)
    o_ref[...] = (gx * lax.rsqrt(ms + 1e-5) * w_ref[0]).astype(o_ref.dtype)


def _ssd_norm(yf, yb, xbc, proj, slab_z, d_ch, w, tr):
    g, rows, gw = yf.shape
    nsl = gw // LANES
    return pl.pallas_call(
        _ssdnorm_kernel,
        grid=(g, rows // tr),
        in_specs=[pl.BlockSpec((1, tr, gw), lambda q, i: (q, i, 0)),
                  pl.BlockSpec((1, tr, gw), lambda q, i: (q, i, 0)),
                  pl.BlockSpec((nsl, tr, LANES), lambda q, i: (q, i, 0)),
                  pl.BlockSpec((nsl, tr, LANES), lambda q, i: (slab_z // nsl + q, i, 0)),
                  pl.BlockSpec((1, 1, gw), lambda q, i: (q, 0, 0)),
                  pl.BlockSpec((1, 1, gw), lambda q, i: (q, 0, 0))],
        out_specs=pl.BlockSpec((tr, gw), lambda q, i: (i, q)),
        out_shape=jax.ShapeDtypeStruct((rows, g * gw), BF16),
        compiler_params=_cparams(("arbitrary", "arbitrary")),
    )(yf, yb, xbc, proj, d_ch.reshape(g, 1, gw), w.reshape(g, 1, gw))


def _merge_kernel(yhy_ref, yss_ref, whh_ref, wss_ref, gh_ref, gs_ref, o_ref):
    a1 = jnp.concatenate([yhy_ref[i] for i in range(yhy_ref.shape[0])], axis=1).astype(BF16)
    p1 = jnp.dot(a1, whh_ref[...], preferred_element_type=F32)
    p2 = jnp.dot(yss_ref[...], wss_ref[...], preferred_element_type=F32)
    gh = jnp.concatenate([gh_ref[i] for i in range(gh_ref.shape[0])], axis=1)
    gs = jnp.concatenate([gs_ref[i] for i in range(gs_ref.shape[0])], axis=1)
    o_ref[...] = (_sigmoid(gh) * p1 + _sigmoid(gs) * p2).astype(o_ref.dtype)


def _branch_merge(yhy, yss, w_hy, w_ssd, proj, slab_gh, slab_gs, tm, tn):
    nsl_k = yhy.shape[0]
    rows = yhy.shape[1]
    n = w_hy.shape[1]
    ts = tn // LANES
    return pl.pallas_call(
        _merge_kernel,
        grid=(rows // tm, n // tn),
        in_specs=[pl.BlockSpec((nsl_k, tm, LANES), lambda i, j: (0, i, 0)),
                  pl.BlockSpec((tm, yss.shape[1]), lambda i, j: (i, 0)),
                  pl.BlockSpec((w_hy.shape[0], tn), lambda i, j: (0, j)),
                  pl.BlockSpec((w_ssd.shape[0], tn), lambda i, j: (0, j)),
                  pl.BlockSpec((ts, tm, LANES), lambda i, j: (slab_gh // ts + j, i, 0)),
                  pl.BlockSpec((ts, tm, LANES), lambda i, j: (slab_gs // ts + j, i, 0))],
        out_specs=pl.BlockSpec((tm, tn), lambda i, j: (i, j)),
        out_shape=jax.ShapeDtypeStruct((rows, n), BF16),
        compiler_params=_cparams(("arbitrary", "arbitrary")),
    )(yhy, yss, w_hy, w_ssd, proj, proj)


def _outproj_ln_kernel(a_ref, w_ref, x_ref, gt_ref, g_ref, b_ref, sc_ref, sh_ref, h_ref, m_ref):
    y = jnp.dot(a_ref[0], w_ref[...], preferred_element_type=F32)
    h = _normalize_rows(DEEPNORM_ALPHA * x_ref[0] + gt_ref[0] * y, 1e-6) * g_ref[...] + b_ref[...]
    h_ref[0] = h
    m_ref[0] = (_normalize_rows(h, 1e-6) * (1.0 + sc_ref[0]) + sh_ref[0]).astype(m_ref.dtype)


def _outproj_ln(merged, w_out, x, gate, ln_g, ln_b, scale2, shift2, tm):
    b, l, d = x.shape
    vec = pl.BlockSpec((1, 1, d), lambda i, j: (i, 0, 0))
    cvec = pl.BlockSpec((1, d), lambda i, j: (0, 0))
    tile = pl.BlockSpec((1, tm, d), lambda i, j: (i, j, 0))
    return pl.pallas_call(
        _outproj_ln_kernel,
        grid=(b, l // tm),
        in_specs=[tile, pl.BlockSpec((d, d), lambda i, j: (0, 0)), tile, vec, cvec, cvec, vec, vec],
        out_specs=[tile, tile],
        out_shape=[jax.ShapeDtypeStruct((b, l, d), F32), jax.ShapeDtypeStruct((b, l, d), BF16)],
        compiler_params=_cparams(("arbitrary", "arbitrary")),
    )(merged, w_out, x, gate, ln_g.reshape(1, d), ln_b.reshape(1, d), scale2, shift2)


def _resid_ln_kernel(x_ref, y_ref, gt_ref, g_ref, b_ref, o_ref):
    o_ref[0] = _normalize_rows(DEEPNORM_ALPHA * x_ref[0] + gt_ref[0] * y_ref[0], 1e-6) * g_ref[...] + b_ref[...]


def _resid_ln(x, y, gate, ln_g, ln_b, tm):
    b, l, d = x.shape
    tile = pl.BlockSpec((1, tm, d), lambda i, j: (i, j, 0))
    return pl.pallas_call(
        _resid_ln_kernel,
        grid=(b, l // tm),
        in_specs=[tile, tile, pl.BlockSpec((1, 1, d), lambda i, j: (i, 0, 0)),
                  pl.BlockSpec((1, d), lambda i, j: (0, 0)), pl.BlockSpec((1, d), lambda i, j: (0, 0))],
        out_specs=tile,
        out_shape=jax.ShapeDtypeStruct((b, l, d), F32),
        compiler_params=_cparams(("arbitrary", "arbitrary")),
    )(x, y, gate, ln_g.reshape(1, d), ln_b.reshape(1, d))


def _peer_scores_kernel(a_ref, wq_ref, k_ref, s_ref):
    q = jnp.dot(a_ref[...], wq_ref[...], preferred_element_type=F32)
    hd = PEER_DKEY // 2
    nt = (((1,), (1,)), ((), ()))
    for h in range(PEER_HEADS):
        qn = _normalize_rows(q[:, h * PEER_DKEY:(h + 1) * PEER_DKEY], 1e-6).astype(BF16)
        s_ref[h, 0] = lax.dot_general(k_ref[h, 0], qn[:, :hd], nt, preferred_element_type=F32)
        s_ref[h, 1] = lax.dot_general(k_ref[h, 1], qn[:, hd:], nt, preferred_element_type=F32)


def _peer_scores(m2, wq, subkeys, tm):
    t, d = m2.shape
    return pl.pallas_call(
        _peer_scores_kernel,
        grid=(t // tm,),
        in_specs=[pl.BlockSpec((tm, d), lambda i: (i, 0)),
                  pl.BlockSpec(wq.shape, lambda i: (0, 0)),
                  pl.BlockSpec(subkeys.shape, lambda i: (0, 0, 0, 0))],
        out_specs=pl.BlockSpec((PEER_HEADS, 2, PEER_NKEYS, tm), lambda i: (0, 0, 0, i)),
        out_shape=jax.ShapeDtypeStruct((PEER_HEADS, 2, PEER_NKEYS, t), F32),
        compiler_params=_cparams(("arbitrary",)),
    )(m2, wq, subkeys)


def _top_values(x, k):
    out = []
    for _ in range(k):
        m = jnp.max(x, axis=0, keepdims=True)
        out.append(m)
        x = jnp.where(x == m, -jnp.inf, x)
    return out


def _peer_stats_kernel(s_ref, p1_ref, p2_ref, tau_ref):
    taus = []
    for h in range(PEER_HEADS):
        s1 = s_ref[h, 0]
        s2 = s_ref[h, 1]
        v1 = _top_values(s1, PEER_TOPK)
        v2 = _top_values(s2, PEER_TOPK)
        v2m = jnp.concatenate(v2, axis=0)
        cand = jnp.concatenate([v1[a] + v2m for a in range(PEER_TOPK)], axis=0)
        best = _top_values(cand, PEER_TOPK)
        z = best[0] * 0.0
        for c in best:
            z = z + jnp.exp(c - best[0])
        taus.append(best[PEER_TOPK - 1])
        p1_ref[h] = jnp.exp(s1 - v1[0])
        p2_ref[h] = jnp.exp(s2 - v2[0]) / z
    tau_ref[...] = jnp.concatenate(taus, axis=0)


def _peer_stats(scores, tm):
    t = scores.shape[-1]
    hk = pl.BlockSpec((PEER_HEADS, PEER_NKEYS, tm), lambda i: (0, 0, i))
    return pl.pallas_call(
        _peer_stats_kernel,
        grid=(t // tm,),
        in_specs=[pl.BlockSpec((PEER_HEADS, 2, PEER_NKEYS, tm), lambda i: (0, 0, 0, i))],
        out_specs=[hk, hk, pl.BlockSpec((PEER_HEADS, tm), lambda i: (0, i))],
        out_shape=[jax.ShapeDtypeStruct((PEER_HEADS, PEER_NKEYS, t), F32),
                   jax.ShapeDtypeStruct((PEER_HEADS, PEER_NKEYS, t), F32),
                   jax.ShapeDtypeStruct((PEER_HEADS, t), F32)],
        compiler_params=_cparams(("arbitrary",)),
    )(scores)


def _peer_dense_kernel(h_ref, u_ref, vt_ref, s_ref, p1_ref, p2_ref, tau_ref, o_ref, acc_ref, *, ne1):
    j = pl.program_id(1)

    @pl.when(j == 0)
    def _():
        acc_ref[...] = jnp.zeros_like(acc_ref)

    nt = (((1,), (1,)), ((), ()))
    act = _gelu_tanh(lax.dot_general(u_ref[...], h_ref[...], nt, preferred_element_type=F32))
    parts = []
    for r in range(ne1):
        e1 = j * ne1 + r
        gacc = None
        for h in range(PEER_HEADS):
            c = s_ref[h, 1] + s_ref[h, 0, pl.ds(e1, 1), :]
            w = jnp.where(c >= tau_ref[h:h + 1, :], p2_ref[h] * p1_ref[h, pl.ds(e1, 1), :], 0.0)
            gacc = w if gacc is None else gacc + w
        parts.append((act[r * PEER_NKEYS:(r + 1) * PEER_NKEYS] * gacc).astype(BF16))
    pt = jnp.concatenate(parts, axis=0)
    acc_ref[...] += jnp.dot(vt_ref[...], pt, preferred_element_type=F32)

    @pl.when(j == pl.num_programs(1) - 1)
    def _():
        o_ref[...] = acc_ref[...].T


def _peer_dense(m2, u_b, vt_b, scores, p1, p2, tau, tm, te):
    t, d = m2.shape
    ne = u_b.shape[0]
    hk = pl.BlockSpec((PEER_HEADS, PEER_NKEYS, tm), lambda i, j: (0, 0, i))
    return pl.pallas_call(
        functools.partial(_peer_dense_kernel, ne1=te // PEER_NKEYS),
        grid=(t // tm, ne // te),
        in_specs=[pl.BlockSpec((tm, d), lambda i, j: (i, 0)),
                  pl.BlockSpec((te, d), lambda i, j: (j, 0)),
                  pl.BlockSpec((d, te), lambda i, j: (0, j)),
                  pl.BlockSpec((PEER_HEADS, 2, PEER_NKEYS, tm), lambda i, j: (0, 0, 0, i)),
                  hk, hk, pl.BlockSpec((PEER_HEADS, tm), lambda i, j: (0, i))],
        out_specs=pl.BlockSpec((tm, d), lambda i, j: (i, 0)),
        out_shape=jax.ShapeDtypeStruct((t, d), F32),
        scratch_shapes=[pltpu.VMEM((d, tm), F32)],
        compiler_params=_cparams(("arbitrary", "arbitrary")),
    )(m2, u_b, vt_b, scores, p1, p2, tau)


SL_HY = 0
SL_X = OFF_X // LANES
SL_B = OFF_B // LANES
SL_C = OFF_C // LANES
SL_Z = OFF_DT // LANES
SL_GH = SL_Z + SSD_INNER // LANES
SL_GS = SL_GH + D_MODEL // LANES


def kernel(x, c, ctx, c_ctx, ada_w, ada_b, w_in, hy_conv_w, hy_conv_b, hy_ffn_w1, hy_ffn_b1, hy_freq1,
           hy_ffn_w2, hy_ffn_b2, hy_freq2, hy_ffn_w3, hy_skip, ssd_conv_w, ssd_conv_b, ssd_a_log,
           ssd_dt_bias, ssd_d, ssd_norm_w, w_branch_hy, w_branch_ssd, w_out, ln1_g, ln1_b, peer_wq,
           peer_subkeys, peer_u, peer_v, ln2_g, ln2_b):
    nb, seq, d = x.shape
    lc = ctx.shape[1]
    t = nb * seq
    p = dict(hy_ffn_w1=hy_ffn_w1[0], hy_ffn_b1=hy_ffn_b1[0], hy_freq1=hy_freq1[0], hy_ffn_w2=hy_ffn_w2[0],
             hy_ffn_b2=hy_ffn_b2[0], hy_freq2=hy_freq2[0], hy_ffn_w3=hy_ffn_w3[0])

    cc = jnp.zeros((8, d), F32).at[:nb].set(c).at[nb].set(c_ctx)
    mod = _ada(cc, ada_w[0], ada_b[0])
    sh1, sc1, gt1, sh2, sc2, gt2 = [mod[:nb, i * d:(i + 1) * d].reshape(nb, 1, d) for i in range(6)]
    csh1 = mod[nb:nb + 1, 0:d].reshape(1, 1, d)
    csc1 = mod[nb:nb + 1, d:2 * d].reshape(1, 1, d)

    w = w_in[0]
    w_main = jnp.concatenate([w[:, :OFF_DT], w[:, OFF_Z:]], axis=1).astype(BF16)
    w_dt = w[:, OFF_DT:OFF_Z].astype(BF16)
    w_ctx = w[:, OFF_X:OFF_C].astype(BF16)

    m_ctx = _modulate(ctx, csc1, csh1, 256).reshape(nb * lc, d)
    pc = _matmul(m_ctx, w_ctx, 512, 1024, slab=True)
    dtc = _matmul(m_ctx, w_dt, 512, LANES)
    nxb = (OFF_C - OFF_X) // LANES
    cw = ssd_conv_w[0].reshape(3, -1, LANES).transpose(1, 0, 2)
    cbias = ssd_conv_b[0].reshape(-1, 1, LANES)
    xbc_ctx = _conv3(pc, 0, nxb, cw[:nxb], cbias[:nxb], lc, True)
    gw = SSD_HPG * SSD_HEAD_DIM
    zero_state = jnp.zeros((nb, SSD_GROUPS, SSD_STATE, gw), F32)
    dt_bias = ssd_dt_bias[0]
    a_log = ssd_a_log[0]
    xs_sl = SSD_INNER // LANES
    _, s_f = _ssd_scan(xbc_ctx, 0, xs_sl, xs_sl, dtc, dt_bias, a_log, zero_state, lc, False)
    _, s_b = _ssd_scan(xbc_ctx, 0, xs_sl, xs_sl, dtc, dt_bias, a_log, zero_state, lc, True)

    m1 = _modulate(x, sc1, sh1, 256).reshape(t, d)
    proj = _matmul(m1, w_main, 1024, 1024, slab=True)
    dtp = _matmul(m1, w_dt, 1024, LANES)

    hw = hy_conv_w[0].reshape(3, -1, LANES).transpose(1, 0, 2)
    hb = hy_conv_b[0].reshape(-1, 1, LANES)
    nh = HY_WIDTH // LANES
    ug = _hyconv(proj, 3 * nh, hw, hb, seq)
    hspec = _hyena_filter_spectra(p, seq)
    zg = _longconv(ug, ug, 0, nh, hy_skip[0, 0], hspec, 0, seq, False)
    y_hy = _longconv(zg, ug, 0, 2 * nh, hy_skip[0, 1], hspec, 1, seq, True)

    xbc = _conv3(proj, SL_X, (OFF_DT - OFF_X) // LANES, cw, cbias, seq, True)
    y_f, _ = _ssd_scan(xbc, 0, xs_sl, xs_sl + SSD_GROUPS, dtp, dt_bias, a_log, s_f, seq, False)
    y_b, _ = _ssd_scan(xbc, 0, xs_sl, xs_sl + SSD_GROUPS, dtp, dt_bias, a_log, s_b, seq, True)
    d_ch = jnp.repeat(ssd_d[0, 0] + ssd_d[0, 1], SSD_HEAD_DIM)
    y_ss = _ssd_norm(y_f, y_b, xbc, proj, SL_Z, d_ch, ssd_norm_w[0], 512)

    merged = _branch_merge(y_hy, y_ss, w_branch_hy[0].astype(BF16), w_branch_ssd[0].astype(BF16),
                           proj, SL_GH, SL_GS, 512, 512)
    h1, m2 = _outproj_ln(merged.reshape(nb, seq, d), w_out[0].astype(BF16), x, gt1, ln1_g[0], ln1_b[0],
                         sc2, sh2, 256)

    m2f = m2.reshape(t, d)
    scores = _peer_scores(m2f, peer_wq[0].astype(BF16), peer_subkeys[0].astype(BF16), 256)
    p1, p2, tau = _peer_stats(scores, 256)
    ffn = _peer_dense(m2f, peer_u[0].astype(BF16), peer_v[0].T.astype(BF16), scores, p1, p2, tau, 512, 256)
    return _resid_ln(h1, ffn.reshape(nb, seq, d), gt2, ln2_g[0], ln2_b[0], 256)
```

```python
import functools
import math

import numpy as np
import jax
import jax.numpy as jnp
from jax import lax
from jax.experimental import pallas as pl
from jax.experimental.pallas import tpu as pltpu

F32 = jnp.float32
BF16 = jnp.bfloat16

D_MODEL = 2048
DEPTH = 1
CTX_LEN = 256
HY_WIDTH = D_MODEL
HY_BANDS = 16
HY_EMB = 2 * HY_BANDS + 1
HY_FFN = 64
HY_MIN_DECAY = math.log(1e-2) / 1.5
HY_MAX_DECAY = math.log(1e-2) / 0.3
SSD_INNER = 2 * D_MODEL
SSD_HEAD_DIM = 64
SSD_HEADS = SSD_INNER // SSD_HEAD_DIM
SSD_STATE = 128
SSD_GROUPS = 8
SSD_HPG = SSD_HEADS // SSD_GROUPS
SSD_CHUNK = 128
OFF_X = 3 * HY_WIDTH
OFF_B = OFF_X + SSD_INNER
OFF_C = OFF_B + SSD_GROUPS * SSD_STATE
OFF_DT = OFF_C + SSD_GROUPS * SSD_STATE
OFF_Z = OFF_DT + 2 * SSD_HEADS
OFF_GATE = OFF_Z + SSD_INNER
N_COLS = OFF_GATE + 2 * D_MODEL
PEER_HEADS = 8
PEER_NKEYS = 128
PEER_EXPERTS = PEER_NKEYS * PEER_NKEYS
PEER_TOPK = 16
PEER_DKEY = 256
DEEPNORM_ALPHA = (2.0 * DEPTH) ** 0.25

LANES = 128
VMEM_LIMIT = 56 * 1024 * 1024

FFT_N1 = 128
SA_PITCH_PAD = 8
SB_PITCH = FFT_N1 + 8


def _cparams(sem, vmem=VMEM_LIMIT):
    return pltpu.CompilerParams(dimension_semantics=sem, vmem_limit_bytes=vmem)


def _silu(x):
    return x * (1.0 / (1.0 + jnp.exp(-x)))


def _sigmoid(x):
    return 1.0 / (1.0 + jnp.exp(-x))


def _softplus(x):
    return jnp.maximum(x, 0.0) + jnp.log(1.0 + jnp.exp(-jnp.abs(x)))


def _gelu_tanh(x):
    return 0.5 * x * (1.0 + jnp.tanh(math.sqrt(2.0 / math.pi) * (x + 0.044715 * (x * x * x))))


def _normalize_rows(x, eps):
    mu = jnp.mean(x, axis=-1, keepdims=True)
    xc = x - mu
    var = jnp.mean(xc * xc, axis=-1, keepdims=True)
    return xc * lax.rsqrt(var + eps)


def _ada_kernel(c_ref, w_ref, b_ref, o_ref):
    a = _silu(c_ref[...]).astype(BF16)
    o_ref[...] = jnp.dot(a, w_ref[...].astype(BF16), preferred_element_type=F32) + b_ref[...]


def _ada(cc, ada_w, ada_b):
    rows, d = cc.shape
    n = ada_w.shape[1]
    tn = 1024
    return pl.pallas_call(
        _ada_kernel,
        grid=(n // tn,),
        in_specs=[pl.BlockSpec((rows, d), lambda j: (0, 0)),
                  pl.BlockSpec((d, tn), lambda j: (0, j)),
                  pl.BlockSpec((1, tn), lambda j: (0, j))],
        out_specs=pl.BlockSpec((rows, tn), lambda j: (0, j)),
        out_shape=jax.ShapeDtypeStruct((rows, n), F32),
        compiler_params=_cparams(("arbitrary",)),
    )(cc, ada_w, ada_b.reshape(1, n))


def _mod_kernel(x_ref, sc_ref, sh_ref, o_ref):
    xn = _normalize_rows(x_ref[0], 1e-6)
    o_ref[0] = (xn * (1.0 + sc_ref[0]) + sh_ref[0]).astype(o_ref.dtype)


def _modulate(x, scale, shift, tr):
    b, l, d = x.shape
    per_batch = scale.shape[0] == b
    smap = (lambda i, j: (i, 0, 0)) if per_batch else (lambda i, j: (0, 0, 0))
    return pl.pallas_call(
        _mod_kernel,
        grid=(b, l // tr),
        in_specs=[pl.BlockSpec((1, tr, d), lambda i, j: (i, j, 0)),
                  pl.BlockSpec((1, 1, d), smap),
                  pl.BlockSpec((1, 1, d), smap)],
        out_specs=pl.BlockSpec((1, tr, d), lambda i, j: (i, j, 0)),
        out_shape=jax.ShapeDtypeStruct((b, l, d), BF16),
        compiler_params=_cparams(("arbitrary", "arbitrary")),
    )(x, scale, shift)


def _mm_kernel(a_ref, b_ref, o_ref, *, slab):
    acc = jnp.dot(a_ref[...], b_ref[...], preferred_element_type=F32)
    if slab:
        for s in range(o_ref.shape[0]):
            o_ref[s] = acc[:, s * LANES:(s + 1) * LANES].astype(o_ref.dtype)
    else:
        o_ref[...] = acc.astype(o_ref.dtype)


def _matmul(a, b, tm, tn, out_dtype=F32, slab=False):
    m, k = a.shape
    n = b.shape[1]
    if slab:
        out_shape = jax.ShapeDtypeStruct((n // LANES, m, LANES), out_dtype)
        out_spec = pl.BlockSpec((tn // LANES, tm, LANES), lambda i, j: (j, i, 0))
    else:
        out_shape = jax.ShapeDtypeStruct((m, n), out_dtype)
        out_spec = pl.BlockSpec((tm, tn), lambda i, j: (i, j))
    return pl.pallas_call(
        functools.partial(_mm_kernel, slab=slab),
        grid=(m // tm, n // tn),
        in_specs=[pl.BlockSpec((tm, k), lambda i, j: (i, 0)),
                  pl.BlockSpec((k, tn), lambda i, j: (0, j))],
        out_specs=out_spec,
        out_shape=out_shape,
        compiler_params=_cparams(("arbitrary", "arbitrary")),
    )(a, b)


def _conv3_kernel(x_ref, w_ref, b_ref, o_ref, *, silu):
    x = x_ref[0]
    l = x.shape[0]
    row = lax.broadcasted_iota(jnp.int32, x.shape, 0)
    prev = jnp.where(row == 0, 0.0, pltpu.roll(x, 1, 0))
    nxt = jnp.where(row == l - 1, 0.0, pltpu.roll(x, l - 1, 0))
    w = w_ref[0]
    y = prev * w[0:1] + x * w[1:2] + nxt * w[2:3] + b_ref[0]
    if silu:
        y = _silu(y)
    o_ref[0] = y


def _conv3(p_slab, slab0, nslab, w, bias, seq, silu):
    rows = p_slab.shape[1]
    nb = rows // seq
    return pl.pallas_call(
        functools.partial(_conv3_kernel, silu=silu),
        grid=(nslab, nb),
        in_specs=[pl.BlockSpec((1, seq, LANES), lambda s, b: (slab0 + s, b, 0)),
                  pl.BlockSpec((1, 3, LANES), lambda s, b: (s, 0, 0)),
                  pl.BlockSpec((1, 1, LANES), lambda s, b: (s, 0, 0))],
        out_specs=pl.BlockSpec((1, seq, LANES), lambda s, b: (s, b, 0)),
        out_shape=jax.ShapeDtypeStruct((nslab, rows, LANES), F32),
        compiler_params=_cparams(("arbitrary", "arbitrary")),
    )(p_slab, w, bias)


def _hyconv_kernel(x_ref, w_ref, b_ref, o_ref, *, seq, nb):
    nj = seq // FFT_N1

    def gather(n1):
        return jnp.concatenate(
            [x_ref[0, pl.ds(n1 + b * seq, nj, stride=FFT_N1), :] for b in range(nb)], axis=0)

    w = w_ref[0]
    w0, w1, w2, bias = w[0:1], w[1:2], w[2:3], b_ref[0]
    jrow = lax.broadcasted_iota(jnp.int32, (nb * nj, LANES), 0) % nj

    def shift_down(g):
        return jnp.where(jrow == 0, 0.0, pltpu.roll(g, 1, 0))

    def shift_up(g):
        return jnp.where(jrow == nj - 1, 0.0, pltpu.roll(g, nb * nj - 1, 0))

    g_first = gather(0)
    g_second = gather(1)
    g_last = gather(FFT_N1 - 1)
    o_ref[0, 0] = shift_down(g_last) * w0 + g_first * w1 + g_second * w2 + bias

    def body(n1, carry):
        g_prev, g_cur = carry
        g_next = gather(n1 + 1)
        o_ref[0, n1] = g_prev * w0 + g_cur * w1 + g_next * w2 + bias
        return g_cur, g_next

    g_prev, g_cur = lax.fori_loop(1, FFT_N1 - 1, body, (g_first, g_second), unroll=6)
    o_ref[0, FFT_N1 - 1] = g_prev * w0 + g_cur * w1 + shift_up(g_first) * w2 + bias


def _hyconv(p_slab, nslab, w, bias, seq):
    rows = p_slab.shape[1]
    nb = rows // seq
    nj = seq // FFT_N1
    return pl.pallas_call(
        functools.partial(_hyconv_kernel, seq=seq, nb=nb),
        grid=(nslab,),
        in_specs=[pl.BlockSpec((1, rows, LANES), lambda s: (s, 0, 0)),
                  pl.BlockSpec((1, 3, LANES), lambda s: (s, 0, 0)),
                  pl.BlockSpec((1, 1, LANES), lambda s: (s, 0, 0))],
        out_specs=pl.BlockSpec((1, FFT_N1, nb * nj, LANES), lambda s: (s, 0, 0, 0)),
        out_shape=jax.ShapeDtypeStruct((nslab, FFT_N1, nb * nj, LANES), F32),
        compiler_params=_cparams(("arbitrary",)),
    )(p_slab, w, bias)


@functools.lru_cache(maxsize=None)
def _dft_tables(seq):
    nj = seq // FFT_N1
    n2 = nj + nj // 2
    n = FFT_N1 * n2
    out0 = nj // 2
    n1 = np.arange(FFT_N1, dtype=np.float64)
    k2 = np.arange(n2, dtype=np.float64)
    jn = np.arange(nj, dtype=np.float64)
    ang = -2.0 * np.pi * (k2[None, :, None] * jn[None, None, :] / n2 + n1[:, None, None] * k2[None, :, None] / n)
    cr, ci = np.cos(ang), np.sin(ang)
    fa = np.concatenate([np.concatenate([cr, -ci], axis=2), np.concatenate([ci, cr], axis=2)], axis=1)
    k1 = np.arange(FFT_N1, dtype=np.float64)
    angb = -2.0 * np.pi * np.outer(k1, n1) / FFT_N1
    dr, di = np.cos(angb), np.sin(angb)
    fb = np.block([[dr, -di], [di, dr]])
    fbi = np.block([[dr, di], [-di, dr]])
    angi = 2.0 * np.pi * (n1[:, None, None] * k2[None, None, :] / n + (out0 + jn)[None, :, None] * k2[None, None, :] / n2)
    er, ei = np.cos(angi) / n, np.sin(angi) / n
    fai = np.concatenate([np.concatenate([er, -ei], axis=2), np.concatenate([ei, er], axis=2)], axis=1)
    return (fa.astype(np.float32), fb.astype(np.float32), fbi.astype(np.float32), fai.astype(np.float32), n2)


@functools.lru_cache(maxsize=None)
def _filter_consts(seq):
    t = np.linspace(0.0, 1.0, seq, dtype=np.float32)[:, None].astype(np.float64)
    w = 2.0 * np.pi * np.arange(seq, dtype=np.float32)[:, None].astype(np.float64) / seq
    bands = np.linspace(1e-4, HY_BANDS - 1, HY_BANDS, dtype=np.float32)[None, :].astype(np.float64)
    feats = np.concatenate([t, np.cos(bands * w), -np.sin(bands * w)], axis=-1)
    feats_p = np.zeros((seq, LANES), np.float32)
    feats_p[:, :HY_EMB] = feats
    dist = (np.abs(np.arange(seq) - seq // 2).astype(np.float32) / np.float32(seq / 2.0)).reshape(seq, 1)
    deltas = np.abs(np.linspace(HY_MIN_DECAY, HY_MAX_DECAY, HY_WIDTH, dtype=np.float32))
    return feats_p, dist.astype(np.float32), deltas.astype(np.float32)


def _dot_f32(a, b):
    def split(x):
        hi = x.astype(BF16)
        r1 = x - hi.astype(F32)
        mid = r1.astype(BF16)
        lo = (r1 - mid.astype(F32)).astype(BF16)
        return hi, mid, lo
    a0, a1, a2 = split(a)
    b0, b1, b2 = split(b)
    d = functools.partial(jnp.dot, preferred_element_type=F32)
    return (d(a0, b0) + (d(a0, b1) + d(a1, b0)) + (d(a0, b2) + d(a1, b1) + d(a2, b0)))


def _filter_kernel(feats_ref, w1_ref, b1_ref, f1_ref, w2_ref, b2_ref, f2_ref, w3_ref, dist_ref, delta_ref,
                   fa_ref, fb_ref, o_ref, hid_ref, h_ref, sar_ref, sai_ref, *, seq, n2, pitch):
    nj = seq // FFT_N1

    @pl.when((pl.program_id(0) == 0) & (pl.program_id(1) == 0))
    def _():
        h1 = jnp.sin(f1_ref[...] * (_dot_f32(feats_ref[...], w1_ref[...]) + b1_ref[...]))
        hid_ref[...] = jnp.sin(f2_ref[...] * (_dot_f32(h1, w2_ref[...]) + b2_ref[...])).astype(BF16)

    h = jnp.dot(hid_ref[...], w3_ref[0, 0].astype(BF16), preferred_element_type=F32)
    h = h * jnp.exp(-dist_ref[...] * delta_ref[0])
    h = h / (jnp.sum(jnp.abs(h), axis=0, keepdims=True) + 1e-6)
    h_ref[...] = h

    def stage_a(n1, c):
        g = h_ref[pl.ds(n1, nj, stride=FFT_N1), :].astype(BF16)
        y = jnp.dot(fa_ref[n1], g, preferred_element_type=F32)
        base = pl.multiple_of(n1 * pitch, 8)
        sar_ref[pl.ds(base, n2), :] = y[:n2]
        sai_ref[pl.ds(base, n2), :] = y[n2:]
        return c

    lax.fori_loop(0, FFT_N1, stage_a, 0, unroll=8)

    def stage_b(k2, c):
        yr = sar_ref[pl.ds(k2, FFT_N1, stride=pitch), :]
        yi = sai_ref[pl.ds(k2, FFT_N1, stride=pitch), :]
        op = jnp.concatenate([yr, yi], axis=0).astype(BF16)
        o_ref[0, 0, k2] = jnp.dot(fb_ref[...], op, preferred_element_type=F32).astype(o_ref.dtype)
        return c

    lax.fori_loop(0, n2, stage_b, 0, unroll=4)


def _hyena_filter_spectra(p, seq):
    fa, fb, _, _, n2 = _dft_tables(seq)
    nj = seq // FFT_N1
    pitch = n2 + SA_PITCH_PAD
    feats, dist, deltas = _filter_consts(seq)
    nsl = HY_WIDTH // LANES

    def padk(w, rows):
        return jnp.zeros((rows, w.shape[1]), F32).at[:w.shape[0]].set(w)

    def padn(v):
        return jnp.zeros((1, LANES), F32).at[0, :v.shape[0]].set(v)

    w1 = jnp.zeros((LANES, LANES), F32).at[:HY_EMB, :HY_FFN].set(p['hy_ffn_w1'])
    w2 = jnp.zeros((LANES, LANES), F32).at[:HY_FFN, :HY_FFN].set(p['hy_ffn_w2'])
    w3 = padk(p['hy_ffn_w3'], LANES).reshape(LANES, 2, nsl, LANES).transpose(1, 2, 0, 3)
    fa_real = jnp.asarray(fa[:, :, :nj]).astype(BF16)
    const = lambda shape: pl.BlockSpec(shape, lambda o, s: (0,) * len(shape))
    return pl.pallas_call(
        functools.partial(_filter_kernel, seq=seq, n2=n2, pitch=pitch),
        grid=(2, nsl),
        in_specs=[const((seq, LANES)), const((LANES, LANES)), const((1, LANES)), const((1, LANES)),
                  const((LANES, LANES)), const((1, LANES)), const((1, LANES)),
                  pl.BlockSpec((1, 1, LANES, LANES), lambda o, s: (o, s, 0, 0)),
                  const((seq, 1)),
                  pl.BlockSpec((1, 1, LANES), lambda o, s: (s, 0, 0)),
                  const((FFT_N1, 2 * n2, nj)), const((2 * FFT_N1, 2 * FFT_N1))],
        out_specs=pl.BlockSpec((1, 1, n2, 2 * FFT_N1, LANES), lambda o, s: (o, s, 0, 0, 0)),
        out_shape=jax.ShapeDtypeStruct((2, nsl, n2, 2 * FFT_N1, LANES), BF16),
        scratch_shapes=[pltpu.VMEM((seq, LANES), BF16), pltpu.VMEM((seq, LANES), F32),
                        pltpu.VMEM((FFT_N1 * pitch, LANES), F32), pltpu.VMEM((FFT_N1 * pitch, LANES), F32)],
        compiler_params=_cparams(("arbitrary", "arbitrary")),
    )(jnp.asarray(feats), w1, padn(p['hy_ffn_b1']), padn(p['hy_freq1']), w2, padn(p['hy_ffn_b2']),
      padn(p['hy_freq2']), w3, jnp.asarray(dist), jnp.asarray(deltas).reshape(nsl, 1, LANES),
      fa_real, jnp.asarray(fb).astype(BF16))


def _longconv_kernel(x_ref, g_ref, skip_ref, h_ref, fa_ref, fb_ref, fbi_ref, fai_ref, o_ref,
                     sar_ref, sai_ref, sbr_ref, sbi_ref, *, seq, n2, pitch, natural_out):
    nj = seq // FFT_N1

    def stage_a(n1, c):
        y = jnp.dot(fa_ref[n1], x_ref[0, n1].astype(BF16), preferred_element_type=F32)
        base = pl.multiple_of(n1 * pitch, 8)
        sar_ref[pl.ds(base, n2), :] = y[:n2]
        sai_ref[pl.ds(base, n2), :] = y[n2:]
        return c

    lax.fori_loop(0, FFT_N1, stage_a, 0, unroll=8)

    def stage_b(k2, c):
        yr = sar_ref[pl.ds(k2, FFT_N1, stride=pitch), :]
        yi = sai_ref[pl.ds(k2, FFT_N1, stride=pitch), :]
        xf = jnp.dot(fb_ref[...], jnp.concatenate([yr, yi], axis=0).astype(BF16), preferred_element_type=F32)
        xr, xi = xf[:FFT_N1], xf[FFT_N1:]
        hf = h_ref[0, k2].astype(F32)
        hr, hi = hf[:FFT_N1], hf[FFT_N1:]
        z = jnp.concatenate([xr * hr - xi * hi, xr * hi + xi * hr], axis=0).astype(BF16)
        w = jnp.dot(fbi_ref[...], z, preferred_element_type=F32)
        base = pl.multiple_of(k2 * SB_PITCH, 8)
        sbr_ref[pl.ds(base, FFT_N1), :] = w[:FFT_N1]
        sbi_ref[pl.ds(base, FFT_N1), :] = w[FFT_N1:]
        return c

    lax.fori_loop(0, n2, stage_b, 0, unroll=4)

    skip = skip_ref[0]

    def stage_c(n1, c):
        wr = sbr_ref[pl.ds(n1, n2, stride=SB_PITCH), :]
        wi = sbi_ref[pl.ds(n1, n2, stride=SB_PITCH), :]
        y = jnp.dot(fai_ref[n1], jnp.concatenate([wr, wi], axis=0).astype(BF16), preferred_element_type=F32)
        res = g_ref[0, n1] * (y + x_ref[0, n1] * skip)
        if natural_out:
            for b in range(res.shape[0] // nj):
                o_ref[0, pl.ds(n1 + b * seq, nj, stride=FFT_N1), :] = res[b * nj:(b + 1) * nj]
        else:
            o_ref[0, n1] = res
        return c

    lax.fori_loop(0, FFT_N1, stage_c, 0, unroll=8)


def _longconv(xg, gg, slab_x, slab_g, skip, hspec, order, seq, natural_out):
    fa, fb, fbi, fai, n2 = _dft_tables(seq)
    pitch = n2 + SA_PITCH_PAD
    nsl = HY_WIDTH // LANES
    rows = xg.shape[2]
    nb = rows // (seq // FFT_N1)
    const = lambda shape: pl.BlockSpec(shape, lambda s: (0,) * len(shape))
    if natural_out:
        out_shape = jax.ShapeDtypeStruct((nsl, nb * seq, LANES), F32)
        out_spec = pl.BlockSpec((1, nb * seq, LANES), lambda s: (s, 0, 0))
    else:
        out_shape = jax.ShapeDtypeStruct((nsl, FFT_N1, rows, LANES), F32)
        out_spec = pl.BlockSpec((1, FFT_N1, rows, LANES), lambda s: (s, 0, 0, 0))
    return pl.pallas_call(
        functools.partial(_longconv_kernel, seq=seq, n2=n2, pitch=pitch, natural_out=natural_out),
        grid=(nsl,),
        in_specs=[pl.BlockSpec((1, FFT_N1, rows, LANES), lambda s: (slab_x + s, 0, 0, 0)),
                  pl.BlockSpec((1, FFT_N1, rows, LANES), lambda s: (slab_g + s, 0, 0, 0)),
                  pl.BlockSpec((1, 1, LANES), lambda s: (s, 0, 0)),
                  pl.BlockSpec((None, 1, n2, 2 * FFT_N1, LANES), lambda s: (order, s, 0, 0, 0)),
                  const(fa.shape), const(fb.shape), const(fbi.shape), const(fai.shape)],
        out_specs=out_spec,
        out_shape=out_shape,
        scratch_shapes=[pltpu.VMEM((FFT_N1 * pitch, LANES), F32), pltpu.VMEM((FFT_N1 * pitch, LANES), F32),
                        pltpu.VMEM((n2 * SB_PITCH, LANES), F32), pltpu.VMEM((n2 * SB_PITCH, LANES), F32)],
        compiler_params=_cparams(("arbitrary",)),
    )(xg, gg, skip.reshape(nsl, 1, LANES), hspec,
      jnp.asarray(fa).astype(BF16), jnp.asarray(fb).astype(BF16),
      jnp.asarray(fbi).astype(BF16), jnp.asarray(fai).astype(BF16))


def _dot_exact_lhs(lhs_bf16, x):
    hi = x.astype(BF16)
    r1 = x - hi.astype(F32)
    mid = r1.astype(BF16)
    lo = (r1 - mid.astype(F32)).astype(BF16)
    d = functools.partial(jnp.dot, preferred_element_type=F32)
    return d(lhs_bf16, hi) + (d(lhs_bf16, mid) + d(lhs_bf16, lo))


def _ssd_kernel(x_ref, b_ref, c_ref, dt_ref, bias_ref, alog_ref, e_ref, init_ref, y_ref, fin_ref, st_ref,
                *, rev, nc):
    ci = pl.program_id(1)
    gw = SSD_HPG * SSD_HEAD_DIM
    dot = functools.partial(jnp.dot, preferred_element_type=F32)

    @pl.when(ci == 0)
    def _():
        st_ref[...] = init_ref[0]

    off = SSD_HEADS if rev else 0
    dt_all = _softplus(dt_ref[...] + bias_ref[...])
    a_all = dt_all * (-jnp.exp(alog_ref[...]))
    r_i = lax.broadcasted_iota(jnp.int32, (SSD_CHUNK, SSD_CHUNK), 0)
    c_i = lax.broadcasted_iota(jnp.int32, (SSD_CHUNK, SSD_CHUNK), 1)
    valid = (c_i >= r_i) if rev else (c_i <= r_i)
    tri = jnp.where(valid, 1.0, 0.0).astype(BF16)
    acum = _dot_exact_lhs(tri, a_all)
    acum_t = acum.T
    tot = acum[0:1, :] if rev else acum[SSD_CHUNK - 1:SSD_CHUNK, :]
    eac = jnp.exp(acum)
    dt_b = dt_all.astype(BF16)
    dtw_b = (dt_all * jnp.exp(tot - acum)).astype(BF16)
    eac_hi = eac.astype(BF16)
    eac_lo = (eac - eac_hi.astype(F32)).astype(BF16)
    cd8 = jnp.broadcast_to(jnp.exp(tot), (8, LANES))
    cd_hi = cd8.astype(BF16)
    cd_r = cd8 - cd_hi.astype(F32)
    cd_mid = cd_r.astype(BF16)
    cd_lo = (cd_r - cd_mid.astype(F32)).astype(BF16)

    for g in range(SSD_GROUPS):
        e_g = e_ref[:, g * gw:(g + 1) * gw]
        bt = b_ref[g].T.astype(BF16)
        cg = c_ref[g].astype(BF16)
        cb = dot(cg, bt)
        sg = st_ref[g]
        xg = jnp.concatenate([x_ref[4 * g + sl] for sl in range(gw // LANES)], axis=1)
        xdt_b = (xg * dot(dt_b, e_g)).astype(BF16)
        wm = (xg * dot(dtw_b, e_g)).astype(BF16)
        yoff = dot(cg, sg.astype(BF16)) * (dot(eac_hi, e_g) + dot(eac_lo, e_g))
        cd = (dot(cd_hi, e_g) + (dot(cd_mid, e_g) + dot(cd_lo, e_g)))[0:1]
        yd = []
        for k in range(SSD_HPG):
            h = off + SSD_HPG * g + k
            seg = acum[:, h:h + 1] - acum_t[h:h + 1, :]
            lmat = jnp.where(valid, jnp.exp(jnp.minimum(seg, 0.0)), 0.0)
            m = (cb * lmat).astype(BF16)
            yd.append(dot(m, xdt_b[:, k * SSD_HEAD_DIM:(k + 1) * SSD_HEAD_DIM]))
        y_ref[g] = jnp.concatenate(yd, axis=1) + yoff
        st_ref[g] = sg * cd + dot(bt, wm)

    @pl.when(ci == nc - 1)
    def _():
        fin_ref[0] = st_ref[...]


def _ssd_scan(xbc, slab_x, slab_b, slab_c, dt, dt_bias, a_log, init, seq, rev):
    rows = xbc.shape[1]
    nb = rows // seq
    nc = seq // SSD_CHUNK
    gw = SSD_HPG * SSD_HEAD_DIM
    xs = SSD_INNER // LANES
    cpos = (lambda b, c: b * nc + (nc - 1 - c)) if rev else (lambda b, c: b * nc + c)
    expand = np.zeros((2 * SSD_HEADS, SSD_INNER), np.float32)
    for h in range(SSD_HEADS):
        expand[(SSD_HEADS if rev else 0) + h, h * SSD_HEAD_DIM:(h + 1) * SSD_HEAD_DIM] = 1.0
    return pl.pallas_call(
        functools.partial(_ssd_kernel, rev=rev, nc=nc),
        grid=(nb, nc),
        in_specs=[pl.BlockSpec((xs, SSD_CHUNK, LANES), lambda b, c: (slab_x // xs, cpos(b, c), 0)),
                  pl.BlockSpec((SSD_GROUPS, SSD_CHUNK, LANES), lambda b, c: (slab_b // SSD_GROUPS, cpos(b, c), 0)),
                  pl.BlockSpec((SSD_GROUPS, SSD_CHUNK, LANES), lambda b, c: (slab_c // SSD_GROUPS, cpos(b, c), 0)),
                  pl.BlockSpec((SSD_CHUNK, LANES), lambda b, c: (cpos(b, c), 0)),
                  pl.BlockSpec((1, LANES), lambda b, c: (0, 0)),
                  pl.BlockSpec((1, LANES), lambda b, c: (0, 0)),
                  pl.BlockSpec((2 * SSD_HEADS, SSD_INNER), lambda b, c: (0, 0)),
                  pl.BlockSpec((1, SSD_GROUPS, SSD_STATE, gw), lambda b, c: (b, 0, 0, 0))],
        out_specs=[pl.BlockSpec((SSD_GROUPS, SSD_CHUNK, gw), lambda b, c: (0, cpos(b, c), 0)),
                   pl.BlockSpec((1, SSD_GROUPS, SSD_STATE, gw), lambda b, c: (b, 0, 0, 0))],
        out_shape=[jax.ShapeDtypeStruct((SSD_GROUPS, rows, gw), F32),
                   jax.ShapeDtypeStruct((nb, SSD_GROUPS, SSD_STATE, gw), F32)],
        scratch_shapes=[pltpu.VMEM((SSD_GROUPS, SSD_STATE, gw), F32)],
        compiler_params=_cparams(("arbitrary", "arbitrary")),
    )(xbc, xbc, xbc, dt, dt_bias.reshape(1, LANES), a_log.reshape(1, LANES),
      jnp.asarray(expand).astype(BF16), init)


def _ssdnorm_kernel(yf_ref, yb_ref, x_ref, z_ref, d_ref, w_ref, o_ref):
    nsl = x_ref.shape[0]
    x = jnp.concatenate([x_ref[i] for i in range(nsl)], axis=1)
    z = jnp.concatenate([z_ref[i] for i in range(nsl)], axis=1)
    y = yf_ref[0] + yb_ref[0] + x * d_ref[0]
    gx = y * _silu(z)
    ms = jnp.mean(gx * gx, axis=-1, keepdims=True)
    o_ref[...] = (gx * lax.rsqrt(ms + 1e-5) * w_ref[0]).astype(o_ref.dtype)


def _ssd_norm(yf, yb, xbc, proj, slab_z, d_ch, w, tr):
    g, rows, gw = yf.shape
    nsl = gw // LANES
    return pl.pallas_call(
        _ssdnorm_kernel,
        grid=(g, rows // tr),
        in_specs=[pl.BlockSpec((1, tr, gw), lambda q, i: (q, i, 0)),
                  pl.BlockSpec((1, tr, gw), lambda q, i: (q, i, 0)),
                  pl.BlockSpec((nsl, tr, LANES), lambda q, i: (q, i, 0)),
                  pl.BlockSpec((nsl, tr, LANES), lambda q, i: (slab_z // nsl + q, i, 0)),
                  pl.BlockSpec((1, 1, gw), lambda q, i: (q, 0, 0)),
                  pl.BlockSpec((1, 1, gw), lambda q, i: (q, 0, 0))],
        out_specs=pl.BlockSpec((tr, gw), lambda q, i: (i, q)),
        out_shape=jax.ShapeDtypeStruct((rows, g * gw), BF16),
        compiler_params=_cparams(("arbitrary", "arbitrary")),
    )(yf, yb, xbc, proj, d_ch.reshape(g, 1, gw), w.reshape(g, 1, gw))


def _merge_kernel(yhy_ref, yss_ref, whh_ref, wss_ref, gh_ref, gs_ref, o_ref):
    a1 = jnp.concatenate([yhy_ref[i] for i in range(yhy_ref.shape[0])], axis=1).astype(BF16)
    p1 = jnp.dot(a1, whh_ref[...], preferred_element_type=F32)
    p2 = jnp.dot(yss_ref[...], wss_ref[...], preferred_element_type=F32)
    gh = jnp.concatenate([gh_ref[i] for i in range(gh_ref.shape[0])], axis=1)
    gs = jnp.concatenate([gs_ref[i] for i in range(gs_ref.shape[0])], axis=1)
    o_ref[...] = (_sigmoid(gh) * p1 + _sigmoid(gs) * p2).astype(o_ref.dtype)


def _branch_merge(yhy, yss, w_hy, w_ssd, proj, slab_gh, slab_gs, tm, tn):
    nsl_k = yhy.shape[0]
    rows = yhy.shape[1]
    n = w_hy.shape[1]
    ts = tn // LANES
    return pl.pallas_call(
        _merge_kernel,
        grid=(rows // tm, n // tn),
        in_specs=[pl.BlockSpec((nsl_k, tm, LANES), lambda i, j: (0, i, 0)),
                  pl.BlockSpec((tm, yss.shape[1]), lambda i, j: (i, 0)),
                  pl.BlockSpec((w_hy.shape[0], tn), lambda i, j: (0, j)),
                  pl.BlockSpec((w_ssd.shape[0], tn), lambda i, j: (0, j)),
                  pl.BlockSpec((ts, tm, LANES), lambda i, j: (slab_gh // ts + j, i, 0)),
                  pl.BlockSpec((ts, tm, LANES), lambda i, j: (slab_gs // ts + j, i, 0))],
        out_specs=pl.BlockSpec((tm, tn), lambda i, j: (i, j)),
        out_shape=jax.ShapeDtypeStruct((rows, n), BF16),
        compiler_params=_cparams(("arbitrary", "arbitrary")),
    )(yhy, yss, w_hy, w_ssd, proj, proj)


def _outproj_ln_kernel(a_ref, w_ref, x_ref, gt_ref, g_ref, b_ref, sc_ref, sh_ref, h_ref, m_ref, mt_ref):
    y = jnp.dot(a_ref[0], w_ref[...], preferred_element_type=F32)
    h = _normalize_rows(DEEPNORM_ALPHA * x_ref[0] + gt_ref[0] * y, 1e-6) * g_ref[...] + b_ref[...]
    h_ref[0] = h
    m = _normalize_rows(h, 1e-6) * (1.0 + sc_ref[0]) + sh_ref[0]
    m_ref[0] = m.astype(m_ref.dtype)
    mt_ref[...] = m.T.astype(mt_ref.dtype)


def _outproj_ln(merged, w_out, x, gate, ln_g, ln_b, scale2, shift2, tm):
    b, l, d = x.shape
    nt = l // tm
    vec = pl.BlockSpec((1, 1, d), lambda i, j: (i, 0, 0))
    cvec = pl.BlockSpec((1, d), lambda i, j: (0, 0))
    tile = pl.BlockSpec((1, tm, d), lambda i, j: (i, j, 0))
    return pl.pallas_call(
        _outproj_ln_kernel,
        grid=(b, nt),
        in_specs=[tile, pl.BlockSpec((d, d), lambda i, j: (0, 0)), tile, vec, cvec, cvec, vec, vec],
        out_specs=[tile, tile, pl.BlockSpec((d, tm), lambda i, j: (0, i * nt + j))],
        out_shape=[jax.ShapeDtypeStruct((b, l, d), F32), jax.ShapeDtypeStruct((b, l, d), BF16),
                   jax.ShapeDtypeStruct((d, b * l), BF16)],
        compiler_params=_cparams(("arbitrary", "arbitrary")),
    )(merged, w_out, x, gate, ln_g.reshape(1, d), ln_b.reshape(1, d), scale2, shift2)


def _resid_ln_kernel(x_ref, y_ref, gt_ref, g_ref, b_ref, o_ref):
    o_ref[0] = _normalize_rows(DEEPNORM_ALPHA * x_ref[0] + gt_ref[0] * y_ref[0], 1e-6) * g_ref[...] + b_ref[...]


def _resid_ln(x, y, gate, ln_g, ln_b, tm):
    b, l, d = x.shape
    tile = pl.BlockSpec((1, tm, d), lambda i, j: (i, j, 0))
    return pl.pallas_call(
        _resid_ln_kernel,
        grid=(b, l // tm),
        in_specs=[tile, tile, pl.BlockSpec((1, 1, d), lambda i, j: (i, 0, 0)),
                  pl.BlockSpec((1, d), lambda i, j: (0, 0)), pl.BlockSpec((1, d), lambda i, j: (0, 0))],
        out_specs=tile,
        out_shape=jax.ShapeDtypeStruct((b, l, d), F32),
        compiler_params=_cparams(("arbitrary", "arbitrary")),
    )(x, y, gate, ln_g.reshape(1, d), ln_b.reshape(1, d))


def _peer_scores_kernel(a_ref, wq_ref, k_ref, s_ref):
    q = jnp.dot(a_ref[...], wq_ref[...], preferred_element_type=F32)
    hd = PEER_DKEY // 2
    nt = (((1,), (1,)), ((), ()))
    for h in range(PEER_HEADS):
        qn = _normalize_rows(q[:, h * PEER_DKEY:(h + 1) * PEER_DKEY], 1e-6).astype(BF16)
        s_ref[h, 0] = lax.dot_general(k_ref[h, 0], qn[:, :hd], nt, preferred_element_type=F32)
        s_ref[h, 1] = lax.dot_general(k_ref[h, 1], qn[:, hd:], nt, preferred_element_type=F32)


def _peer_scores(m2, wq, subkeys, tm):
    t, d = m2.shape
    return pl.pallas_call(
        _peer_scores_kernel,
        grid=(t // tm,),
        in_specs=[pl.BlockSpec((tm, d), lambda i: (i, 0)),
                  pl.BlockSpec(wq.shape, lambda i: (0, 0)),
                  pl.BlockSpec(subkeys.shape, lambda i: (0, 0, 0, 0))],
        out_specs=pl.BlockSpec((PEER_HEADS, 2, PEER_NKEYS, tm), lambda i: (0, 0, 0, i)),
        out_shape=jax.ShapeDtypeStruct((PEER_HEADS, 2, PEER_NKEYS, t), F32),
        compiler_params=_cparams(("arbitrary",)),
    )(m2, wq, subkeys)


def _top_values(x, k):
    out = []
    for _ in range(k):
        m = jnp.max(x, axis=0, keepdims=True)
        out.append(m)
        x = jnp.where(x == m, -jnp.inf, x)
    return out


def _peer_stats_kernel(s_ref, th_ref, p1_ref, p2_ref):
    k = PEER_TOPK
    for h in range(PEER_HEADS):
        s1 = s_ref[h, 0]
        s2 = s_ref[h, 1]
        v1 = _top_values(s1, k + 1)
        v2 = _top_values(s2, k + 1)
        v2m = jnp.concatenate(v2[:k], axis=0)
        cand = jnp.concatenate([v1[a] + v2m for a in range(k)], axis=0)
        best = _top_values(cand, k + 1)
        z = best[0] * 0.0
        for c in best[:k]:
            z = z + jnp.exp(c - best[0])
        runner_up = jnp.maximum(best[k], jnp.maximum(v1[k] + v2[0], v1[0] + v2[k]))
        th_ref[h] = 0.5 * (best[k - 1] + runner_up) - s1
        p1_ref[h] = jnp.exp(s1 - v1[0])
        p2_ref[h] = 0.5 * jnp.exp(s2 - v2[0]) / z


def _peer_stats(scores, tm):
    t = scores.shape[-1]
    hk = pl.BlockSpec((PEER_HEADS, PEER_NKEYS, tm), lambda i: (0, 0, i))
    sds = jax.ShapeDtypeStruct((PEER_HEADS, PEER_NKEYS, t), F32)
    return pl.pallas_call(
        _peer_stats_kernel,
        grid=(t // tm,),
        in_specs=[pl.BlockSpec((PEER_HEADS, 2, PEER_NKEYS, tm), lambda i: (0, 0, 0, i))],
        out_specs=[hk, hk, hk],
        out_shape=[sds, sds, sds],
        compiler_params=_cparams(("arbitrary",)),
    )(scores)


def _peer_dense_kernel(ht_ref, u_ref, vt_ref, s2_ref, th_ref, p1_ref, p2_ref, o_ref, acc_ref, g_ref, *, ne1):
    j = pl.program_id(1)

    @pl.when(j == 0)
    def _():
        acc_ref[...] = jnp.zeros_like(acc_ref)

    for r in range(ne1):
        e1 = j * ne1 + r
        rows = slice(r * PEER_NKEYS, (r + 1) * PEER_NKEYS)
        th_rows = [th_ref[h, pl.ds(e1, 1), :] for h in range(PEER_HEADS)]
        p1_rows = [p1_ref[h, pl.ds(e1, 1), :] for h in range(PEER_HEADS)]
        for tt in range(g_ref.shape[1] // LANES):
            ls = slice(tt * LANES, (tt + 1) * LANES)
            gacc = None
            for h in range(PEER_HEADS):
                w = jnp.where(s2_ref[h, :, ls] >= th_rows[h][:, ls], p2_ref[h, :, ls] * p1_rows[h][:, ls], 0.0)
                gacc = w if gacc is None else gacc + w
            g_ref[rows, ls] = gacc
    act = jnp.dot(u_ref[...], ht_ref[...], preferred_element_type=F32)
    gelu2 = act * (1.0 + jnp.tanh(math.sqrt(2.0 / math.pi) * (act + 0.044715 * (act * act * act))))
    pt = (gelu2 * g_ref[...]).astype(BF16)
    acc_ref[...] += jnp.dot(vt_ref[...], pt, preferred_element_type=F32)

    @pl.when(j == pl.num_programs(1) - 1)
    def _():
        o_ref[...] = acc_ref[...].T


def _peer_dense(m2t, u_b, vt_b, scores, th, p1, p2, tm, te):
    d, t = m2t.shape
    ne = u_b.shape[0]
    hk = pl.BlockSpec((PEER_HEADS, PEER_NKEYS, tm), lambda i, j: (0, 0, i))
    return pl.pallas_call(
        functools.partial(_peer_dense_kernel, ne1=te // PEER_NKEYS),
        grid=(t // tm, ne // te),
        in_specs=[pl.BlockSpec((d, tm), lambda i, j: (0, i)),
                  pl.BlockSpec((te, d), lambda i, j: (j, 0)),
                  pl.BlockSpec((d, te), lambda i, j: (0, j)),
                  pl.BlockSpec((PEER_HEADS, None, PEER_NKEYS, tm), lambda i, j: (0, 1, 0, i)),
                  hk, hk, hk],
        out_specs=pl.BlockSpec((tm, d), lambda i, j: (i, 0)),
        out_shape=jax.ShapeDtypeStruct((t, d), F32),
        scratch_shapes=[pltpu.VMEM((d, tm), F32), pltpu.VMEM((te, tm), F32)],
        compiler_params=_cparams(("arbitrary", "arbitrary")),
    )(m2t, u_b, vt_b, scores, th, p1, p2)


SL_HY = 0
SL_X = OFF_X // LANES
SL_B = OFF_B // LANES
SL_C = OFF_C // LANES
SL_Z = OFF_DT // LANES
SL_GH = SL_Z + SSD_INNER // LANES
SL_GS = SL_GH + D_MODEL // LANES


def kernel(x, c, ctx, c_ctx, ada_w, ada_b, w_in, hy_conv_w, hy_conv_b, hy_ffn_w1, hy_ffn_b1, hy_freq1,
           hy_ffn_w2, hy_ffn_b2, hy_freq2, hy_ffn_w3, hy_skip, ssd_conv_w, ssd_conv_b, ssd_a_log,
           ssd_dt_bias, ssd_d, ssd_norm_w, w_branch_hy, w_branch_ssd, w_out, ln1_g, ln1_b, peer_wq,
           peer_subkeys, peer_u, peer_v, ln2_g, ln2_b):
    nb, seq, d = x.shape
    lc = ctx.shape[1]
    t = nb * seq
    p = dict(hy_ffn_w1=hy_ffn_w1[0], hy_ffn_b1=hy_ffn_b1[0], hy_freq1=hy_freq1[0], hy_ffn_w2=hy_ffn_w2[0],
             hy_ffn_b2=hy_ffn_b2[0], hy_freq2=hy_freq2[0], hy_ffn_w3=hy_ffn_w3[0])

    cc = jnp.zeros((8, d), F32).at[:nb].set(c).at[nb].set(c_ctx)
    mod = _ada(cc, ada_w[0], ada_b[0])
    sh1, sc1, gt1, sh2, sc2, gt2 = [mod[:nb, i * d:(i + 1) * d].reshape(nb, 1, d) for i in range(6)]
    csh1 = mod[nb:nb + 1, 0:d].reshape(1, 1, d)
    csc1 = mod[nb:nb + 1, d:2 * d].reshape(1, 1, d)

    w = w_in[0]
    w_main = jnp.concatenate([w[:, :OFF_DT], w[:, OFF_Z:]], axis=1).astype(BF16)
    w_dt = w[:, OFF_DT:OFF_Z].astype(BF16)
    w_ctx = w[:, OFF_X:OFF_C].astype(BF16)

    m_ctx = _modulate(ctx, csc1, csh1, 256).reshape(nb * lc, d)
    pc = _matmul(m_ctx, w_ctx, 512, 1024, slab=True)
    dtc = _matmul(m_ctx, w_dt, 512, LANES)
    nxb = (OFF_C - OFF_X) // LANES
    cw = ssd_conv_w[0].reshape(3, -1, LANES).transpose(1, 0, 2)
    cbias = ssd_conv_b[0].reshape(-1, 1, LANES)
    xbc_ctx = _conv3(pc, 0, nxb, cw[:nxb], cbias[:nxb], lc, True)
    gw = SSD_HPG * SSD_HEAD_DIM
    zero_state = jnp.zeros((nb, SSD_GROUPS, SSD_STATE, gw), F32)
    dt_bias = ssd_dt_bias[0]
    a_log = ssd_a_log[0]
    xs_sl = SSD_INNER // LANES
    _, s_f = _ssd_scan(xbc_ctx, 0, xs_sl, xs_sl, dtc, dt_bias, a_log, zero_state, lc, False)
    _, s_b = _ssd_scan(xbc_ctx, 0, xs_sl, xs_sl, dtc, dt_bias, a_log, zero_state, lc, True)

    m1 = _modulate(x, sc1, sh1, 256).reshape(t, d)
    proj = _matmul(m1, w_main, 1024, 1024, slab=True)
    dtp = _matmul(m1, w_dt, 1024, LANES)

    hw = hy_conv_w[0].reshape(3, -1, LANES).transpose(1, 0, 2)
    hb = hy_conv_b[0].reshape(-1, 1, LANES)
    nh = HY_WIDTH // LANES
    ug = _hyconv(proj, 3 * nh, hw, hb, seq)
    hspec = _hyena_filter_spectra(p, seq)
    zg = _longconv(ug, ug, 0, nh, hy_skip[0, 0], hspec, 0, seq, False)
    y_hy = _longconv(zg, ug, 0, 2 * nh, hy_skip[0, 1], hspec, 1, seq, True)

    xbc = _conv3(proj, SL_X, (OFF_DT - OFF_X) // LANES, cw, cbias, seq, True)
    y_f, _ = _ssd_scan(xbc, 0, xs_sl, xs_sl + SSD_GROUPS, dtp, dt_bias, a_log, s_f, seq, False)
    y_b, _ = _ssd_scan(xbc, 0, xs_sl, xs_sl + SSD_GROUPS, dtp, dt_bias, a_log, s_b, seq, True)
    d_ch = jnp.repeat(ssd_d[0, 0] + ssd_d[0, 1], SSD_HEAD_DIM)
    y_ss = _ssd_norm(y_f, y_b, xbc, proj, SL_Z, d_ch, ssd_norm_w[0], 512)

    merged = _branch_merge(y_hy, y_ss, w_branch_hy[0].astype(BF16), w_branch_ssd[0].astype(BF16),
                           proj, SL_GH, SL_GS, 512, 512)
    h1, m2, m2t = _outproj_ln(merged.reshape(nb, seq, d), w_out[0].astype(BF16), x, gt1, ln1_g[0], ln1_b[0],
                              sc2, sh2, 256)

    m2f = m2.reshape(t, d)
    scores = _peer_scores(m2f, peer_wq[0].astype(BF16), peer_subkeys[0].astype(BF16), 256)
    th, p1, p2 = _peer_stats(scores, 256)
    ffn = _peer_dense(m2t, peer_u[0].astype(BF16), peer_v[0].T.astype(BF16), scores, th, p1, p2, 512, 512)
    return _resid_ln(h1, ffn.reshape(nb, seq, d), gt2, ln2_g[0], ln2_b[0], 256)
```

```python
import functools
import math

import numpy as np
import jax
import jax.numpy as jnp
from jax import lax
from jax.experimental import pallas as pl
from jax.experimental.pallas import tpu as pltpu

F32 = jnp.float32
BF16 = jnp.bfloat16

D_MODEL = 2048
DEPTH = 1
CTX_LEN = 256
HY_WIDTH = D_MODEL
HY_BANDS = 16
HY_EMB = 2 * HY_BANDS + 1
HY_FFN = 64
HY_MIN_DECAY = math.log(1e-2) / 1.5
HY_MAX_DECAY = math.log(1e-2) / 0.3
SSD_INNER = 2 * D_MODEL
SSD_HEAD_DIM = 64
SSD_HEADS = SSD_INNER // SSD_HEAD_DIM
SSD_STATE = 128
SSD_GROUPS = 8
SSD_HPG = SSD_HEADS // SSD_GROUPS
SSD_CHUNK = 128
OFF_X = 3 * HY_WIDTH
OFF_B = OFF_X + SSD_INNER
OFF_C = OFF_B + SSD_GROUPS * SSD_STATE
OFF_DT = OFF_C + SSD_GROUPS * SSD_STATE
OFF_Z = OFF_DT + 2 * SSD_HEADS
OFF_GATE = OFF_Z + SSD_INNER
N_COLS = OFF_GATE + 2 * D_MODEL
PEER_HEADS = 8
PEER_NKEYS = 128
PEER_EXPERTS = PEER_NKEYS * PEER_NKEYS
PEER_TOPK = 16
PEER_DKEY = 256
DEEPNORM_ALPHA = (2.0 * DEPTH) ** 0.25

LANES = 128
BF16_ROWS = 16
VMEM_LIMIT = 56 * 1024 * 1024

FFT_N1 = 128
SA_PITCH_PAD = 8
SB_PITCH = FFT_N1 + 8


def _cparams(sem, vmem=VMEM_LIMIT):
    return pltpu.CompilerParams(dimension_semantics=sem, vmem_limit_bytes=vmem)


def _silu(x):
    return x * (1.0 / (1.0 + jnp.exp(-x)))


def _sigmoid(x):
    return 1.0 / (1.0 + jnp.exp(-x))


def _softplus(x):
    return jnp.maximum(x, 0.0) + jnp.log(1.0 + jnp.exp(-jnp.abs(x)))


def _gelu_tanh(x):
    return 0.5 * x * (1.0 + jnp.tanh(math.sqrt(2.0 / math.pi) * (x + 0.044715 * (x * x * x))))


def _normalize_rows(x, eps):
    mu = jnp.mean(x, axis=-1, keepdims=True)
    xc = x - mu
    var = jnp.mean(xc * xc, axis=-1, keepdims=True)
    return xc * lax.rsqrt(var + eps)


def _ada_kernel(c_ref, w_ref, b_ref, o_ref):
    a = _silu(c_ref[...]).astype(BF16)
    o_ref[...] = jnp.dot(a, w_ref[...].astype(BF16), preferred_element_type=F32) + b_ref[...]


def _ada(cc, ada_w, ada_b):
    rows, d = cc.shape
    n = ada_w.shape[1]
    tn = 1024
    return pl.pallas_call(
        _ada_kernel,
        grid=(n // tn,),
        in_specs=[pl.BlockSpec((rows, d), lambda j: (0, 0)),
                  pl.BlockSpec((d, tn), lambda j: (0, j)),
                  pl.BlockSpec((1, tn), lambda j: (0, j))],
        out_specs=pl.BlockSpec((rows, tn), lambda j: (0, j)),
        out_shape=jax.ShapeDtypeStruct((rows, n), F32),
        compiler_params=_cparams(("arbitrary",)),
    )(cc, ada_w, ada_b.reshape(1, n))


def _mod_kernel(x_ref, sc_ref, sh_ref, o_ref):
    xn = _normalize_rows(x_ref[0], 1e-6)
    o_ref[0] = (xn * (1.0 + sc_ref[0]) + sh_ref[0]).astype(o_ref.dtype)


def _modulate(x, scale, shift, tr):
    b, l, d = x.shape
    per_batch = scale.shape[0] == b
    smap = (lambda i, j: (i, 0, 0)) if per_batch else (lambda i, j: (0, 0, 0))
    return pl.pallas_call(
        _mod_kernel,
        grid=(b, l // tr),
        in_specs=[pl.BlockSpec((1, tr, d), lambda i, j: (i, j, 0)),
                  pl.BlockSpec((1, 1, d), smap),
                  pl.BlockSpec((1, 1, d), smap)],
        out_specs=pl.BlockSpec((1, tr, d), lambda i, j: (i, j, 0)),
        out_shape=jax.ShapeDtypeStruct((b, l, d), BF16),
        compiler_params=_cparams(("arbitrary", "arbitrary")),
    )(x, scale, shift)


def _mm_kernel(a_ref, b_ref, o_ref, *, slab):
    acc = jnp.dot(a_ref[...], b_ref[...], preferred_element_type=F32)
    if slab:
        for s in range(o_ref.shape[0]):
            o_ref[s] = acc[:, s * LANES:(s + 1) * LANES].astype(o_ref.dtype)
    else:
        o_ref[...] = acc.astype(o_ref.dtype)


def _matmul(a, b, tm, tn, out_dtype=F32, slab=False):
    m, k = a.shape
    n = b.shape[1]
    if slab:
        out_shape = jax.ShapeDtypeStruct((n // LANES, m, LANES), out_dtype)
        out_spec = pl.BlockSpec((tn // LANES, tm, LANES), lambda i, j: (j, i, 0))
    else:
        out_shape = jax.ShapeDtypeStruct((m, n), out_dtype)
        out_spec = pl.BlockSpec((tm, tn), lambda i, j: (i, j))
    return pl.pallas_call(
        functools.partial(_mm_kernel, slab=slab),
        grid=(m // tm, n // tn),
        in_specs=[pl.BlockSpec((tm, k), lambda i, j: (i, 0)),
                  pl.BlockSpec((k, tn), lambda i, j: (0, j))],
        out_specs=out_spec,
        out_shape=out_shape,
        compiler_params=_cparams(("arbitrary", "arbitrary")),
    )(a, b)


def _conv3_kernel(x_ref, w_ref, b_ref, o_ref, *, silu):
    x = x_ref[0].astype(F32)
    l = x.shape[0]
    row = lax.broadcasted_iota(jnp.int32, x.shape, 0)
    prev = jnp.where(row == 0, 0.0, pltpu.roll(x, 1, 0))
    nxt = jnp.where(row == l - 1, 0.0, pltpu.roll(x, l - 1, 0))
    w = w_ref[0]
    y = prev * w[0:1] + x * w[1:2] + nxt * w[2:3] + b_ref[0]
    if silu:
        y = _silu(y)
    o_ref[0] = y.astype(o_ref.dtype)


def _conv3(p_slab, slab0, nslab, w, bias, seq, silu):
    rows = p_slab.shape[1]
    nb = rows // seq
    return pl.pallas_call(
        functools.partial(_conv3_kernel, silu=silu),
        grid=(nslab, nb),
        in_specs=[pl.BlockSpec((1, seq, LANES), lambda s, b: (slab0 + s, b, 0)),
                  pl.BlockSpec((1, 3, LANES), lambda s, b: (s, 0, 0)),
                  pl.BlockSpec((1, 1, LANES), lambda s, b: (s, 0, 0))],
        out_specs=pl.BlockSpec((1, seq, LANES), lambda s, b: (s, b, 0)),
        out_shape=jax.ShapeDtypeStruct((nslab, rows, LANES), BF16),
        compiler_params=_cparams(("arbitrary", "arbitrary")),
    )(p_slab, w, bias)


def _hyconv_kernel(x_ref, w_ref, b_ref, o_ref, *, seq, nb):
    nj = seq // FFT_N1

    def gather(n1):
        return jnp.concatenate(
            [x_ref[0, pl.ds(n1 + b * seq, nj, stride=FFT_N1), :] for b in range(nb)], axis=0)

    w = w_ref[0]
    w0, w1, w2, bias = w[0:1], w[1:2], w[2:3], b_ref[0]
    jrow = lax.broadcasted_iota(jnp.int32, (nb * nj, LANES), 0) % nj

    def shift_down(g):
        return jnp.where(jrow == 0, 0.0, pltpu.roll(g, 1, 0))

    def shift_up(g):
        return jnp.where(jrow == nj - 1, 0.0, pltpu.roll(g, nb * nj - 1, 0))

    g_first = gather(0)
    g_second = gather(1)
    g_last = gather(FFT_N1 - 1)
    o_ref[0, 0] = shift_down(g_last) * w0 + g_first * w1 + g_second * w2 + bias

    def body(n1, carry):
        g_prev, g_cur = carry
        g_next = gather(n1 + 1)
        o_ref[0, n1] = g_prev * w0 + g_cur * w1 + g_next * w2 + bias
        return g_cur, g_next

    g_prev, g_cur = lax.fori_loop(1, FFT_N1 - 1, body, (g_first, g_second), unroll=6)
    o_ref[0, FFT_N1 - 1] = g_prev * w0 + g_cur * w1 + shift_up(g_first) * w2 + bias


def _hyconv(p_slab, nslab, w, bias, seq):
    rows = p_slab.shape[1]
    nb = rows // seq
    nj = seq // FFT_N1
    return pl.pallas_call(
        functools.partial(_hyconv_kernel, seq=seq, nb=nb),
        grid=(nslab,),
        in_specs=[pl.BlockSpec((1, rows, LANES), lambda s: (s, 0, 0)),
                  pl.BlockSpec((1, 3, LANES), lambda s: (s, 0, 0)),
                  pl.BlockSpec((1, 1, LANES), lambda s: (s, 0, 0))],
        out_specs=pl.BlockSpec((1, FFT_N1, nb * nj, LANES), lambda s: (s, 0, 0, 0)),
        out_shape=jax.ShapeDtypeStruct((nslab, FFT_N1, nb * nj, LANES), F32),
        compiler_params=_cparams(("arbitrary",)),
    )(p_slab, w, bias)


@functools.lru_cache(maxsize=None)
def _dft_tables(seq):
    nj = seq // FFT_N1
    n2 = nj + nj // 2
    n = FFT_N1 * n2
    out0 = nj // 2
    n1 = np.arange(FFT_N1, dtype=np.float64)
    k2 = np.arange(n2, dtype=np.float64)
    jn = np.arange(nj, dtype=np.float64)
    ang = -2.0 * np.pi * (k2[None, :, None] * jn[None, None, :] / n2 + n1[:, None, None] * k2[None, :, None] / n)
    cr, ci = np.cos(ang), np.sin(ang)
    fa = np.concatenate([np.concatenate([cr, -ci], axis=2), np.concatenate([ci, cr], axis=2)], axis=1)
    k1 = np.arange(FFT_N1, dtype=np.float64)
    angb = -2.0 * np.pi * np.outer(k1, n1) / FFT_N1
    dr, di = np.cos(angb), np.sin(angb)
    fb = np.block([[dr, -di], [di, dr]])
    fbi = np.block([[dr, di], [-di, dr]])
    angi = 2.0 * np.pi * (n1[:, None, None] * k2[None, None, :] / n + (out0 + jn)[None, :, None] * k2[None, None, :] / n2)
    er, ei = np.cos(angi) / n, np.sin(angi) / n
    fai = np.concatenate([np.concatenate([er, -ei], axis=2), np.concatenate([ei, er], axis=2)], axis=1)
    return (fa.astype(np.float32), fb.astype(np.float32), fbi.astype(np.float32), fai.astype(np.float32), n2)


@functools.lru_cache(maxsize=None)
def _filter_consts(seq):
    t = np.linspace(0.0, 1.0, seq, dtype=np.float32)[:, None].astype(np.float64)
    w = 2.0 * np.pi * np.arange(seq, dtype=np.float32)[:, None].astype(np.float64) / seq
    bands = np.linspace(1e-4, HY_BANDS - 1, HY_BANDS, dtype=np.float32)[None, :].astype(np.float64)
    feats = np.concatenate([t, np.cos(bands * w), -np.sin(bands * w)], axis=-1)
    feats_p = np.zeros((seq, LANES), np.float32)
    feats_p[:, :HY_EMB] = feats
    dist = (np.abs(np.arange(seq) - seq // 2).astype(np.float32) / np.float32(seq / 2.0)).reshape(seq, 1)
    deltas = np.abs(np.linspace(HY_MIN_DECAY, HY_MAX_DECAY, HY_WIDTH, dtype=np.float32))
    return feats_p, dist.astype(np.float32), deltas.astype(np.float32)


def _dot_f32(a, b):
    def split(x):
        hi = x.astype(BF16)
        r1 = x - hi.astype(F32)
        mid = r1.astype(BF16)
        lo = (r1 - mid.astype(F32)).astype(BF16)
        return hi, mid, lo
    a0, a1, a2 = split(a)
    b0, b1, b2 = split(b)
    d = functools.partial(jnp.dot, preferred_element_type=F32)
    return (d(a0, b0) + (d(a0, b1) + d(a1, b0)) + (d(a0, b2) + d(a1, b1) + d(a2, b0)))


def _filter_kernel(feats_ref, w1_ref, b1_ref, f1_ref, w2_ref, b2_ref, f2_ref, w3_ref, dist_ref, delta_ref,
                   fa_ref, fb_ref, o_ref, hid_ref, h_ref, sar_ref, sai_ref, *, seq, n2, pitch):
    nj = seq // FFT_N1

    @pl.when((pl.program_id(0) == 0) & (pl.program_id(1) == 0))
    def _():
        h1 = jnp.sin(f1_ref[...] * (_dot_f32(feats_ref[...], w1_ref[...]) + b1_ref[...]))
        hid_ref[...] = jnp.sin(f2_ref[...] * (_dot_f32(h1, w2_ref[...]) + b2_ref[...])).astype(BF16)

    h = jnp.dot(hid_ref[...], w3_ref[0, 0].astype(BF16), preferred_element_type=F32)
    h = h * jnp.exp(-dist_ref[...] * delta_ref[0])
    h = h / (jnp.sum(jnp.abs(h), axis=0, keepdims=True) + 1e-6)
    h_ref[...] = h

    def stage_a(n1, c):
        g = h_ref[pl.ds(n1, nj, stride=FFT_N1), :].astype(BF16)
        y = jnp.dot(fa_ref[n1], g, preferred_element_type=F32)
        base = pl.multiple_of(n1 * pitch, 8)
        sar_ref[pl.ds(base, n2), :] = y[:n2]
        sai_ref[pl.ds(base, n2), :] = y[n2:]
        return c

    lax.fori_loop(0, FFT_N1, stage_a, 0, unroll=8)

    def stage_b(k2, c):
        yr = sar_ref[pl.ds(k2, FFT_N1, stride=pitch), :]
        yi = sai_ref[pl.ds(k2, FFT_N1, stride=pitch), :]
        op = jnp.concatenate([yr, yi], axis=0).astype(BF16)
        o_ref[0, 0, k2] = jnp.dot(fb_ref[...], op, preferred_element_type=F32).astype(o_ref.dtype)
        return c

    lax.fori_loop(0, n2, stage_b, 0, unroll=4)


def _hyena_filter_spectra(p, seq):
    fa, fb, _, _, n2 = _dft_tables(seq)
    nj = seq // FFT_N1
    pitch = n2 + SA_PITCH_PAD
    feats, dist, deltas = _filter_consts(seq)
    nsl = HY_WIDTH // LANES

    def padk(w, rows):
        return jnp.zeros((rows, w.shape[1]), F32).at[:w.shape[0]].set(w)

    def padn(v):
        return jnp.zeros((1, LANES), F32).at[0, :v.shape[0]].set(v)

    w1 = jnp.zeros((LANES, LANES), F32).at[:HY_EMB, :HY_FFN].set(p['hy_ffn_w1'])
    w2 = jnp.zeros((LANES, LANES), F32).at[:HY_FFN, :HY_FFN].set(p['hy_ffn_w2'])
    w3 = padk(p['hy_ffn_w3'], LANES).reshape(LANES, 2, nsl, LANES).transpose(1, 2, 0, 3)
    fa_real = jnp.asarray(fa[:, :, :nj]).astype(BF16)
    const = lambda shape: pl.BlockSpec(shape, lambda o, s: (0,) * len(shape))
    return pl.pallas_call(
        functools.partial(_filter_kernel, seq=seq, n2=n2, pitch=pitch),
        grid=(2, nsl),
        in_specs=[const((seq, LANES)), const((LANES, LANES)), const((1, LANES)), const((1, LANES)),
                  const((LANES, LANES)), const((1, LANES)), const((1, LANES)),
                  pl.BlockSpec((1, 1, LANES, LANES), lambda o, s: (o, s, 0, 0)),
                  const((seq, 1)),
                  pl.BlockSpec((1, 1, LANES), lambda o, s: (s, 0, 0)),
                  const((FFT_N1, 2 * n2, nj)), const((2 * FFT_N1, 2 * FFT_N1))],
        out_specs=pl.BlockSpec((1, 1, n2, 2 * FFT_N1, LANES), lambda o, s: (o, s, 0, 0, 0)),
        out_shape=jax.ShapeDtypeStruct((2, nsl, n2, 2 * FFT_N1, LANES), BF16),
        scratch_shapes=[pltpu.VMEM((seq, LANES), BF16), pltpu.VMEM((seq, LANES), F32),
                        pltpu.VMEM((FFT_N1 * pitch, LANES), F32), pltpu.VMEM((FFT_N1 * pitch, LANES), F32)],
        compiler_params=_cparams(("arbitrary", "arbitrary")),
    )(jnp.asarray(feats), w1, padn(p['hy_ffn_b1']), padn(p['hy_freq1']), w2, padn(p['hy_ffn_b2']),
      padn(p['hy_freq2']), w3, jnp.asarray(dist), jnp.asarray(deltas).reshape(nsl, 1, LANES),
      fa_real, jnp.asarray(fb).astype(BF16))


def _longconv_kernel(x_ref, g_ref, skip_ref, h_ref, fa_ref, fb_ref, fbi_ref, fai_ref, o_ref,
                     sar_ref, sai_ref, sbr_ref, sbi_ref, *, seq, n2, pitch, natural_out):
    nj = seq // FFT_N1

    def stage_a(n1, c):
        y = jnp.dot(fa_ref[n1], x_ref[0, n1].astype(BF16), preferred_element_type=F32)
        base = pl.multiple_of(n1 * pitch, 8)
        sar_ref[pl.ds(base, n2), :] = y[:n2]
        sai_ref[pl.ds(base, n2), :] = y[n2:]
        return c

    lax.fori_loop(0, FFT_N1, stage_a, 0, unroll=8)

    def stage_b(k2, c):
        yr = sar_ref[pl.ds(k2, FFT_N1, stride=pitch), :]
        yi = sai_ref[pl.ds(k2, FFT_N1, stride=pitch), :]
        xf = jnp.dot(fb_ref[...], jnp.concatenate([yr, yi], axis=0).astype(BF16), preferred_element_type=F32)
        xr, xi = xf[:FFT_N1], xf[FFT_N1:]
        hf = h_ref[0, k2].astype(F32)
        hr, hi = hf[:FFT_N1], hf[FFT_N1:]
        z = jnp.concatenate([xr * hr - xi * hi, xr * hi + xi * hr], axis=0).astype(BF16)
        w = jnp.dot(fbi_ref[...], z, preferred_element_type=F32)
        base = pl.multiple_of(k2 * SB_PITCH, 8)
        sbr_ref[pl.ds(base, FFT_N1), :] = w[:FFT_N1]
        sbi_ref[pl.ds(base, FFT_N1), :] = w[FFT_N1:]
        return c

    lax.fori_loop(0, n2, stage_b, 0, unroll=4)

    skip = skip_ref[0]

    def stage_c(n1, c):
        wr = sbr_ref[pl.ds(n1, n2, stride=SB_PITCH), :]
        wi = sbi_ref[pl.ds(n1, n2, stride=SB_PITCH), :]
        y = jnp.dot(fai_ref[n1], jnp.concatenate([wr, wi], axis=0).astype(BF16), preferred_element_type=F32)
        res = g_ref[0, n1] * (y + x_ref[0, n1] * skip)
        if natural_out:
            for b in range(res.shape[0] // nj):
                o_ref[0, pl.ds(n1 + b * seq, nj, stride=FFT_N1), :] = res[b * nj:(b + 1) * nj]
        else:
            o_ref[0, n1] = res
        return c

    lax.fori_loop(0, FFT_N1, stage_c, 0, unroll=8)


def _longconv(xg, gg, slab_x, slab_g, skip, hspec, order, seq, natural_out):
    fa, fb, fbi, fai, n2 = _dft_tables(seq)
    pitch = n2 + SA_PITCH_PAD
    nsl = HY_WIDTH // LANES
    rows = xg.shape[2]
    nb = rows // (seq // FFT_N1)
    const = lambda shape: pl.BlockSpec(shape, lambda s: (0,) * len(shape))
    if natural_out:
        out_shape = jax.ShapeDtypeStruct((nsl, nb * seq, LANES), F32)
        out_spec = pl.BlockSpec((1, nb * seq, LANES), lambda s: (s, 0, 0))
    else:
        out_shape = jax.ShapeDtypeStruct((nsl, FFT_N1, rows, LANES), F32)
        out_spec = pl.BlockSpec((1, FFT_N1, rows, LANES), lambda s: (s, 0, 0, 0))
    return pl.pallas_call(
        functools.partial(_longconv_kernel, seq=seq, n2=n2, pitch=pitch, natural_out=natural_out),
        grid=(nsl,),
        in_specs=[pl.BlockSpec((1, FFT_N1, rows, LANES), lambda s: (slab_x + s, 0, 0, 0)),
                  pl.BlockSpec((1, FFT_N1, rows, LANES), lambda s: (slab_g + s, 0, 0, 0)),
                  pl.BlockSpec((1, 1, LANES), lambda s: (s, 0, 0)),
                  pl.BlockSpec((None, 1, n2, 2 * FFT_N1, LANES), lambda s: (order, s, 0, 0, 0)),
                  const(fa.shape), const(fb.shape), const(fbi.shape), const(fai.shape)],
        out_specs=out_spec,
        out_shape=out_shape,
        scratch_shapes=[pltpu.VMEM((FFT_N1 * pitch, LANES), F32), pltpu.VMEM((FFT_N1 * pitch, LANES), F32),
                        pltpu.VMEM((n2 * SB_PITCH, LANES), F32), pltpu.VMEM((n2 * SB_PITCH, LANES), F32)],
        compiler_params=_cparams(("arbitrary",)),
    )(xg, gg, skip.reshape(nsl, 1, LANES), hspec,
      jnp.asarray(fa).astype(BF16), jnp.asarray(fb).astype(BF16),
      jnp.asarray(fbi).astype(BF16), jnp.asarray(fai).astype(BF16))


def _dot_exact_lhs(lhs_bf16, x):
    hi = x.astype(BF16)
    r1 = x - hi.astype(F32)
    mid = r1.astype(BF16)
    lo = (r1 - mid.astype(F32)).astype(BF16)
    d = functools.partial(jnp.dot, preferred_element_type=F32)
    return d(lhs_bf16, hi) + (d(lhs_bf16, mid) + d(lhs_bf16, lo))


def _ssd_kernel(x_ref, b_ref, c_ref, dt_ref, bias_ref, alog_ref, e_ref, init_ref, y_ref, fin_ref, st_ref,
                *, rev, nc):
    ci = pl.program_id(1)
    gw = SSD_HPG * SSD_HEAD_DIM
    dot = functools.partial(jnp.dot, preferred_element_type=F32)

    @pl.when(ci == 0)
    def _():
        st_ref[...] = init_ref[0]

    off = SSD_HEADS if rev else 0
    dt_all = _softplus(dt_ref[...] + bias_ref[...])
    a_all = dt_all * (-jnp.exp(alog_ref[...]))
    r_i = lax.broadcasted_iota(jnp.int32, (SSD_CHUNK, SSD_CHUNK), 0)
    c_i = lax.broadcasted_iota(jnp.int32, (SSD_CHUNK, SSD_CHUNK), 1)
    valid = (c_i >= r_i) if rev else (c_i <= r_i)
    tri = jnp.where(valid, 1.0, 0.0).astype(BF16)
    acum = _dot_exact_lhs(tri, a_all)
    acum_t = acum.T
    tot = acum[0:1, :] if rev else acum[SSD_CHUNK - 1:SSD_CHUNK, :]
    eac = jnp.exp(acum)
    dt_b = dt_all.astype(BF16)
    dtw_b = (dt_all * jnp.exp(tot - acum)).astype(BF16)
    eac_hi = eac.astype(BF16)
    eac_lo = (eac - eac_hi.astype(F32)).astype(BF16)
    cd8 = jnp.broadcast_to(jnp.exp(tot), (8, LANES))
    cd_hi = cd8.astype(BF16)
    cd_r = cd8 - cd_hi.astype(F32)
    cd_mid = cd_r.astype(BF16)
    cd_lo = (cd_r - cd_mid.astype(F32)).astype(BF16)

    for g in range(SSD_GROUPS):
        e_g = e_ref[:, g * gw:(g + 1) * gw]
        bt = b_ref[g].astype(F32).T.astype(BF16)
        cg = c_ref[g].astype(BF16)
        cb = dot(cg, bt)
        sg = st_ref[g]
        xg = jnp.concatenate([x_ref[4 * g + sl] for sl in range(gw // LANES)], axis=1).astype(F32)
        xdt_b = (xg * dot(dt_b, e_g)).astype(BF16)
        wm = (xg * dot(dtw_b, e_g)).astype(BF16)
        yoff = dot(cg, sg.astype(BF16)) * (dot(eac_hi, e_g) + dot(eac_lo, e_g))
        cd = (dot(cd_hi, e_g) + (dot(cd_mid, e_g) + dot(cd_lo, e_g)))[0:1]
        yd = []
        for k in range(SSD_HPG):
            h = off + SSD_HPG * g + k
            seg = acum[:, h:h + 1] - acum_t[h:h + 1, :]
            lmat = jnp.where(valid, jnp.exp(jnp.minimum(seg, 0.0)), 0.0)
            m = (cb * lmat).astype(BF16)
            yd.append(dot(m, xdt_b[:, k * SSD_HEAD_DIM:(k + 1) * SSD_HEAD_DIM]))
        y_ref[g] = (jnp.concatenate(yd, axis=1) + yoff).astype(y_ref.dtype)
        st_ref[g] = sg * cd + dot(bt, wm)

    @pl.when(ci == nc - 1)
    def _():
        fin_ref[0] = st_ref[...]


def _ssd_scan(xbc, slab_x, slab_b, slab_c, dt, dt_bias, a_log, init, seq, rev):
    rows = xbc.shape[1]
    nb = rows // seq
    nc = seq // SSD_CHUNK
    gw = SSD_HPG * SSD_HEAD_DIM
    xs = SSD_INNER // LANES
    cpos = (lambda b, c: b * nc + (nc - 1 - c)) if rev else (lambda b, c: b * nc + c)
    expand = np.zeros((2 * SSD_HEADS, SSD_INNER), np.float32)
    for h in range(SSD_HEADS):
        expand[(SSD_HEADS if rev else 0) + h, h * SSD_HEAD_DIM:(h + 1) * SSD_HEAD_DIM] = 1.0
    return pl.pallas_call(
        functools.partial(_ssd_kernel, rev=rev, nc=nc),
        grid=(nb, nc),
        in_specs=[pl.BlockSpec((xs, SSD_CHUNK, LANES), lambda b, c: (slab_x // xs, cpos(b, c), 0)),
                  pl.BlockSpec((SSD_GROUPS, SSD_CHUNK, LANES), lambda b, c: (slab_b // SSD_GROUPS, cpos(b, c), 0)),
                  pl.BlockSpec((SSD_GROUPS, SSD_CHUNK, LANES), lambda b, c: (slab_c // SSD_GROUPS, cpos(b, c), 0)),
                  pl.BlockSpec((SSD_CHUNK, LANES), lambda b, c: (cpos(b, c), 0)),
                  pl.BlockSpec((1, LANES), lambda b, c: (0, 0)),
                  pl.BlockSpec((1, LANES), lambda b, c: (0, 0)),
                  pl.BlockSpec((2 * SSD_HEADS, SSD_INNER), lambda b, c: (0, 0)),
                  pl.BlockSpec((1, SSD_GROUPS, SSD_STATE, gw), lambda b, c: (b, 0, 0, 0))],
        out_specs=[pl.BlockSpec((SSD_GROUPS, SSD_CHUNK, gw), lambda b, c: (0, cpos(b, c), 0)),
                   pl.BlockSpec((1, SSD_GROUPS, SSD_STATE, gw), lambda b, c: (b, 0, 0, 0))],
        out_shape=[jax.ShapeDtypeStruct((SSD_GROUPS, rows, gw), BF16),
                   jax.ShapeDtypeStruct((nb, SSD_GROUPS, SSD_STATE, gw), F32)],
        scratch_shapes=[pltpu.VMEM((SSD_GROUPS, SSD_STATE, gw), F32)],
        compiler_params=_cparams(("arbitrary", "arbitrary")),
    )(xbc, xbc, xbc, dt, dt_bias.reshape(1, LANES), a_log.reshape(1, LANES),
      jnp.asarray(expand).astype(BF16), init)


def _ssdnorm_kernel(yf_ref, yb_ref, x_ref, z_ref, d_ref, w_ref, o_ref):
    nsl = x_ref.shape[0]
    x = jnp.concatenate([x_ref[i] for i in range(nsl)], axis=1).astype(F32)
    z = jnp.concatenate([z_ref[i] for i in range(nsl)], axis=1).astype(F32)
    y = yf_ref[0].astype(F32) + yb_ref[0].astype(F32) + x * d_ref[0]
    gx = y * _silu(z)
    ms = jnp.mean(gx * gx, axis=-1, keepdims=True)
    o_ref[...] = (gx * lax.rsqrt(ms + 1e-5) * w_ref[0]).astype(o_ref.dtype)


def _ssd_norm(yf, yb, xbc, proj, slab_z, d_ch, w, tr):
    g, rows, gw = yf.shape
    nsl = gw // LANES
    return pl.pallas_call(
        _ssdnorm_kernel,
        grid=(g, rows // tr),
        in_specs=[pl.BlockSpec((1, tr, gw), lambda q, i: (q, i, 0)),
                  pl.BlockSpec((1, tr, gw), lambda q, i: (q, i, 0)),
                  pl.BlockSpec((nsl, tr, LANES), lambda q, i: (q, i, 0)),
                  pl.BlockSpec((nsl, tr, LANES), lambda q, i: (slab_z // nsl + q, i, 0)),
                  pl.BlockSpec((1, 1, gw), lambda q, i: (q, 0, 0)),
                  pl.BlockSpec((1, 1, gw), lambda q, i: (q, 0, 0))],
        out_specs=pl.BlockSpec((tr, gw), lambda q, i: (i, q)),
        out_shape=jax.ShapeDtypeStruct((rows, g * gw), BF16),
        compiler_params=_cparams(("arbitrary", "arbitrary")),
    )(yf, yb, xbc, proj, d_ch.reshape(g, 1, gw), w.reshape(g, 1, gw))


def _merge_kernel(yhy_ref, yss_ref, whh_ref, wss_ref, gh_ref, gs_ref, o_ref):
    a1 = jnp.concatenate([yhy_ref[i] for i in range(yhy_ref.shape[0])], axis=1).astype(BF16)
    p1 = jnp.dot(a1, whh_ref[...], preferred_element_type=F32)
    p2 = jnp.dot(yss_ref[...], wss_ref[...], preferred_element_type=F32)
    gh = jnp.concatenate([gh_ref[i] for i in range(gh_ref.shape[0])], axis=1).astype(F32)
    gs = jnp.concatenate([gs_ref[i] for i in range(gs_ref.shape[0])], axis=1).astype(F32)
    o_ref[...] = (_sigmoid(gh) * p1 + _sigmoid(gs) * p2).astype(o_ref.dtype)


def _branch_merge(yhy, yss, w_hy, w_ssd, proj, slab_gh, slab_gs, tm, tn):
    nsl_k = yhy.shape[0]
    rows = yhy.shape[1]
    n = w_hy.shape[1]
    ts = tn // LANES
    return pl.pallas_call(
        _merge_kernel,
        grid=(rows // tm, n // tn),
        in_specs=[pl.BlockSpec((nsl_k, tm, LANES), lambda i, j: (0, i, 0)),
                  pl.BlockSpec((tm, yss.shape[1]), lambda i, j: (i, 0)),
                  pl.BlockSpec((w_hy.shape[0], tn), lambda i, j: (0, j)),
                  pl.BlockSpec((w_ssd.shape[0], tn), lambda i, j: (0, j)),
                  pl.BlockSpec((ts, tm, LANES), lambda i, j: (slab_gh // ts + j, i, 0)),
                  pl.BlockSpec((ts, tm, LANES), lambda i, j: (slab_gs // ts + j, i, 0))],
        out_specs=pl.BlockSpec((tm, tn), lambda i, j: (i, j)),
        out_shape=jax.ShapeDtypeStruct((rows, n), BF16),
        compiler_params=_cparams(("arbitrary", "arbitrary")),
    )(yhy, yss, w_hy, w_ssd, proj, proj)


def _outproj_ln_kernel(a_ref, w_ref, x_ref, gt_ref, g_ref, b_ref, sc_ref, sh_ref, h_ref, m_ref, mt_ref):
    y = jnp.dot(a_ref[0], w_ref[...], preferred_element_type=F32)
    h = _normalize_rows(DEEPNORM_ALPHA * x_ref[0] + gt_ref[0] * y, 1e-6) * g_ref[...] + b_ref[...]
    h_ref[0] = h
    m = _normalize_rows(h, 1e-6) * (1.0 + sc_ref[0]) + sh_ref[0]
    m_ref[0] = m.astype(m_ref.dtype)
    mt_ref[...] = m.T.astype(mt_ref.dtype)


def _outproj_ln(merged, w_out, x, gate, ln_g, ln_b, scale2, shift2, tm):
    b, l, d = x.shape
    nt = l // tm
    vec = pl.BlockSpec((1, 1, d), lambda i, j: (i, 0, 0))
    cvec = pl.BlockSpec((1, d), lambda i, j: (0, 0))
    tile = pl.BlockSpec((1, tm, d), lambda i, j: (i, j, 0))
    return pl.pallas_call(
        _outproj_ln_kernel,
        grid=(b, nt),
        in_specs=[tile, pl.BlockSpec((d, d), lambda i, j: (0, 0)), tile, vec, cvec, cvec, vec, vec],
        out_specs=[tile, tile, pl.BlockSpec((d, tm), lambda i, j: (0, i * nt + j))],
        out_shape=[jax.ShapeDtypeStruct((b, l, d), F32), jax.ShapeDtypeStruct((b, l, d), BF16),
                   jax.ShapeDtypeStruct((d, b * l), BF16)],
        compiler_params=_cparams(("arbitrary", "arbitrary")),
    )(merged, w_out, x, gate, ln_g.reshape(1, d), ln_b.reshape(1, d), scale2, shift2)


def _resid_ln_kernel(x_ref, y_ref, gt_ref, g_ref, b_ref, o_ref):
    o_ref[0] = _normalize_rows(DEEPNORM_ALPHA * x_ref[0] + gt_ref[0] * y_ref[0], 1e-6) * g_ref[...] + b_ref[...]


def _resid_ln(x, y, gate, ln_g, ln_b, tm):
    b, l, d = x.shape
    tile = pl.BlockSpec((1, tm, d), lambda i, j: (i, j, 0))
    return pl.pallas_call(
        _resid_ln_kernel,
        grid=(b, l // tm),
        in_specs=[tile, tile, pl.BlockSpec((1, 1, d), lambda i, j: (i, 0, 0)),
                  pl.BlockSpec((1, d), lambda i, j: (0, 0)), pl.BlockSpec((1, d), lambda i, j: (0, 0))],
        out_specs=tile,
        out_shape=jax.ShapeDtypeStruct((b, l, d), F32),
        compiler_params=_cparams(("arbitrary", "arbitrary")),
    )(x, y, gate, ln_g.reshape(1, d), ln_b.reshape(1, d))


def _peer_scores_kernel(a_ref, wq_ref, k_ref, s_ref):
    q = jnp.dot(a_ref[...], wq_ref[...], preferred_element_type=F32)
    hd = PEER_DKEY // 2
    nt = (((1,), (1,)), ((), ()))
    for h in range(PEER_HEADS):
        qn = _normalize_rows(q[:, h * PEER_DKEY:(h + 1) * PEER_DKEY], 1e-6).astype(BF16)
        s_ref[h, 0] = lax.dot_general(k_ref[h, 0], qn[:, :hd], nt, preferred_element_type=F32)
        s_ref[h, 1] = lax.dot_general(k_ref[h, 1], qn[:, hd:], nt, preferred_element_type=F32)


def _peer_scores(m2, wq, subkeys, tm):
    t, d = m2.shape
    return pl.pallas_call(
        _peer_scores_kernel,
        grid=(t // tm,),
        in_specs=[pl.BlockSpec((tm, d), lambda i: (i, 0)),
                  pl.BlockSpec(wq.shape, lambda i: (0, 0)),
                  pl.BlockSpec(subkeys.shape, lambda i: (0, 0, 0, 0))],
        out_specs=pl.BlockSpec((PEER_HEADS, 2, PEER_NKEYS, tm), lambda i: (0, 0, 0, i)),
        out_shape=jax.ShapeDtypeStruct((PEER_HEADS, 2, PEER_NKEYS, t), F32),
        compiler_params=_cparams(("arbitrary",)),
    )(m2, wq, subkeys)


def _top_values(x, k):
    out = []
    for _ in range(k):
        m = jnp.max(x, axis=0, keepdims=True)
        out.append(m)
        x = jnp.where(x == m, -jnp.inf, x)
    return out


def _peer_stats_kernel(s_ref, rk_ref, lim_ref, p1_ref, p2_ref):
    k = PEER_TOPK
    for h in range(PEER_HEADS):
        s1 = s_ref[h, 0]
        s2 = s_ref[h, 1]
        v1 = _top_values(s1, k)
        v2 = _top_values(s2, k)
        v2m = jnp.concatenate(v2, axis=0)
        grid = [v1[a] + v2m[:k // (a + 1)] for a in range(k)]
        n_cand = sum(g.shape[0] for g in grid)
        pad = jnp.full((-n_cand % 8, s1.shape[1]), -jnp.inf, F32)
        best = _top_values(jnp.concatenate(grid + [pad], axis=0), k)
        z = best[0] * 0.0
        for c in best:
            z = z + jnp.exp(c - best[0])
        lim = jnp.zeros_like(s1)
        rk = jnp.zeros_like(s2)
        for a in range(k):
            n_sel = jnp.sum(jnp.where(grid[a] >= best[k - 1], 1.0, 0.0), axis=0, keepdims=True)
            lim = lim + jnp.where(s1 == v1[a], n_sel, 0.0)
            rk = rk + jnp.where(s2 < v2[a], 1.0, 0.0)
        lim_ref[h] = lim
        rk_ref[h] = rk.astype(rk_ref.dtype)
        p1_ref[h] = jnp.exp(s1 - v1[0])
        p2_ref[h] = (0.5 * jnp.exp(s2 - v2[0]) / z).astype(p2_ref.dtype)


def _peer_stats(scores, tm):
    t = scores.shape[-1]
    hk = pl.BlockSpec((PEER_HEADS, PEER_NKEYS, tm), lambda i: (0, 0, i))
    f32 = jax.ShapeDtypeStruct((PEER_HEADS, PEER_NKEYS, t), F32)
    b16 = jax.ShapeDtypeStruct((PEER_HEADS, PEER_NKEYS, t), BF16)
    return pl.pallas_call(
        _peer_stats_kernel,
        grid=(t // tm,),
        in_specs=[pl.BlockSpec((PEER_HEADS, 2, PEER_NKEYS, tm), lambda i: (0, 0, 0, i))],
        out_specs=[hk, hk, hk, hk],
        out_shape=[b16, f32, f32, b16],
        compiler_params=_cparams(("arbitrary",)),
    )(scores)


def _peer_dense_kernel(ht_ref, u_ref, vt_ref, rk_ref, lim_ref, p1_ref, p2_ref, o_ref, acc_ref, g_ref, *, ne1):
    j = pl.program_id(1)

    @pl.when(j == 0)
    def _():
        acc_ref[...] = jnp.zeros_like(acc_ref)

    zero = jnp.zeros((), BF16)
    for r in range(ne1):
        e1 = j * ne1 + r
        rows = slice(r * PEER_NKEYS, (r + 1) * PEER_NKEYS)
        tile = (BF16_ROWS, g_ref.shape[1])
        lim_rows = [jnp.broadcast_to(lim_ref[h, pl.ds(e1, 1), :], tile).astype(BF16) for h in range(PEER_HEADS)]
        p1_rows = [jnp.broadcast_to(p1_ref[h, pl.ds(e1, 1), :], tile).astype(BF16) for h in range(PEER_HEADS)]
        reps = PEER_NKEYS // BF16_ROWS
        for tt in range(g_ref.shape[1] // LANES):
            ls = slice(tt * LANES, (tt + 1) * LANES)
            gacc = None
            for h in range(PEER_HEADS):
                lim = pltpu.repeat(lim_rows[h][:, ls], reps, axis=0)
                p1 = pltpu.repeat(p1_rows[h][:, ls], reps, axis=0)
                hs = slice(h * PEER_NKEYS, (h + 1) * PEER_NKEYS)
                w = jnp.where(rk_ref[hs, ls] < lim, p2_ref[hs, ls] * p1, zero)
                gacc = w if gacc is None else gacc + w
            g_ref[rows, ls] = gacc
    act = jnp.dot(u_ref[...], ht_ref[...], preferred_element_type=F32)
    gelu2 = act * (1.0 + jnp.tanh(math.sqrt(2.0 / math.pi) * (act + 0.044715 * (act * act * act))))
    pt = gelu2.astype(BF16) * g_ref[...]
    acc_ref[...] += jnp.dot(vt_ref[...], pt, preferred_element_type=F32)

    @pl.when(j == pl.num_programs(1) - 1)
    def _():
        o_ref[...] = acc_ref[...].T


def _peer_dense(m2t, u_b, vt_b, rk, lim, p1, p2, tm, te):
    d, t = m2t.shape
    ne = u_b.shape[0]
    hk = pl.BlockSpec((PEER_HEADS, PEER_NKEYS, tm), lambda i, j: (0, 0, i))
    flat = pl.BlockSpec((PEER_HEADS * PEER_NKEYS, tm), lambda i, j: (0, i))
    return pl.pallas_call(
        functools.partial(_peer_dense_kernel, ne1=te // PEER_NKEYS),
        grid=(t // tm, ne // te),
        in_specs=[pl.BlockSpec((d, tm), lambda i, j: (0, i)),
                  pl.BlockSpec((te, d), lambda i, j: (j, 0)),
                  pl.BlockSpec((d, te), lambda i, j: (0, j)),
                  flat, hk, hk, flat],
        out_specs=pl.BlockSpec((tm, d), lambda i, j: (i, 0)),
        out_shape=jax.ShapeDtypeStruct((t, d), F32),
        scratch_shapes=[pltpu.VMEM((d, tm), F32), pltpu.VMEM((te, tm), BF16)],
        compiler_params=_cparams(("arbitrary", "arbitrary")),
    )(m2t, u_b, vt_b, rk.reshape(PEER_HEADS * PEER_NKEYS, t), lim, p1, p2.reshape(PEER_HEADS * PEER_NKEYS, t))


SL_X = 0
SL_B = SL_X + SSD_INNER // LANES
SL_C = SL_B + SSD_GROUPS * SSD_STATE // LANES
SL_Z = SL_C + SSD_GROUPS * SSD_STATE // LANES
SL_GH = SL_Z + SSD_INNER // LANES
SL_GS = SL_GH + D_MODEL // LANES


def kernel(x, c, ctx, c_ctx, ada_w, ada_b, w_in, hy_conv_w, hy_conv_b, hy_ffn_w1, hy_ffn_b1, hy_freq1,
           hy_ffn_w2, hy_ffn_b2, hy_freq2, hy_ffn_w3, hy_skip, ssd_conv_w, ssd_conv_b, ssd_a_log,
           ssd_dt_bias, ssd_d, ssd_norm_w, w_branch_hy, w_branch_ssd, w_out, ln1_g, ln1_b, peer_wq,
           peer_subkeys, peer_u, peer_v, ln2_g, ln2_b):
    nb, seq, d = x.shape
    lc = ctx.shape[1]
    t = nb * seq
    p = dict(hy_ffn_w1=hy_ffn_w1[0], hy_ffn_b1=hy_ffn_b1[0], hy_freq1=hy_freq1[0], hy_ffn_w2=hy_ffn_w2[0],
             hy_ffn_b2=hy_ffn_b2[0], hy_freq2=hy_freq2[0], hy_ffn_w3=hy_ffn_w3[0])

    cc = jnp.zeros((8, d), F32).at[:nb].set(c).at[nb].set(c_ctx)
    mod = _ada(cc, ada_w[0], ada_b[0])
    sh1, sc1, gt1, sh2, sc2, gt2 = [mod[:nb, i * d:(i + 1) * d].reshape(nb, 1, d) for i in range(6)]
    csh1 = mod[nb:nb + 1, 0:d].reshape(1, 1, d)
    csc1 = mod[nb:nb + 1, d:2 * d].reshape(1, 1, d)

    w = w_in[0]
    w_hy = w[:, :OFF_X].astype(BF16)
    w_rest = jnp.concatenate([w[:, OFF_X:OFF_DT], w[:, OFF_Z:]], axis=1).astype(BF16)
    w_dt = w[:, OFF_DT:OFF_Z].astype(BF16)
    w_ctx = w[:, OFF_X:OFF_C].astype(BF16)

    m_ctx = _modulate(ctx, csc1, csh1, 256).reshape(nb * lc, d)
    pc = _matmul(m_ctx, w_ctx, 512, 1024, slab=True)
    dtc = _matmul(m_ctx, w_dt, 512, LANES)
    nxb = (OFF_C - OFF_X) // LANES
    cw = ssd_conv_w[0].reshape(3, -1, LANES).transpose(1, 0, 2)
    cbias = ssd_conv_b[0].reshape(-1, 1, LANES)
    xbc_ctx = _conv3(pc, 0, nxb, cw[:nxb], cbias[:nxb], lc, True)
    gw = SSD_HPG * SSD_HEAD_DIM
    zero_state = jnp.zeros((nb, SSD_GROUPS, SSD_STATE, gw), F32)
    dt_bias = ssd_dt_bias[0]
    a_log = ssd_a_log[0]
    xs_sl = SSD_INNER // LANES
    _, s_f = _ssd_scan(xbc_ctx, 0, xs_sl, xs_sl, dtc, dt_bias, a_log, zero_state, lc, False)
    _, s_b = _ssd_scan(xbc_ctx, 0, xs_sl, xs_sl, dtc, dt_bias, a_log, zero_state, lc, True)

    m1 = _modulate(x, sc1, sh1, 256).reshape(t, d)
    proj_hy = _matmul(m1, w_hy, 1024, 1024, slab=True)
    proj = _matmul(m1, w_rest, 1024, 1024, out_dtype=BF16, slab=True)
    dtp = _matmul(m1, w_dt, 1024, LANES)

    hw = hy_conv_w[0].reshape(3, -1, LANES).transpose(1, 0, 2)
    hb = hy_conv_b[0].reshape(-1, 1, LANES)
    nh = HY_WIDTH // LANES
    ug = _hyconv(proj_hy, 3 * nh, hw, hb, seq)
    hspec = _hyena_filter_spectra(p, seq)
    zg = _longconv(ug, ug, 0, nh, hy_skip[0, 0], hspec, 0, seq, False)
    y_hy = _longconv(zg, ug, 0, 2 * nh, hy_skip[0, 1], hspec, 1, seq, True)

    xbc = _conv3(proj, SL_X, SL_Z - SL_X, cw, cbias, seq, True)
    y_f, _ = _ssd_scan(xbc, 0, xs_sl, xs_sl + SSD_GROUPS, dtp, dt_bias, a_log, s_f, seq, False)
    y_b, _ = _ssd_scan(xbc, 0, xs_sl, xs_sl + SSD_GROUPS, dtp, dt_bias, a_log, s_b, seq, True)
    d_ch = jnp.repeat(ssd_d[0, 0] + ssd_d[0, 1], SSD_HEAD_DIM)
    y_ss = _ssd_norm(y_f, y_b, xbc, proj, SL_Z, d_ch, ssd_norm_w[0], 512)

    merged = _branch_merge(y_hy, y_ss, w_branch_hy[0].astype(BF16), w_branch_ssd[0].astype(BF16),
                           proj, SL_GH, SL_GS, 512, 512)
    h1, m2, m2t = _outproj_ln(merged.reshape(nb, seq, d), w_out[0].astype(BF16), x, gt1, ln1_g[0], ln1_b[0],
                              sc2, sh2, 256)

    m2f = m2.reshape(t, d)
    scores = _peer_scores(m2f, peer_wq[0].astype(BF16), peer_subkeys[0].astype(BF16), 256)
    rk, lim, p1, p2 = _peer_stats(scores, 256)
    ffn = _peer_dense(m2t, peer_u[0].astype(BF16), peer_v[0].T.astype(BF16), rk, lim, p1, p2, 512, 1024)
    return _resid_ln(h1, ffn.reshape(nb, seq, d), gt2, ln2_g[0], ln2_b[0], 256)
```

```python
import functools
import math

import numpy as np
import jax
import jax.numpy as jnp
from jax import lax
from jax.experimental import pallas as pl
from jax.experimental.pallas import tpu as pltpu

F32 = jnp.float32
BF16 = jnp.bfloat16

D_MODEL = 2048
DEPTH = 1
CTX_LEN = 256
HY_WIDTH = D_MODEL
HY_BANDS = 16
HY_EMB = 2 * HY_BANDS + 1
HY_FFN = 64
HY_MIN_DECAY = math.log(1e-2) / 1.5
HY_MAX_DECAY = math.log(1e-2) / 0.3
SSD_INNER = 2 * D_MODEL
SSD_HEAD_DIM = 64
SSD_HEADS = SSD_INNER // SSD_HEAD_DIM
SSD_STATE = 128
SSD_GROUPS = 8
SSD_HPG = SSD_HEADS // SSD_GROUPS
SSD_CHUNK = 128
OFF_X = 3 * HY_WIDTH
OFF_B = OFF_X + SSD_INNER
OFF_C = OFF_B + SSD_GROUPS * SSD_STATE
OFF_DT = OFF_C + SSD_GROUPS * SSD_STATE
OFF_Z = OFF_DT + 2 * SSD_HEADS
OFF_GATE = OFF_Z + SSD_INNER
N_COLS = OFF_GATE + 2 * D_MODEL
PEER_HEADS = 8
PEER_NKEYS = 128
PEER_EXPERTS = PEER_NKEYS * PEER_NKEYS
PEER_TOPK = 16
PEER_DKEY = 256
DEEPNORM_ALPHA = (2.0 * DEPTH) ** 0.25

LANES = 128
VMEM_LIMIT = 56 * 1024 * 1024

FFT_N1 = 128
SA_PITCH_PAD = 8
SB_PITCH = FFT_N1 + 8


def _cparams(sem, vmem=VMEM_LIMIT):
    return pltpu.CompilerParams(dimension_semantics=sem, vmem_limit_bytes=vmem)


def _silu(x):
    return x * (1.0 / (1.0 + jnp.exp(-x)))


def _sigmoid(x):
    return 1.0 / (1.0 + jnp.exp(-x))


def _softplus(x):
    return jnp.maximum(x, 0.0) + jnp.log(1.0 + jnp.exp(-jnp.abs(x)))


def _gelu_tanh(x):
    return 0.5 * x * (1.0 + jnp.tanh(math.sqrt(2.0 / math.pi) * (x + 0.044715 * (x * x * x))))


def _normalize_rows(x, eps):
    mu = jnp.mean(x, axis=-1, keepdims=True)
    xc = x - mu
    var = jnp.mean(xc * xc, axis=-1, keepdims=True)
    return xc * lax.rsqrt(var + eps)


def _ada_kernel(c_ref, w_ref, b_ref, o_ref):
    a = _silu(c_ref[...]).astype(BF16)
    o_ref[...] = jnp.dot(a, w_ref[...].astype(BF16), preferred_element_type=F32) + b_ref[...]


def _ada(cc, ada_w, ada_b):
    rows, d = cc.shape
    n = ada_w.shape[1]
    tn = 1024
    return pl.pallas_call(
        _ada_kernel,
        grid=(n // tn,),
        in_specs=[pl.BlockSpec((rows, d), lambda j: (0, 0)),
                  pl.BlockSpec((d, tn), lambda j: (0, j)),
                  pl.BlockSpec((1, tn), lambda j: (0, j))],
        out_specs=pl.BlockSpec((rows, tn), lambda j: (0, j)),
        out_shape=jax.ShapeDtypeStruct((rows, n), F32),
        compiler_params=_cparams(("arbitrary",)),
    )(cc, ada_w, ada_b.reshape(1, n))


def _mod_kernel(x_ref, sc_ref, sh_ref, o_ref):
    xn = _normalize_rows(x_ref[0], 1e-6)
    o_ref[0] = (xn * (1.0 + sc_ref[0]) + sh_ref[0]).astype(o_ref.dtype)


def _modulate(x, scale, shift, tr):
    b, l, d = x.shape
    per_batch = scale.shape[0] == b
    smap = (lambda i, j: (i, 0, 0)) if per_batch else (lambda i, j: (0, 0, 0))
    return pl.pallas_call(
        _mod_kernel,
        grid=(b, l // tr),
        in_specs=[pl.BlockSpec((1, tr, d), lambda i, j: (i, j, 0)),
                  pl.BlockSpec((1, 1, d), smap),
                  pl.BlockSpec((1, 1, d), smap)],
        out_specs=pl.BlockSpec((1, tr, d), lambda i, j: (i, j, 0)),
        out_shape=jax.ShapeDtypeStruct((b, l, d), BF16),
        compiler_params=_cparams(("arbitrary", "arbitrary")),
    )(x, scale, shift)


def _mm_kernel(a_ref, b_ref, o_ref, *, slab):
    acc = jnp.dot(a_ref[...], b_ref[...], preferred_element_type=F32)
    if slab:
        for s in range(o_ref.shape[0]):
            o_ref[s] = acc[:, s * LANES:(s + 1) * LANES].astype(o_ref.dtype)
    else:
        o_ref[...] = acc.astype(o_ref.dtype)


def _matmul(a, b, tm, tn, out_dtype=F32, slab=False, ncols=None):
    m, k = a.shape
    n = b.shape[1] if ncols is None else ncols
    if slab:
        out_shape = jax.ShapeDtypeStruct((n // LANES, m, LANES), out_dtype)
        out_spec = pl.BlockSpec((tn // LANES, tm, LANES), lambda i, j: (j, i, 0))
    else:
        out_shape = jax.ShapeDtypeStruct((m, n), out_dtype)
        out_spec = pl.BlockSpec((tm, tn), lambda i, j: (i, j))
    return pl.pallas_call(
        functools.partial(_mm_kernel, slab=slab),
        grid=(m // tm, n // tn),
        in_specs=[pl.BlockSpec((tm, k), lambda i, j: (i, 0)),
                  pl.BlockSpec((k, tn), lambda i, j: (0, j))],
        out_specs=out_spec,
        out_shape=out_shape,
        compiler_params=_cparams(("arbitrary", "arbitrary")),
    )(a, b)


def _conv3_kernel(x_ref, w_ref, b_ref, o_ref, *, silu):
    x = x_ref[0].astype(F32)
    l = x.shape[0]
    row = lax.broadcasted_iota(jnp.int32, x.shape, 0)
    prev = jnp.where(row == 0, 0.0, pltpu.roll(x, 1, 0))
    nxt = jnp.where(row == l - 1, 0.0, pltpu.roll(x, l - 1, 0))
    w = w_ref[0]
    y = prev * w[0:1] + x * w[1:2] + nxt * w[2:3] + b_ref[0]
    if silu:
        y = _silu(y)
    o_ref[0] = y.astype(o_ref.dtype)


def _conv3(p_slab, slab0, nslab, w, bias, seq, silu):
    rows = p_slab.shape[1]
    nb = rows // seq
    return pl.pallas_call(
        functools.partial(_conv3_kernel, silu=silu),
        grid=(nslab, nb),
        in_specs=[pl.BlockSpec((1, seq, LANES), lambda s, b: (slab0 + s, b, 0)),
                  pl.BlockSpec((1, 3, LANES), lambda s, b: (s, 0, 0)),
                  pl.BlockSpec((1, 1, LANES), lambda s, b: (s, 0, 0))],
        out_specs=pl.BlockSpec((1, seq, LANES), lambda s, b: (s, b, 0)),
        out_shape=jax.ShapeDtypeStruct((nslab, rows, LANES), BF16),
        compiler_params=_cparams(("arbitrary", "arbitrary")),
    )(p_slab, w, bias)


def _hyconv_kernel(x_ref, w_ref, b_ref, o_ref, *, seq, nb):
    nj = seq // FFT_N1

    def gather(n1):
        return jnp.concatenate(
            [x_ref[0, pl.ds(n1 + b * seq, nj, stride=FFT_N1), :] for b in range(nb)], axis=0)

    w = w_ref[0]
    w0, w1, w2, bias = w[0:1], w[1:2], w[2:3], b_ref[0]
    jrow = lax.broadcasted_iota(jnp.int32, (nb * nj, LANES), 0) % nj

    def shift_down(g):
        return jnp.where(jrow == 0, 0.0, pltpu.roll(g, 1, 0))

    def shift_up(g):
        return jnp.where(jrow == nj - 1, 0.0, pltpu.roll(g, nb * nj - 1, 0))

    g_first = gather(0)
    g_second = gather(1)
    g_last = gather(FFT_N1 - 1)
    dt = o_ref.dtype
    o_ref[0, 0] = (shift_down(g_last) * w0 + g_first * w1 + g_second * w2 + bias).astype(dt)

    def body(n1, carry):
        g_prev, g_cur = carry
        g_next = gather(n1 + 1)
        o_ref[0, n1] = (g_prev * w0 + g_cur * w1 + g_next * w2 + bias).astype(dt)
        return g_cur, g_next

    g_prev, g_cur = lax.fori_loop(1, FFT_N1 - 1, body, (g_first, g_second), unroll=6)
    o_ref[0, FFT_N1 - 1] = (g_prev * w0 + g_cur * w1 + shift_up(g_first) * w2 + bias).astype(dt)


def _hyconv(p_slab, nslab, w, bias, seq):
    rows = p_slab.shape[1]
    nb = rows // seq
    nj = seq // FFT_N1
    return pl.pallas_call(
        functools.partial(_hyconv_kernel, seq=seq, nb=nb),
        grid=(nslab,),
        in_specs=[pl.BlockSpec((1, rows, LANES), lambda s: (s, 0, 0)),
                  pl.BlockSpec((1, 3, LANES), lambda s: (s, 0, 0)),
                  pl.BlockSpec((1, 1, LANES), lambda s: (s, 0, 0))],
        out_specs=pl.BlockSpec((1, FFT_N1, nb * nj, LANES), lambda s: (s, 0, 0, 0)),
        out_shape=jax.ShapeDtypeStruct((nslab, FFT_N1, nb * nj, LANES), BF16),
        compiler_params=_cparams(("arbitrary",)),
    )(p_slab, w, bias)


@functools.lru_cache(maxsize=None)
def _dft_tables(seq):
    nj = seq // FFT_N1
    n2 = nj + nj // 2
    n = FFT_N1 * n2
    out0 = nj // 2
    n1 = np.arange(FFT_N1, dtype=np.float64)
    k2 = np.arange(n2, dtype=np.float64)
    jn = np.arange(nj, dtype=np.float64)
    ang = -2.0 * np.pi * (k2[None, :, None] * jn[None, None, :] / n2 + n1[:, None, None] * k2[None, :, None] / n)
    cr, ci = np.cos(ang), np.sin(ang)
    fa = np.concatenate([np.concatenate([cr, -ci], axis=2), np.concatenate([ci, cr], axis=2)], axis=1)
    k1 = np.arange(FFT_N1, dtype=np.float64)
    angb = -2.0 * np.pi * np.outer(k1, n1) / FFT_N1
    dr, di = np.cos(angb), np.sin(angb)
    fb = np.block([[dr, -di], [di, dr]])
    fbi = np.block([[dr, di], [-di, dr]])
    angi = 2.0 * np.pi * (n1[:, None, None] * k2[None, None, :] / n + (out0 + jn)[None, :, None] * k2[None, None, :] / n2)
    er, ei = np.cos(angi) / n, np.sin(angi) / n
    fai = np.concatenate([np.concatenate([er, -ei], axis=2), np.concatenate([ei, er], axis=2)], axis=1)
    return (fa.astype(np.float32), fb.astype(np.float32), fbi.astype(np.float32), fai.astype(np.float32), n2)


@functools.lru_cache(maxsize=None)
def _filter_consts(seq):
    t = np.linspace(0.0, 1.0, seq, dtype=np.float32)[:, None].astype(np.float64)
    w = 2.0 * np.pi * np.arange(seq, dtype=np.float32)[:, None].astype(np.float64) / seq
    bands = np.linspace(1e-4, HY_BANDS - 1, HY_BANDS, dtype=np.float32)[None, :].astype(np.float64)
    feats = np.concatenate([t, np.cos(bands * w), -np.sin(bands * w)], axis=-1)
    feats_p = np.zeros((seq, LANES), np.float32)
    feats_p[:, :HY_EMB] = feats
    dist = (np.abs(np.arange(seq) - seq // 2).astype(np.float32) / np.float32(seq / 2.0)).reshape(seq, 1)
    deltas = np.abs(np.linspace(HY_MIN_DECAY, HY_MAX_DECAY, HY_WIDTH, dtype=np.float32))
    return feats_p, dist.astype(np.float32), deltas.astype(np.float32)


def _dot_f32(a, b):
    def split(x):
        hi = x.astype(BF16)
        r1 = x - hi.astype(F32)
        mid = r1.astype(BF16)
        lo = (r1 - mid.astype(F32)).astype(BF16)
        return hi, mid, lo
    a0, a1, a2 = split(a)
    b0, b1, b2 = split(b)
    d = functools.partial(jnp.dot, preferred_element_type=F32)
    return (d(a0, b0) + (d(a0, b1) + d(a1, b0)) + (d(a0, b2) + d(a1, b1) + d(a2, b0)))


def _filter_kernel(feats_ref, w1_ref, b1_ref, f1_ref, w2_ref, b2_ref, f2_ref, w3_ref, dist_ref, delta_ref,
                   fa_ref, fb_ref, o_ref, hid_ref, h_ref, sar_ref, sai_ref, *, seq, n2, pitch):
    nj = seq // FFT_N1

    @pl.when((pl.program_id(0) == 0) & (pl.program_id(1) == 0))
    def _():
        h1 = jnp.sin(f1_ref[...] * (_dot_f32(feats_ref[...], w1_ref[...]) + b1_ref[...]))
        hid_ref[...] = jnp.sin(f2_ref[...] * (_dot_f32(h1, w2_ref[...]) + b2_ref[...])).astype(BF16)

    ns = h_ref.shape[0]
    lanes = lambda f: jnp.concatenate([f(s) for s in range(ns)], axis=1)
    w3 = lanes(lambda s: w3_ref[0, s]).astype(BF16)
    h = jnp.dot(hid_ref[...], w3, preferred_element_type=F32)
    h = h * jnp.exp(-dist_ref[...] * lanes(lambda s: delta_ref[s]))
    h = h / (jnp.sum(jnp.abs(h), axis=0, keepdims=True) + 1e-6)
    for s in range(ns):
        h_ref[s] = h[:, s * LANES:(s + 1) * LANES]

    def stage_a(n1, c):
        g = lanes(lambda s: h_ref[s, pl.ds(n1, nj, stride=FFT_N1), :]).astype(BF16)
        y = jnp.dot(fa_ref[n1], g, preferred_element_type=F32)
        base = pl.multiple_of(n1 * pitch, 8)
        for s in range(ns):
            sar_ref[s, pl.ds(base, n2), :] = y[:n2, s * LANES:(s + 1) * LANES]
            sai_ref[s, pl.ds(base, n2), :] = y[n2:, s * LANES:(s + 1) * LANES]
        return c

    lax.fori_loop(0, FFT_N1, stage_a, 0, unroll=8)

    def stage_b(k2, c):
        yr = lanes(lambda s: sar_ref[s, pl.ds(k2, FFT_N1, stride=pitch), :])
        yi = lanes(lambda s: sai_ref[s, pl.ds(k2, FFT_N1, stride=pitch), :])
        op = jnp.concatenate([yr, yi], axis=0).astype(BF16)
        res = jnp.dot(fb_ref[...], op, preferred_element_type=F32).astype(o_ref.dtype)
        for s in range(ns):
            o_ref[0, s, k2] = res[:, s * LANES:(s + 1) * LANES]
        return c

    lax.fori_loop(0, n2, stage_b, 0, unroll=4)


def _hyena_filter_spectra(p, seq):
    fa, fb, _, _, n2 = _dft_tables(seq)
    nj = seq // FFT_N1
    pitch = n2 + SA_PITCH_PAD
    feats, dist, deltas = _filter_consts(seq)
    nsl = HY_WIDTH // LANES

    def padk(w, rows):
        return jnp.zeros((rows, w.shape[1]), F32).at[:w.shape[0]].set(w)

    def padn(v):
        return jnp.zeros((1, LANES), F32).at[0, :v.shape[0]].set(v)

    w1 = jnp.zeros((LANES, LANES), F32).at[:HY_EMB, :HY_FFN].set(p['hy_ffn_w1'])
    w2 = jnp.zeros((LANES, LANES), F32).at[:HY_FFN, :HY_FFN].set(p['hy_ffn_w2'])
    w3 = padk(p['hy_ffn_w3'], LANES).reshape(LANES, 2, nsl, LANES).transpose(1, 2, 0, 3)
    fa_real = jnp.asarray(fa[:, :, :nj]).astype(BF16)
    const = lambda shape: pl.BlockSpec(shape, lambda o, s: (0,) * len(shape))
    ns = 2
    return pl.pallas_call(
        functools.partial(_filter_kernel, seq=seq, n2=n2, pitch=pitch),
        grid=(2, nsl // ns),
        in_specs=[const((seq, LANES)), const((LANES, LANES)), const((1, LANES)), const((1, LANES)),
                  const((LANES, LANES)), const((1, LANES)), const((1, LANES)),
                  pl.BlockSpec((1, ns, LANES, LANES), lambda o, s: (o, s, 0, 0)),
                  const((seq, 1)),
                  pl.BlockSpec((ns, 1, LANES), lambda o, s: (s, 0, 0)),
                  const((FFT_N1, 2 * n2, nj)), const((2 * FFT_N1, 2 * FFT_N1))],
        out_specs=pl.BlockSpec((1, ns, n2, 2 * FFT_N1, LANES), lambda o, s: (o, s, 0, 0, 0)),
        out_shape=jax.ShapeDtypeStruct((2, nsl, n2, 2 * FFT_N1, LANES), BF16),
        scratch_shapes=[pltpu.VMEM((seq, LANES), BF16), pltpu.VMEM((ns, seq, LANES), F32),
                        pltpu.VMEM((ns, FFT_N1 * pitch, LANES), F32),
                        pltpu.VMEM((ns, FFT_N1 * pitch, LANES), F32)],
        compiler_params=_cparams(("arbitrary", "arbitrary")),
    )(jnp.asarray(feats), w1, padn(p['hy_ffn_b1']), padn(p['hy_freq1']), w2, padn(p['hy_ffn_b2']),
      padn(p['hy_freq2']), w3, jnp.asarray(dist), jnp.asarray(deltas).reshape(nsl, 1, LANES),
      fa_real, jnp.asarray(fb).astype(BF16))


def _longconv_kernel(x_ref, g_ref, skip_ref, h_ref, fa_ref, fb_ref, fbi_ref, fai_ref, o_ref,
                     sar_ref, sai_ref, sbr_ref, sbi_ref, *, seq, n2, pitch, natural_out):
    nj = seq // FFT_N1

    def stage_a(n1, c):
        y = jnp.dot(fa_ref[n1], x_ref[0, n1].astype(BF16), preferred_element_type=F32)
        base = pl.multiple_of(n1 * pitch, 8)
        sar_ref[pl.ds(base, n2), :] = y[:n2]
        sai_ref[pl.ds(base, n2), :] = y[n2:]
        return c

    lax.fori_loop(0, FFT_N1, stage_a, 0, unroll=8)

    def stage_b(k2, c):
        yr = sar_ref[pl.ds(k2, FFT_N1, stride=pitch), :]
        yi = sai_ref[pl.ds(k2, FFT_N1, stride=pitch), :]
        xf = jnp.dot(fb_ref[...], jnp.concatenate([yr, yi], axis=0).astype(BF16), preferred_element_type=F32)
        xr, xi = xf[:FFT_N1], xf[FFT_N1:]
        hf = h_ref[0, k2].astype(F32)
        hr, hi = hf[:FFT_N1], hf[FFT_N1:]
        z = jnp.concatenate([xr * hr - xi * hi, xr * hi + xi * hr], axis=0).astype(BF16)
        w = jnp.dot(fbi_ref[...], z, preferred_element_type=F32)
        base = pl.multiple_of(k2 * SB_PITCH, 8)
        sbr_ref[pl.ds(base, FFT_N1), :] = w[:FFT_N1]
        sbi_ref[pl.ds(base, FFT_N1), :] = w[FFT_N1:]
        return c

    lax.fori_loop(0, n2, stage_b, 0, unroll=4)

    skip = skip_ref[0]

    def stage_c(n1, c):
        wr = sbr_ref[pl.ds(n1, n2, stride=SB_PITCH), :]
        wi = sbi_ref[pl.ds(n1, n2, stride=SB_PITCH), :]
        y = jnp.dot(fai_ref[n1], jnp.concatenate([wr, wi], axis=0).astype(BF16), preferred_element_type=F32)
        res = g_ref[0, n1].astype(F32) * (y + x_ref[0, n1].astype(F32) * skip)
        if natural_out:
            for b in range(res.shape[0] // nj):
                o_ref[0, pl.ds(n1 + b * seq, nj, stride=FFT_N1), :] = res[b * nj:(b + 1) * nj]
        else:
            o_ref[0, n1] = res.astype(o_ref.dtype)
        return c

    lax.fori_loop(0, FFT_N1, stage_c, 0, unroll=8)


def _longconv(xg, gg, slab_x, slab_g, skip, hspec, order, seq, natural_out):
    fa, fb, fbi, fai, n2 = _dft_tables(seq)
    pitch = n2 + SA_PITCH_PAD
    nsl = HY_WIDTH // LANES
    rows = xg.shape[2]
    nb = rows // (seq // FFT_N1)
    const = lambda shape: pl.BlockSpec(shape, lambda s: (0,) * len(shape))
    if natural_out:
        out_shape = jax.ShapeDtypeStruct((nsl, nb * seq, LANES), F32)
        out_spec = pl.BlockSpec((1, nb * seq, LANES), lambda s: (s, 0, 0))
    else:
        out_shape = jax.ShapeDtypeStruct((nsl, FFT_N1, rows, LANES), BF16)
        out_spec = pl.BlockSpec((1, FFT_N1, rows, LANES), lambda s: (s, 0, 0, 0))
    return pl.pallas_call(
        functools.partial(_longconv_kernel, seq=seq, n2=n2, pitch=pitch, natural_out=natural_out),
        grid=(nsl,),
        in_specs=[pl.BlockSpec((1, FFT_N1, rows, LANES), lambda s: (slab_x + s, 0, 0, 0)),
                  pl.BlockSpec((1, FFT_N1, rows, LANES), lambda s: (slab_g + s, 0, 0, 0)),
                  pl.BlockSpec((1, 1, LANES), lambda s: (s, 0, 0)),
                  pl.BlockSpec((None, 1, n2, 2 * FFT_N1, LANES), lambda s: (order, s, 0, 0, 0)),
                  const(fa.shape), const(fb.shape), const(fbi.shape), const(fai.shape)],
        out_specs=out_spec,
        out_shape=out_shape,
        scratch_shapes=[pltpu.VMEM((FFT_N1 * pitch, LANES), F32), pltpu.VMEM((FFT_N1 * pitch, LANES), F32),
                        pltpu.VMEM((n2 * SB_PITCH, LANES), F32), pltpu.VMEM((n2 * SB_PITCH, LANES), F32)],
        compiler_params=_cparams(("arbitrary",)),
    )(xg, gg, skip.reshape(nsl, 1, LANES), hspec,
      jnp.asarray(fa).astype(BF16), jnp.asarray(fb).astype(BF16),
      jnp.asarray(fbi).astype(BF16), jnp.asarray(fai).astype(BF16))


def _dot_exact_lhs(lhs_bf16, x):
    hi = x.astype(BF16)
    r1 = x - hi.astype(F32)
    mid = r1.astype(BF16)
    lo = (r1 - mid.astype(F32)).astype(BF16)
    d = functools.partial(jnp.dot, preferred_element_type=F32)
    return d(lhs_bf16, hi) + (d(lhs_bf16, mid) + d(lhs_bf16, lo))


def _ssd_kernel(x_ref, b_ref, c_ref, dt_ref, bias_ref, alog_ref, e_ref, init_ref, y_ref, fin_ref, st_ref,
                *, rev, nc):
    ci = pl.program_id(1)
    gw = SSD_HPG * SSD_HEAD_DIM
    dot = functools.partial(jnp.dot, preferred_element_type=F32)

    @pl.when(ci == 0)
    def _():
        st_ref[...] = init_ref[0]

    off = SSD_HEADS if rev else 0
    dt_all = _softplus(dt_ref[...] + bias_ref[...])
    a_all = dt_all * (-jnp.exp(alog_ref[...]))
    r_i = lax.broadcasted_iota(jnp.int32, (SSD_CHUNK, SSD_CHUNK), 0)
    c_i = lax.broadcasted_iota(jnp.int32, (SSD_CHUNK, SSD_CHUNK), 1)
    valid = (c_i >= r_i) if rev else (c_i <= r_i)
    tri = jnp.where(valid, 1.0, 0.0).astype(BF16)
    acum = _dot_exact_lhs(tri, a_all)
    src_t = (acum - jnp.log(dt_all)).T
    tot = acum[0:1, :] if rev else acum[SSD_CHUNK - 1:SSD_CHUNK, :]
    dtw_b = (dt_all * jnp.exp(tot - acum)).astype(BF16)
    cd8 = jnp.broadcast_to(jnp.exp(tot), (8, LANES))
    cd_hi = cd8.astype(BF16)
    cd_r = cd8 - cd_hi.astype(F32)
    cd_mid = cd_r.astype(BF16)
    cd_lo = (cd_r - cd_mid.astype(F32)).astype(BF16)
    lo_half = c_i < SSD_HEAD_DIM
    zero_b = jnp.zeros((), BF16)

    for g in range(SSD_GROUPS):
        bt = b_ref[g].astype(F32).T.astype(BF16)
        cg = c_ref[g].astype(BF16)
        cb = dot(cg, bt)
        for sl in range(gw // LANES):
            col = slice(sl * LANES, (sl + 1) * LANES)
            e_s = e_ref[:, g * gw + sl * LANES:g * gw + (sl + 1) * LANES]
            xb = x_ref[(gw // LANES) * g + sl]
            ms, ea = [], []
            for hh in range(2):
                h = off + SSD_HPG * g + 2 * sl + hh
                a_col = jnp.broadcast_to(acum[:, h:h + 1], (SSD_CHUNK, SSD_CHUNK))
                seg = jnp.where(valid, a_col - src_t[h:h + 1, :], -1e30)
                ms.append((cb * jnp.exp(seg)).astype(BF16))
                ea.append(jnp.exp(a_col))
            rhs = jnp.concatenate([jnp.where(lo_half, xb, zero_b), jnp.where(lo_half, zero_b, xb)], axis=0)
            yd = dot(jnp.concatenate(ms, axis=1), rhs)
            sgs = st_ref[g, :, col]
            yoff = dot(cg, sgs.astype(BF16)) * jnp.where(lo_half, ea[0], ea[1])
            y_ref[g, :, col] = (yd + yoff).astype(y_ref.dtype)
            wm = (xb.astype(F32) * dot(dtw_b, e_s)).astype(BF16)
            cd = (dot(cd_hi, e_s) + (dot(cd_mid, e_s) + dot(cd_lo, e_s)))[0:1]
            st_ref[g, :, col] = sgs * cd + dot(bt, wm)

    @pl.when(ci == nc - 1)
    def _():
        fin_ref[0] = st_ref[...]


def _ssd_scan(xbc, slab_x, slab_b, slab_c, dt, dt_bias, a_log, init, seq, rev):
    rows = xbc.shape[1]
    nb = rows // seq
    nc = seq // SSD_CHUNK
    gw = SSD_HPG * SSD_HEAD_DIM
    xs = SSD_INNER // LANES
    cpos = (lambda b, c: b * nc + (nc - 1 - c)) if rev else (lambda b, c: b * nc + c)
    expand = np.zeros((2 * SSD_HEADS, SSD_INNER), np.float32)
    for h in range(SSD_HEADS):
        expand[(SSD_HEADS if rev else 0) + h, h * SSD_HEAD_DIM:(h + 1) * SSD_HEAD_DIM] = 1.0
    return pl.pallas_call(
        functools.partial(_ssd_kernel, rev=rev, nc=nc),
        grid=(nb, nc),
        in_specs=[pl.BlockSpec((xs, SSD_CHUNK, LANES), lambda b, c: (slab_x // xs, cpos(b, c), 0)),
                  pl.BlockSpec((SSD_GROUPS, SSD_CHUNK, LANES), lambda b, c: (slab_b // SSD_GROUPS, cpos(b, c), 0)),
                  pl.BlockSpec((SSD_GROUPS, SSD_CHUNK, LANES), lambda b, c: (slab_c // SSD_GROUPS, cpos(b, c), 0)),
                  pl.BlockSpec((SSD_CHUNK, LANES), lambda b, c: (cpos(b, c), 0)),
                  pl.BlockSpec((1, LANES), lambda b, c: (0, 0)),
                  pl.BlockSpec((1, LANES), lambda b, c: (0, 0)),
                  pl.BlockSpec((2 * SSD_HEADS, SSD_INNER), lambda b, c: (0, 0)),
                  pl.BlockSpec((1, SSD_GROUPS, SSD_STATE, gw), lambda b, c: (b, 0, 0, 0))],
        out_specs=[pl.BlockSpec((SSD_GROUPS, SSD_CHUNK, gw), lambda b, c: (0, cpos(b, c), 0)),
                   pl.BlockSpec((1, SSD_GROUPS, SSD_STATE, gw), lambda b, c: (b, 0, 0, 0))],
        out_shape=[jax.ShapeDtypeStruct((SSD_GROUPS, rows, gw), BF16),
                   jax.ShapeDtypeStruct((nb, SSD_GROUPS, SSD_STATE, gw), F32)],
        scratch_shapes=[pltpu.VMEM((SSD_GROUPS, SSD_STATE, gw), F32)],
        compiler_params=_cparams(("arbitrary", "arbitrary")),
    )(xbc, xbc, xbc, dt, dt_bias.reshape(1, LANES), a_log.reshape(1, LANES),
      jnp.asarray(expand).astype(BF16), init)


def _ssdnorm_kernel(yf_ref, yb_ref, x_ref, z_ref, d_ref, w_ref, o_ref):
    nsl = x_ref.shape[0]
    x = jnp.concatenate([x_ref[i] for i in range(nsl)], axis=1).astype(F32)
    z = jnp.concatenate([z_ref[i] for i in range(nsl)], axis=1).astype(F32)
    y = yf_ref[0].astype(F32) + yb_ref[0].astype(F32) + x * d_ref[0]
    gx = y * _silu(z)
    ms = jnp.mean(gx * gx, axis=-1, keepdims=True)
    o_ref[...] = (gx * lax.rsqrt(ms + 1e-5) * w_ref[0]).astype(o_ref.dtype)


def _ssd_norm(yf, yb, xbc, proj, slab_z, d_ch, w, tr):
    g, rows, gw = yf.shape
    nsl = gw // LANES
    return pl.pallas_call(
        _ssdnorm_kernel,
        grid=(g, rows // tr),
        in_specs=[pl.BlockSpec((1, tr, gw), lambda q, i: (q, i, 0)),
                  pl.BlockSpec((1, tr, gw), lambda q, i: (q, i, 0)),
                  pl.BlockSpec((nsl, tr, LANES), lambda q, i: (q, i, 0)),
                  pl.BlockSpec((nsl, tr, LANES), lambda q, i: (slab_z // nsl + q, i, 0)),
                  pl.BlockSpec((1, 1, gw), lambda q, i: (q, 0, 0)),
                  pl.BlockSpec((1, 1, gw), lambda q, i: (q, 0, 0))],
        out_specs=pl.BlockSpec((tr, gw), lambda q, i: (i, q)),
        out_shape=jax.ShapeDtypeStruct((rows, g * gw), BF16),
        compiler_params=_cparams(("arbitrary", "arbitrary")),
    )(yf, yb, xbc, proj, d_ch.reshape(g, 1, gw), w.reshape(g, 1, gw))


def _merge_kernel(yhy_ref, yss_ref, whh_ref, wss_ref, gh_ref, gs_ref, o_ref):
    a1 = jnp.concatenate([yhy_ref[i] for i in range(yhy_ref.shape[0])], axis=1).astype(BF16)
    p1 = jnp.dot(a1, whh_ref[...], preferred_element_type=F32)
    p2 = jnp.dot(yss_ref[...], wss_ref[...], preferred_element_type=F32)
    gh = jnp.concatenate([gh_ref[i] for i in range(gh_ref.shape[0])], axis=1).astype(F32)
    gs = jnp.concatenate([gs_ref[i] for i in range(gs_ref.shape[0])], axis=1).astype(F32)
    o_ref[...] = (_sigmoid(gh) * p1 + _sigmoid(gs) * p2).astype(o_ref.dtype)


def _branch_merge(yhy, yss, w_hy, w_ssd, proj, slab_gh, slab_gs, tm, tn):
    nsl_k = yhy.shape[0]
    rows = yhy.shape[1]
    n = w_hy.shape[1]
    ts = tn // LANES
    return pl.pallas_call(
        _merge_kernel,
        grid=(rows // tm, n // tn),
        in_specs=[pl.BlockSpec((nsl_k, tm, LANES), lambda i, j: (0, i, 0)),
                  pl.BlockSpec((tm, yss.shape[1]), lambda i, j: (i, 0)),
                  pl.BlockSpec((w_hy.shape[0], tn), lambda i, j: (0, j)),
                  pl.BlockSpec((w_ssd.shape[0], tn), lambda i, j: (0, j)),
                  pl.BlockSpec((ts, tm, LANES), lambda i, j: (slab_gh // ts + j, i, 0)),
                  pl.BlockSpec((ts, tm, LANES), lambda i, j: (slab_gs // ts + j, i, 0))],
        out_specs=pl.BlockSpec((tm, tn), lambda i, j: (i, j)),
        out_shape=jax.ShapeDtypeStruct((rows, n), BF16),
        compiler_params=_cparams(("arbitrary", "arbitrary")),
    )(yhy, yss, w_hy, w_ssd, proj, proj)


def _outproj_ln_kernel(a_ref, w_ref, x_ref, gt_ref, g_ref, b_ref, sc_ref, sh_ref, h_ref, m_ref, mt_ref):
    y = jnp.dot(a_ref[0], w_ref[...], preferred_element_type=F32)
    h = _normalize_rows(DEEPNORM_ALPHA * x_ref[0] + gt_ref[0] * y, 1e-6) * g_ref[...] + b_ref[...]
    h_ref[0] = h
    m = _normalize_rows(h, 1e-6) * (1.0 + sc_ref[0]) + sh_ref[0]
    m_ref[0] = m.astype(m_ref.dtype)
    mt_ref[...] = m.T.astype(mt_ref.dtype)


def _outproj_ln(merged, w_out, x, gate, ln_g, ln_b, scale2, shift2, tm):
    b, l, d = x.shape
    nt = l // tm
    vec = pl.BlockSpec((1, 1, d), lambda i, j: (i, 0, 0))
    cvec = pl.BlockSpec((1, d), lambda i, j: (0, 0))
    tile = pl.BlockSpec((1, tm, d), lambda i, j: (i, j, 0))
    return pl.pallas_call(
        _outproj_ln_kernel,
        grid=(b, nt),
        in_specs=[tile, pl.BlockSpec((d, d), lambda i, j: (0, 0)), tile, vec, cvec, cvec, vec, vec],
        out_specs=[tile, tile, pl.BlockSpec((d, tm), lambda i, j: (0, i * nt + j))],
        out_shape=[jax.ShapeDtypeStruct((b, l, d), F32), jax.ShapeDtypeStruct((b, l, d), BF16),
                   jax.ShapeDtypeStruct((d, b * l), BF16)],
        compiler_params=_cparams(("arbitrary", "arbitrary")),
    )(merged, w_out, x, gate, ln_g.reshape(1, d), ln_b.reshape(1, d), scale2, shift2)


def _resid_ln_kernel(x_ref, y_ref, gt_ref, g_ref, b_ref, o_ref):
    o_ref[0] = _normalize_rows(DEEPNORM_ALPHA * x_ref[0] + gt_ref[0] * y_ref[0], 1e-6) * g_ref[...] + b_ref[...]


def _resid_ln(x, y, gate, ln_g, ln_b, tm):
    b, l, d = x.shape
    tile = pl.BlockSpec((1, tm, d), lambda i, j: (i, j, 0))
    return pl.pallas_call(
        _resid_ln_kernel,
        grid=(b, l // tm),
        in_specs=[tile, tile, pl.BlockSpec((1, 1, d), lambda i, j: (i, 0, 0)),
                  pl.BlockSpec((1, d), lambda i, j: (0, 0)), pl.BlockSpec((1, d), lambda i, j: (0, 0))],
        out_specs=tile,
        out_shape=jax.ShapeDtypeStruct((b, l, d), F32),
        compiler_params=_cparams(("arbitrary", "arbitrary")),
    )(x, y, gate, ln_g.reshape(1, d), ln_b.reshape(1, d))


def _peer_scores_kernel(a_ref, wq_ref, k_ref, s_ref):
    q = jnp.dot(a_ref[...], wq_ref[...], preferred_element_type=F32)
    hd = PEER_DKEY // 2
    nt = (((1,), (1,)), ((), ()))
    for h in range(PEER_HEADS):
        qn = _normalize_rows(q[:, h * PEER_DKEY:(h + 1) * PEER_DKEY], 1e-6).astype(BF16)
        s_ref[h, 0] = lax.dot_general(k_ref[h, 0], qn[:, :hd], nt, preferred_element_type=F32)
        s_ref[h, 1] = lax.dot_general(k_ref[h, 1], qn[:, hd:], nt, preferred_element_type=F32)


def _peer_scores(m2, wq, subkeys, tm):
    t, d = m2.shape
    return pl.pallas_call(
        _peer_scores_kernel,
        grid=(t // tm,),
        in_specs=[pl.BlockSpec((tm, d), lambda i: (i, 0)),
                  pl.BlockSpec(wq.shape, lambda i: (0, 0)),
                  pl.BlockSpec(subkeys.shape, lambda i: (0, 0, 0, 0))],
        out_specs=pl.BlockSpec((PEER_HEADS, 2, PEER_NKEYS, tm), lambda i: (0, 0, 0, i)),
        out_shape=jax.ShapeDtypeStruct((PEER_HEADS, 2, PEER_NKEYS, t), F32),
        compiler_params=_cparams(("arbitrary",)),
    )(m2, wq, subkeys)


def _top_values(x, k):
    out = []
    for _ in range(k):
        m = jnp.max(x, axis=0, keepdims=True)
        out.append(m)
        x = jnp.where(x == m, -jnp.inf, x)
    return out


def _top_values_ranked(x, k):
    out = []
    big = 2.0 ** 100
    for a in range(k):
        m = jnp.max(x, axis=0, keepdims=True)
        out.append(m)
        x = jnp.where(x == m, -big * (1.0 + a / 64.0), x)
    rank = jnp.where(x <= -big, (x * (-1.0 / big) - 1.0) * 64.0, float(k))
    return out, rank


def _peer_stats_kernel(s_ref, rk_ref, lim_ref, p1_ref, p2_ref):
    k = PEER_TOPK
    for h in range(PEER_HEADS):
        s1 = s_ref[h, 0]
        s2 = s_ref[h, 1]
        v1, r1 = _top_values_ranked(s1, k)
        v2, rk = _top_values_ranked(s2, k)
        v2m = jnp.concatenate(v2, axis=0)
        grid = [v1[a] + v2m[:k // (a + 1)] for a in range(k)]
        n_cand = sum(g.shape[0] for g in grid)
        pad = jnp.full((-n_cand % 8, s1.shape[1]), -jnp.inf, F32)
        best = _top_values(jnp.concatenate(grid + [pad], axis=0), k)
        z = best[0] * 0.0
        for c in best:
            z = z + jnp.exp(c - best[0])
        lim = jnp.zeros_like(s1)
        for a in range(k):
            n_sel = jnp.sum(jnp.where(grid[a] >= best[k - 1], 1.0, 0.0), axis=0, keepdims=True)
            lim = jnp.where(r1 == float(a), n_sel, lim)
        lim_ref[h] = lim
        rk_ref[h] = rk.astype(rk_ref.dtype)
        p1_ref[h] = jnp.exp(s1 - v1[0])
        p2_ref[h] = (0.5 * jnp.exp(s2 - v2[0]) / z).astype(p2_ref.dtype)


def _peer_stats(scores, tm):
    t = scores.shape[-1]
    hk = pl.BlockSpec((PEER_HEADS, PEER_NKEYS, tm), lambda i: (0, 0, i))
    f32 = jax.ShapeDtypeStruct((PEER_HEADS, PEER_NKEYS, t), F32)
    b16 = jax.ShapeDtypeStruct((PEER_HEADS, PEER_NKEYS, t), BF16)
    return pl.pallas_call(
        _peer_stats_kernel,
        grid=(t // tm,),
        in_specs=[pl.BlockSpec((PEER_HEADS, 2, PEER_NKEYS, tm), lambda i: (0, 0, 0, i))],
        out_specs=[hk, hk, hk, hk],
        out_shape=[b16, f32, f32, b16],
        compiler_params=_cparams(("arbitrary",)),
    )(scores)


def _peer_dense_kernel(ht_ref, u_ref, vt_ref, rk_ref, lim_ref, p1_ref, p2_ref, o_ref, acc_ref, g_ref, *, ne1):
    j = pl.program_id(1)

    @pl.when(j == 0)
    def _():
        acc_ref[...] = jnp.zeros_like(acc_ref)

    zero = jnp.zeros((), BF16)
    for r in range(ne1):
        e1 = j * ne1 + r
        rows = slice(r * PEER_NKEYS, (r + 1) * PEER_NKEYS)
        lim_rows = [lim_ref[h, pl.ds(e1, 1), :].astype(BF16) for h in range(PEER_HEADS)]
        p1_rows = [p1_ref[h, pl.ds(e1, 1), :].astype(BF16) for h in range(PEER_HEADS)]
        for tt in range(g_ref.shape[1] // LANES):
            ls = slice(tt * LANES, (tt + 1) * LANES)
            gacc = None
            for h in range(PEER_HEADS):
                hs = slice(h * PEER_NKEYS, (h + 1) * PEER_NKEYS)
                w = jnp.where(rk_ref[hs, ls] < lim_rows[h][:, ls], p2_ref[hs, ls] * p1_rows[h][:, ls], zero)
                gacc = w if gacc is None else gacc + w
            g_ref[rows, ls] = gacc
    act = jnp.dot(u_ref[...], ht_ref[...], preferred_element_type=F32)
    gelu2 = act * (1.0 + jnp.tanh(math.sqrt(2.0 / math.pi) * (act + 0.044715 * (act * act * act))))
    pt = gelu2.astype(BF16) * g_ref[...]
    acc_ref[...] += jnp.dot(vt_ref[...], pt, preferred_element_type=F32)

    @pl.when(j == pl.num_programs(1) - 1)
    def _():
        o_ref[...] = acc_ref[...].T


def _peer_dense(m2t, u_b, vt_b, rk, lim, p1, p2, tm, te):
    d, t = m2t.shape
    ne = u_b.shape[0]
    hk = pl.BlockSpec((PEER_HEADS, PEER_NKEYS, tm), lambda i, j: (0, 0, i))
    flat = pl.BlockSpec((PEER_HEADS * PEER_NKEYS, tm), lambda i, j: (0, i))
    return pl.pallas_call(
        functools.partial(_peer_dense_kernel, ne1=te // PEER_NKEYS),
        grid=(t // tm, ne // te),
        in_specs=[pl.BlockSpec((d, tm), lambda i, j: (0, i)),
                  pl.BlockSpec((te, d), lambda i, j: (j, 0)),
                  pl.BlockSpec((d, te), lambda i, j: (0, j)),
                  flat, hk, hk, flat],
        out_specs=pl.BlockSpec((tm, d), lambda i, j: (i, 0)),
        out_shape=jax.ShapeDtypeStruct((t, d), F32),
        scratch_shapes=[pltpu.VMEM((d, tm), F32), pltpu.VMEM((te, tm), BF16)],
        compiler_params=_cparams(("arbitrary", "arbitrary")),
    )(m2t, u_b, vt_b, rk.reshape(PEER_HEADS * PEER_NKEYS, t), lim, p1, p2.reshape(PEER_HEADS * PEER_NKEYS, t))


SL_X = 0
SL_B = SL_X + SSD_INNER // LANES
SL_C = SL_B + SSD_GROUPS * SSD_STATE // LANES
SL_Z = SL_C + SSD_GROUPS * SSD_STATE // LANES
SL_GH = SL_Z + SSD_INNER // LANES
SL_GS = SL_GH + D_MODEL // LANES


def kernel(x, c, ctx, c_ctx, ada_w, ada_b, w_in, hy_conv_w, hy_conv_b, hy_ffn_w1, hy_ffn_b1, hy_freq1,
           hy_ffn_w2, hy_ffn_b2, hy_freq2, hy_ffn_w3, hy_skip, ssd_conv_w, ssd_conv_b, ssd_a_log,
           ssd_dt_bias, ssd_d, ssd_norm_w, w_branch_hy, w_branch_ssd, w_out, ln1_g, ln1_b, peer_wq,
           peer_subkeys, peer_u, peer_v, ln2_g, ln2_b):
    nb, seq, d = x.shape
    lc = ctx.shape[1]
    t = nb * seq
    p = dict(hy_ffn_w1=hy_ffn_w1[0], hy_ffn_b1=hy_ffn_b1[0], hy_freq1=hy_freq1[0], hy_ffn_w2=hy_ffn_w2[0],
             hy_ffn_b2=hy_ffn_b2[0], hy_freq2=hy_freq2[0], hy_ffn_w3=hy_ffn_w3[0])

    cc = jnp.zeros((8, d), F32).at[:nb].set(c).at[nb].set(c_ctx)
    mod = _ada(cc, ada_w[0], ada_b[0])
    sh1, sc1, gt1, sh2, sc2, gt2 = [mod[:nb, i * d:(i + 1) * d].reshape(nb, 1, d) for i in range(6)]
    csh1 = mod[nb:nb + 1, 0:d].reshape(1, 1, d)
    csc1 = mod[nb:nb + 1, d:2 * d].reshape(1, 1, d)

    w = w_in[0]
    w_hy = w[:, :OFF_X].astype(BF16)
    w_rest = jnp.concatenate([w[:, OFF_X:OFF_DT], w[:, OFF_Z:]], axis=1).astype(BF16)
    w_dt = w[:, OFF_DT:OFF_Z].astype(BF16)

    m_ctx = _modulate(ctx, csc1, csh1, 256).reshape(nb * lc, d)
    pc = _matmul(m_ctx, w_rest, 512, 1024, slab=True, ncols=OFF_C - OFF_X)
    dtc = _matmul(m_ctx, w_dt, 512, LANES)
    nxb = (OFF_C - OFF_X) // LANES
    cw = ssd_conv_w[0].reshape(3, -1, LANES).transpose(1, 0, 2)
    cbias = ssd_conv_b[0].reshape(-1, 1, LANES)
    xbc_ctx = _conv3(pc, 0, nxb, cw[:nxb], cbias[:nxb], lc, True)
    gw = SSD_HPG * SSD_HEAD_DIM
    zero_state = jnp.zeros((nb, SSD_GROUPS, SSD_STATE, gw), F32)
    dt_bias = ssd_dt_bias[0]
    a_log = ssd_a_log[0]
    xs_sl = SSD_INNER // LANES
    _, s_f = _ssd_scan(xbc_ctx, 0, xs_sl, xs_sl, dtc, dt_bias, a_log, zero_state, lc, False)
    _, s_b = _ssd_scan(xbc_ctx, 0, xs_sl, xs_sl, dtc, dt_bias, a_log, zero_state, lc, True)

    m1 = _modulate(x, sc1, sh1, 256).reshape(t, d)
    proj_hy = _matmul(m1, w_hy, 1024, 1024, slab=True)
    proj = _matmul(m1, w_rest, 1024, 1024, out_dtype=BF16, slab=True)
    dtp = _matmul(m1, w_dt, 1024, LANES)

    hw = hy_conv_w[0].reshape(3, -1, LANES).transpose(1, 0, 2)
    hb = hy_conv_b[0].reshape(-1, 1, LANES)
    nh = HY_WIDTH // LANES
    ug = _hyconv(proj_hy, 3 * nh, hw, hb, seq)
    hspec = _hyena_filter_spectra(p, seq)
    zg = _longconv(ug, ug, 0, nh, hy_skip[0, 0], hspec, 0, seq, False)
    y_hy = _longconv(zg, ug, 0, 2 * nh, hy_skip[0, 1], hspec, 1, seq, True)

    xbc = _conv3(proj, SL_X, SL_Z - SL_X, cw, cbias, seq, True)
    y_f, _ = _ssd_scan(xbc, 0, xs_sl, xs_sl + SSD_GROUPS, dtp, dt_bias, a_log, s_f, seq, False)
    y_b, _ = _ssd_scan(xbc, 0, xs_sl, xs_sl + SSD_GROUPS, dtp, dt_bias, a_log, s_b, seq, True)
    d_ch = jnp.repeat(ssd_d[0, 0] + ssd_d[0, 1], SSD_HEAD_DIM)
    y_ss = _ssd_norm(y_f, y_b, xbc, proj, SL_Z, d_ch, ssd_norm_w[0], 512)

    merged = _branch_merge(y_hy, y_ss, w_branch_hy[0].astype(BF16), w_branch_ssd[0].astype(BF16),
                           proj, SL_GH, SL_GS, 512, 512)
    h1, m2, m2t = _outproj_ln(merged.reshape(nb, seq, d), w_out[0].astype(BF16), x, gt1, ln1_g[0], ln1_b[0],
                              sc2, sh2, 512)

    m2f = m2.reshape(t, d)
    scores = _peer_scores(m2f, peer_wq[0].astype(BF16), peer_subkeys[0].astype(BF16), 256)
    rk, lim, p1, p2 = _peer_stats(scores, 256)
    ffn = _peer_dense(m2t, peer_u[0].astype(BF16), peer_v[0].T.astype(BF16), rk, lim, p1, p2, 512, 1024)
    return _resid_ln(h1, ffn.reshape(nb, seq, d), gt2, ln2_g[0], ln2_b[0], 256)
```

```python
import functools
import math

import numpy as np
import jax
import jax.numpy as jnp
from jax import lax
from jax.experimental import pallas as pl
from jax.experimental.pallas import tpu as pltpu

F32 = jnp.float32
BF16 = jnp.bfloat16

D_MODEL = 2048
DEPTH = 1
CTX_LEN = 256
HY_WIDTH = D_MODEL
HY_BANDS = 16
HY_EMB = 2 * HY_BANDS + 1
HY_FFN = 64
HY_MIN_DECAY = math.log(1e-2) / 1.5
HY_MAX_DECAY = math.log(1e-2) / 0.3
SSD_INNER = 2 * D_MODEL
SSD_HEAD_DIM = 64
SSD_HEADS = SSD_INNER // SSD_HEAD_DIM
SSD_STATE = 128
SSD_GROUPS = 8
SSD_HPG = SSD_HEADS // SSD_GROUPS
SSD_CHUNK = 128
OFF_X = 3 * HY_WIDTH
OFF_B = OFF_X + SSD_INNER
OFF_C = OFF_B + SSD_GROUPS * SSD_STATE
OFF_DT = OFF_C + SSD_GROUPS * SSD_STATE
OFF_Z = OFF_DT + 2 * SSD_HEADS
OFF_GATE = OFF_Z + SSD_INNER
N_COLS = OFF_GATE + 2 * D_MODEL
PEER_HEADS = 8
PEER_NKEYS = 128
PEER_EXPERTS = PEER_NKEYS * PEER_NKEYS
PEER_TOPK = 16
PEER_DKEY = 256
DEEPNORM_ALPHA = (2.0 * DEPTH) ** 0.25

LANES = 128
VMEM_LIMIT = 56 * 1024 * 1024

FFT_N1 = 128
SA_PITCH_PAD = 8
SB_PITCH = FFT_N1 + 8


def _cparams(sem, vmem=VMEM_LIMIT):
    return pltpu.CompilerParams(dimension_semantics=sem, vmem_limit_bytes=vmem)


def _silu(x):
    return x * (1.0 / (1.0 + jnp.exp(-x)))


def _sigmoid(x):
    return 1.0 / (1.0 + jnp.exp(-x))


def _softplus(x):
    return jnp.maximum(x, 0.0) + jnp.log(1.0 + jnp.exp(-jnp.abs(x)))


def _gelu_tanh(x):
    return 0.5 * x * (1.0 + jnp.tanh(math.sqrt(2.0 / math.pi) * (x + 0.044715 * (x * x * x))))


def _normalize_rows(x, eps):
    mu = jnp.mean(x, axis=-1, keepdims=True)
    xc = x - mu
    var = jnp.mean(xc * xc, axis=-1, keepdims=True)
    return xc * lax.rsqrt(var + eps)


def _ada_kernel(c_ref, w_ref, b_ref, o_ref):
    a = _silu(c_ref[...]).astype(BF16)
    o_ref[...] = jnp.dot(a, w_ref[...].astype(BF16), preferred_element_type=F32) + b_ref[...]


def _ada(cc, ada_w, ada_b):
    rows, d = cc.shape
    n = ada_w.shape[1]
    tn = 1024
    return pl.pallas_call(
        _ada_kernel,
        grid=(n // tn,),
        in_specs=[pl.BlockSpec((rows, d), lambda j: (0, 0)),
                  pl.BlockSpec((d, tn), lambda j: (0, j)),
                  pl.BlockSpec((1, tn), lambda j: (0, j))],
        out_specs=pl.BlockSpec((rows, tn), lambda j: (0, j)),
        out_shape=jax.ShapeDtypeStruct((rows, n), F32),
        compiler_params=_cparams(("arbitrary",)),
    )(cc, ada_w, ada_b.reshape(1, n))


def _mod_kernel(x_ref, sc_ref, sh_ref, o_ref):
    xn = _normalize_rows(x_ref[0], 1e-6)
    o_ref[0] = (xn * (1.0 + sc_ref[0]) + sh_ref[0]).astype(o_ref.dtype)


def _modulate(x, scale, shift, tr):
    b, l, d = x.shape
    per_batch = scale.shape[0] == b
    smap = (lambda i, j: (i, 0, 0)) if per_batch else (lambda i, j: (0, 0, 0))
    return pl.pallas_call(
        _mod_kernel,
        grid=(b, l // tr),
        in_specs=[pl.BlockSpec((1, tr, d), lambda i, j: (i, j, 0)),
                  pl.BlockSpec((1, 1, d), smap),
                  pl.BlockSpec((1, 1, d), smap)],
        out_specs=pl.BlockSpec((1, tr, d), lambda i, j: (i, j, 0)),
        out_shape=jax.ShapeDtypeStruct((b, l, d), BF16),
        compiler_params=_cparams(("arbitrary", "arbitrary")),
    )(x, scale, shift)


def _mm_kernel(a_ref, b_ref, o_ref, *, slab):
    acc = jnp.dot(a_ref[...], b_ref[...], preferred_element_type=F32)
    if slab:
        for s in range(o_ref.shape[0]):
            o_ref[s] = acc[:, s * LANES:(s + 1) * LANES].astype(o_ref.dtype)
    else:
        o_ref[...] = acc.astype(o_ref.dtype)


def _matmul(a, b, tm, tn, out_dtype=F32, slab=False, ncols=None):
    m, k = a.shape
    n = b.shape[1] if ncols is None else ncols
    if slab:
        out_shape = jax.ShapeDtypeStruct((n // LANES, m, LANES), out_dtype)
        out_spec = pl.BlockSpec((tn // LANES, tm, LANES), lambda i, j: (j, i, 0))
    else:
        out_shape = jax.ShapeDtypeStruct((m, n), out_dtype)
        out_spec = pl.BlockSpec((tm, tn), lambda i, j: (i, j))
    return pl.pallas_call(
        functools.partial(_mm_kernel, slab=slab),
        grid=(m // tm, n // tn),
        in_specs=[pl.BlockSpec((tm, k), lambda i, j: (i, 0)),
                  pl.BlockSpec((k, tn), lambda i, j: (0, j))],
        out_specs=out_spec,
        out_shape=out_shape,
        compiler_params=_cparams(("arbitrary", "arbitrary")),
    )(a, b)


def _conv3_kernel(x_ref, w_ref, b_ref, o_ref, *, silu):
    x = x_ref[0].astype(F32)
    l = x.shape[0]
    row = lax.broadcasted_iota(jnp.int32, x.shape, 0)
    prev = jnp.where(row == 0, 0.0, pltpu.roll(x, 1, 0))
    nxt = jnp.where(row == l - 1, 0.0, pltpu.roll(x, l - 1, 0))
    w = w_ref[0]
    y = prev * w[0:1] + x * w[1:2] + nxt * w[2:3] + b_ref[0]
    if silu:
        y = _silu(y)
    o_ref[0] = y.astype(o_ref.dtype)


def _conv3(p_slab, slab0, nslab, w, bias, seq, silu):
    rows = p_slab.shape[1]
    nb = rows // seq
    return pl.pallas_call(
        functools.partial(_conv3_kernel, silu=silu),
        grid=(nslab, nb),
        in_specs=[pl.BlockSpec((1, seq, LANES), lambda s, b: (slab0 + s, b, 0)),
                  pl.BlockSpec((1, 3, LANES), lambda s, b: (s, 0, 0)),
                  pl.BlockSpec((1, 1, LANES), lambda s, b: (s, 0, 0))],
        out_specs=pl.BlockSpec((1, seq, LANES), lambda s, b: (s, b, 0)),
        out_shape=jax.ShapeDtypeStruct((nslab, rows, LANES), BF16),
        compiler_params=_cparams(("arbitrary", "arbitrary")),
    )(p_slab, w, bias)


def _hyconv_kernel(x_ref, w_ref, b_ref, o_ref, *, seq, nb):
    nj = seq // FFT_N1

    def gather(n1):
        return jnp.concatenate(
            [x_ref[0, pl.ds(n1 + b * seq, nj, stride=FFT_N1), :] for b in range(nb)], axis=0)

    w = w_ref[0]
    w0, w1, w2, bias = w[0:1], w[1:2], w[2:3], b_ref[0]
    jrow = lax.broadcasted_iota(jnp.int32, (nb * nj, LANES), 0) % nj

    def shift_down(g):
        return jnp.where(jrow == 0, 0.0, pltpu.roll(g, 1, 0))

    def shift_up(g):
        return jnp.where(jrow == nj - 1, 0.0, pltpu.roll(g, nb * nj - 1, 0))

    g_first = gather(0)
    g_second = gather(1)
    g_last = gather(FFT_N1 - 1)
    dt = o_ref.dtype
    o_ref[0, 0] = (shift_down(g_last) * w0 + g_first * w1 + g_second * w2 + bias).astype(dt)

    def body(n1, carry):
        g_prev, g_cur = carry
        g_next = gather(n1 + 1)
        o_ref[0, n1] = (g_prev * w0 + g_cur * w1 + g_next * w2 + bias).astype(dt)
        return g_cur, g_next

    g_prev, g_cur = lax.fori_loop(1, FFT_N1 - 1, body, (g_first, g_second), unroll=6)
    o_ref[0, FFT_N1 - 1] = (g_prev * w0 + g_cur * w1 + shift_up(g_first) * w2 + bias).astype(dt)


def _hyconv(p_slab, nslab, w, bias, seq):
    rows = p_slab.shape[1]
    nb = rows // seq
    nj = seq // FFT_N1
    return pl.pallas_call(
        functools.partial(_hyconv_kernel, seq=seq, nb=nb),
        grid=(nslab,),
        in_specs=[pl.BlockSpec((1, rows, LANES), lambda s: (s, 0, 0)),
                  pl.BlockSpec((1, 3, LANES), lambda s: (s, 0, 0)),
                  pl.BlockSpec((1, 1, LANES), lambda s: (s, 0, 0))],
        out_specs=pl.BlockSpec((1, FFT_N1, nb * nj, LANES), lambda s: (s, 0, 0, 0)),
        out_shape=jax.ShapeDtypeStruct((nslab, FFT_N1, nb * nj, LANES), BF16),
        compiler_params=_cparams(("arbitrary",)),
    )(p_slab, w, bias)


@functools.lru_cache(maxsize=None)
def _dft_tables(seq):
    nj = seq // FFT_N1
    n2 = nj + nj // 2
    n = FFT_N1 * n2
    out0 = nj // 2
    n1 = np.arange(FFT_N1, dtype=np.float64)
    k2 = np.arange(n2, dtype=np.float64)
    jn = np.arange(nj, dtype=np.float64)
    ang = -2.0 * np.pi * (k2[None, :, None] * jn[None, None, :] / n2 + n1[:, None, None] * k2[None, :, None] / n)
    cr, ci = np.cos(ang), np.sin(ang)
    fa = np.concatenate([np.concatenate([cr, -ci], axis=2), np.concatenate([ci, cr], axis=2)], axis=1)
    k1 = np.arange(FFT_N1, dtype=np.float64)
    angb = -2.0 * np.pi * np.outer(k1, n1) / FFT_N1
    dr, di = np.cos(angb), np.sin(angb)
    fb = np.block([[dr, -di], [di, dr]])
    fbi = np.block([[dr, di], [-di, dr]])
    angi = 2.0 * np.pi * (n1[:, None, None] * k2[None, None, :] / n + (out0 + jn)[None, :, None] * k2[None, None, :] / n2)
    er, ei = np.cos(angi) / n, np.sin(angi) / n
    fai = np.concatenate([np.concatenate([er, -ei], axis=2), np.concatenate([ei, er], axis=2)], axis=1)
    return (fa.astype(np.float32), fb.astype(np.float32), fbi.astype(np.float32), fai.astype(np.float32), n2)


@functools.lru_cache(maxsize=None)
def _filter_consts(seq):
    t = np.linspace(0.0, 1.0, seq, dtype=np.float32)[:, None].astype(np.float64)
    w = 2.0 * np.pi * np.arange(seq, dtype=np.float32)[:, None].astype(np.float64) / seq
    bands = np.linspace(1e-4, HY_BANDS - 1, HY_BANDS, dtype=np.float32)[None, :].astype(np.float64)
    feats = np.concatenate([t, np.cos(bands * w), -np.sin(bands * w)], axis=-1)
    feats_p = np.zeros((seq, LANES), np.float32)
    feats_p[:, :HY_EMB] = feats
    dist = (np.abs(np.arange(seq) - seq // 2).astype(np.float32) / np.float32(seq / 2.0)).reshape(seq, 1)
    deltas = np.abs(np.linspace(HY_MIN_DECAY, HY_MAX_DECAY, HY_WIDTH, dtype=np.float32))
    return feats_p, dist.astype(np.float32), deltas.astype(np.float32)


def _dot_f32(a, b):
    def split(x):
        hi = x.astype(BF16)
        r1 = x - hi.astype(F32)
        mid = r1.astype(BF16)
        lo = (r1 - mid.astype(F32)).astype(BF16)
        return hi, mid, lo
    a0, a1, a2 = split(a)
    b0, b1, b2 = split(b)
    d = functools.partial(jnp.dot, preferred_element_type=F32)
    return (d(a0, b0) + (d(a0, b1) + d(a1, b0)) + (d(a0, b2) + d(a1, b1) + d(a2, b0)))


def _filter_kernel(feats_ref, w1_ref, b1_ref, f1_ref, w2_ref, b2_ref, f2_ref, w3_ref, dist_ref, delta_ref,
                   fa_ref, fb_ref, o_ref, hid_ref, h_ref, sar_ref, sai_ref, *, seq, n2, pitch):
    nj = seq // FFT_N1

    @pl.when((pl.program_id(0) == 0) & (pl.program_id(1) == 0))
    def _():
        h1 = jnp.sin(f1_ref[...] * (_dot_f32(feats_ref[...], w1_ref[...]) + b1_ref[...]))
        hid_ref[...] = jnp.sin(f2_ref[...] * (_dot_f32(h1, w2_ref[...]) + b2_ref[...])).astype(BF16)

    ns = h_ref.shape[0]
    lanes = lambda f: jnp.concatenate([f(s) for s in range(ns)], axis=1)
    w3 = lanes(lambda s: w3_ref[0, s]).astype(BF16)
    h = jnp.dot(hid_ref[...], w3, preferred_element_type=F32)
    h = h * jnp.exp(-dist_ref[...] * lanes(lambda s: delta_ref[s]))
    h = h / (jnp.sum(jnp.abs(h), axis=0, keepdims=True) + 1e-6)
    for s in range(ns):
        h_ref[s] = h[:, s * LANES:(s + 1) * LANES]

    def stage_a(n1, c):
        g = lanes(lambda s: h_ref[s, pl.ds(n1, nj, stride=FFT_N1), :]).astype(BF16)
        y = jnp.dot(fa_ref[n1], g, preferred_element_type=F32)
        base = pl.multiple_of(n1 * pitch, 8)
        for s in range(ns):
            sar_ref[s, pl.ds(base, n2), :] = y[:n2, s * LANES:(s + 1) * LANES]
            sai_ref[s, pl.ds(base, n2), :] = y[n2:, s * LANES:(s + 1) * LANES]
        return c

    lax.fori_loop(0, FFT_N1, stage_a, 0, unroll=8)

    def stage_b(k2, c):
        yr = lanes(lambda s: sar_ref[s, pl.ds(k2, FFT_N1, stride=pitch), :])
        yi = lanes(lambda s: sai_ref[s, pl.ds(k2, FFT_N1, stride=pitch), :])
        op = jnp.concatenate([yr, yi], axis=0).astype(BF16)
        res = jnp.dot(fb_ref[...], op, preferred_element_type=F32).astype(o_ref.dtype)
        for s in range(ns):
            o_ref[0, s, k2] = res[:, s * LANES:(s + 1) * LANES]
        return c

    lax.fori_loop(0, n2, stage_b, 0, unroll=4)


def _hyena_filter_spectra(p, seq):
    fa, fb, _, _, n2 = _dft_tables(seq)
    nj = seq // FFT_N1
    pitch = n2 + SA_PITCH_PAD
    feats, dist, deltas = _filter_consts(seq)
    nsl = HY_WIDTH // LANES

    def padk(w, rows):
        return jnp.zeros((rows, w.shape[1]), F32).at[:w.shape[0]].set(w)

    def padn(v):
        return jnp.zeros((1, LANES), F32).at[0, :v.shape[0]].set(v)

    w1 = jnp.zeros((LANES, LANES), F32).at[:HY_EMB, :HY_FFN].set(p['hy_ffn_w1'])
    w2 = jnp.zeros((LANES, LANES), F32).at[:HY_FFN, :HY_FFN].set(p['hy_ffn_w2'])
    w3 = padk(p['hy_ffn_w3'], LANES).reshape(LANES, 2, nsl, LANES).transpose(1, 2, 0, 3)
    fa_real = jnp.asarray(fa[:, :, :nj]).astype(BF16)
    const = lambda shape: pl.BlockSpec(shape, lambda o, s: (0,) * len(shape))
    ns = 2
    return pl.pallas_call(
        functools.partial(_filter_kernel, seq=seq, n2=n2, pitch=pitch),
        grid=(2, nsl // ns),
        in_specs=[const((seq, LANES)), const((LANES, LANES)), const((1, LANES)), const((1, LANES)),
                  const((LANES, LANES)), const((1, LANES)), const((1, LANES)),
                  pl.BlockSpec((1, ns, LANES, LANES), lambda o, s: (o, s, 0, 0)),
                  const((seq, 1)),
                  pl.BlockSpec((ns, 1, LANES), lambda o, s: (s, 0, 0)),
                  const((FFT_N1, 2 * n2, nj)), const((2 * FFT_N1, 2 * FFT_N1))],
        out_specs=pl.BlockSpec((1, ns, n2, 2 * FFT_N1, LANES), lambda o, s: (o, s, 0, 0, 0)),
        out_shape=jax.ShapeDtypeStruct((2, nsl, n2, 2 * FFT_N1, LANES), BF16),
        scratch_shapes=[pltpu.VMEM((seq, LANES), BF16), pltpu.VMEM((ns, seq, LANES), F32),
                        pltpu.VMEM((ns, FFT_N1 * pitch, LANES), F32),
                        pltpu.VMEM((ns, FFT_N1 * pitch, LANES), F32)],
        compiler_params=_cparams(("arbitrary", "arbitrary")),
    )(jnp.asarray(feats), w1, padn(p['hy_ffn_b1']), padn(p['hy_freq1']), w2, padn(p['hy_ffn_b2']),
      padn(p['hy_freq2']), w3, jnp.asarray(dist), jnp.asarray(deltas).reshape(nsl, 1, LANES),
      fa_real, jnp.asarray(fb).astype(BF16))


def _longconv_kernel(x_ref, g_ref, skip_ref, h_ref, fa_ref, fb_ref, fbi_ref, fai_ref, o_ref,
                     sar_ref, sai_ref, sbr_ref, sbi_ref, *, seq, n2, pitch, natural_out):
    nj = seq // FFT_N1

    def stage_a(n1, c):
        y = jnp.dot(fa_ref[n1], x_ref[0, n1].astype(BF16), preferred_element_type=F32)
        base = pl.multiple_of(n1 * pitch, 8)
        sar_ref[pl.ds(base, n2), :] = y[:n2]
        sai_ref[pl.ds(base, n2), :] = y[n2:]
        return c

    lax.fori_loop(0, FFT_N1, stage_a, 0, unroll=8)

    def stage_b(kp, c):
        ks = (2 * kp, 2 * kp + 1)
        yr = jnp.concatenate([sar_ref[pl.ds(k, FFT_N1, stride=pitch), :] for k in ks], axis=1)
        yi = jnp.concatenate([sai_ref[pl.ds(k, FFT_N1, stride=pitch), :] for k in ks], axis=1)
        xf = jnp.dot(fb_ref[...], jnp.concatenate([yr, yi], axis=0).astype(BF16), preferred_element_type=F32)
        xr, xi = xf[:FFT_N1], xf[FFT_N1:]
        hf = jnp.concatenate([h_ref[0, k] for k in ks], axis=1).astype(F32)
        hr, hi = hf[:FFT_N1], hf[FFT_N1:]
        z = jnp.concatenate([xr * hr - xi * hi, xr * hi + xi * hr], axis=0).astype(BF16)
        w = jnp.dot(fbi_ref[...], z, preferred_element_type=F32)
        for q, k in enumerate(ks):
            base = pl.multiple_of(k * SB_PITCH, 8)
            sbr_ref[pl.ds(base, FFT_N1), :] = w[:FFT_N1, q * LANES:(q + 1) * LANES]
            sbi_ref[pl.ds(base, FFT_N1), :] = w[FFT_N1:, q * LANES:(q + 1) * LANES]
        return c

    lax.fori_loop(0, n2 // 2, stage_b, 0, unroll=2)

    skip = skip_ref[0]

    def stage_c(n1, c):
        wr = sbr_ref[pl.ds(n1, n2, stride=SB_PITCH), :]
        wi = sbi_ref[pl.ds(n1, n2, stride=SB_PITCH), :]
        y = jnp.dot(fai_ref[n1], jnp.concatenate([wr, wi], axis=0).astype(BF16), preferred_element_type=F32)
        res = g_ref[0, n1].astype(F32) * (y + x_ref[0, n1].astype(F32) * skip)
        if natural_out:
            for b in range(res.shape[0] // nj):
                o_ref[0, pl.ds(n1 + b * seq, nj, stride=FFT_N1), :] = res[b * nj:(b + 1) * nj]
        else:
            o_ref[0, n1] = res.astype(o_ref.dtype)
        return c

    lax.fori_loop(0, FFT_N1, stage_c, 0, unroll=8)


def _longconv(xg, gg, slab_x, slab_g, skip, hspec, order, seq, natural_out):
    fa, fb, fbi, fai, n2 = _dft_tables(seq)
    pitch = n2 + SA_PITCH_PAD
    nsl = HY_WIDTH // LANES
    rows = xg.shape[2]
    nb = rows // (seq // FFT_N1)
    const = lambda shape: pl.BlockSpec(shape, lambda s: (0,) * len(shape))
    if natural_out:
        out_shape = jax.ShapeDtypeStruct((nsl, nb * seq, LANES), F32)
        out_spec = pl.BlockSpec((1, nb * seq, LANES), lambda s: (s, 0, 0))
    else:
        out_shape = jax.ShapeDtypeStruct((nsl, FFT_N1, rows, LANES), BF16)
        out_spec = pl.BlockSpec((1, FFT_N1, rows, LANES), lambda s: (s, 0, 0, 0))
    return pl.pallas_call(
        functools.partial(_longconv_kernel, seq=seq, n2=n2, pitch=pitch, natural_out=natural_out),
        grid=(nsl,),
        in_specs=[pl.BlockSpec((1, FFT_N1, rows, LANES), lambda s: (slab_x + s, 0, 0, 0)),
                  pl.BlockSpec((1, FFT_N1, rows, LANES), lambda s: (slab_g + s, 0, 0, 0)),
                  pl.BlockSpec((1, 1, LANES), lambda s: (s, 0, 0)),
                  pl.BlockSpec((None, 1, n2, 2 * FFT_N1, LANES), lambda s: (order, s, 0, 0, 0)),
                  const(fa.shape), const(fb.shape), const(fbi.shape), const(fai.shape)],
        out_specs=out_spec,
        out_shape=out_shape,
        scratch_shapes=[pltpu.VMEM((FFT_N1 * pitch, LANES), F32), pltpu.VMEM((FFT_N1 * pitch, LANES), F32),
                        pltpu.VMEM((n2 * SB_PITCH, LANES), F32), pltpu.VMEM((n2 * SB_PITCH, LANES), F32)],
        compiler_params=_cparams(("arbitrary",)),
    )(xg, gg, skip.reshape(nsl, 1, LANES), hspec,
      jnp.asarray(fa).astype(BF16), jnp.asarray(fb).astype(BF16),
      jnp.asarray(fbi).astype(BF16), jnp.asarray(fai).astype(BF16))


def _dot_exact_lhs(lhs_bf16, x):
    hi = x.astype(BF16)
    r1 = x - hi.astype(F32)
    mid = r1.astype(BF16)
    lo = (r1 - mid.astype(F32)).astype(BF16)
    d = functools.partial(jnp.dot, preferred_element_type=F32)
    return d(lhs_bf16, hi) + (d(lhs_bf16, mid) + d(lhs_bf16, lo))


def _ssd_kernel(x_ref, b_ref, c_ref, dt_ref, bias_ref, alog_ref, e_ref, init_ref, y_ref, fin_ref, st_ref,
                *, rev, nc):
    ci = pl.program_id(1)
    gw = SSD_HPG * SSD_HEAD_DIM
    dot = functools.partial(jnp.dot, preferred_element_type=F32)

    @pl.when(ci == 0)
    def _():
        st_ref[...] = init_ref[0]

    off = SSD_HEADS if rev else 0
    dt_all = _softplus(dt_ref[...] + bias_ref[...])
    a_all = dt_all * (-jnp.exp(alog_ref[...]))
    r_i = lax.broadcasted_iota(jnp.int32, (SSD_CHUNK, SSD_CHUNK), 0)
    c_i = lax.broadcasted_iota(jnp.int32, (SSD_CHUNK, SSD_CHUNK), 1)
    valid = (c_i >= r_i) if rev else (c_i <= r_i)
    tri = jnp.where(valid, 1.0, 0.0).astype(BF16)
    acum = _dot_exact_lhs(tri, a_all)
    src_t = (acum - jnp.log(dt_all)).T
    tot = acum[0:1, :] if rev else acum[SSD_CHUNK - 1:SSD_CHUNK, :]
    dtw_b = (dt_all * jnp.exp(tot - acum)).astype(BF16)
    cd8 = jnp.broadcast_to(jnp.exp(tot), (8, LANES))
    cd_hi = cd8.astype(BF16)
    cd_r = cd8 - cd_hi.astype(F32)
    cd_mid = cd_r.astype(BF16)
    cd_lo = (cd_r - cd_mid.astype(F32)).astype(BF16)
    lo_half = c_i < SSD_HEAD_DIM
    zero_b = jnp.zeros((), BF16)

    for g in range(SSD_GROUPS):
        bt = b_ref[g].astype(F32).T.astype(BF16)
        cg = c_ref[g].astype(BF16)
        cb = dot(cg, bt)
        for sl in range(gw // LANES):
            col = slice(sl * LANES, (sl + 1) * LANES)
            e_s = e_ref[:, g * gw + sl * LANES:g * gw + (sl + 1) * LANES]
            xb = x_ref[(gw // LANES) * g + sl]
            ms, ea = [], []
            for hh in range(2):
                h = off + SSD_HPG * g + 2 * sl + hh
                a_col = jnp.broadcast_to(acum[:, h:h + 1], (SSD_CHUNK, SSD_CHUNK))
                seg = jnp.where(valid, a_col - src_t[h:h + 1, :], -1e30)
                ms.append((cb * jnp.exp(seg)).astype(BF16))
                ea.append(jnp.exp(a_col))
            rhs = jnp.concatenate([jnp.where(lo_half, xb, zero_b), jnp.where(lo_half, zero_b, xb)], axis=0)
            yd = dot(jnp.concatenate(ms, axis=1), rhs)
            sgs = st_ref[g, :, col]
            yoff = dot(cg, sgs.astype(BF16)) * jnp.where(lo_half, ea[0], ea[1])
            y_ref[g, :, col] = (yd + yoff).astype(y_ref.dtype)
            wm = (xb.astype(F32) * dot(dtw_b, e_s)).astype(BF16)
            cd = (dot(cd_hi, e_s) + (dot(cd_mid, e_s) + dot(cd_lo, e_s)))[0:1]
            st_ref[g, :, col] = sgs * cd + dot(bt, wm)

    @pl.when(ci == nc - 1)
    def _():
        fin_ref[0] = st_ref[...]


def _ssd_scan(xbc, slab_x, slab_b, slab_c, dt, dt_bias, a_log, init, seq, rev):
    rows = xbc.shape[1]
    nb = rows // seq
    nc = seq // SSD_CHUNK
    gw = SSD_HPG * SSD_HEAD_DIM
    xs = SSD_INNER // LANES
    cpos = (lambda b, c: b * nc + (nc - 1 - c)) if rev else (lambda b, c: b * nc + c)
    expand = np.zeros((2 * SSD_HEADS, SSD_INNER), np.float32)
    for h in range(SSD_HEADS):
        expand[(SSD_HEADS if rev else 0) + h, h * SSD_HEAD_DIM:(h + 1) * SSD_HEAD_DIM] = 1.0
    return pl.pallas_call(
        functools.partial(_ssd_kernel, rev=rev, nc=nc),
        grid=(nb, nc),
        in_specs=[pl.BlockSpec((xs, SSD_CHUNK, LANES), lambda b, c: (slab_x // xs, cpos(b, c), 0)),
                  pl.BlockSpec((SSD_GROUPS, SSD_CHUNK, LANES), lambda b, c: (slab_b // SSD_GROUPS, cpos(b, c), 0)),
                  pl.BlockSpec((SSD_GROUPS, SSD_CHUNK, LANES), lambda b, c: (slab_c // SSD_GROUPS, cpos(b, c), 0)),
                  pl.BlockSpec((SSD_CHUNK, LANES), lambda b, c: (cpos(b, c), 0)),
                  pl.BlockSpec((1, LANES), lambda b, c: (0, 0)),
                  pl.BlockSpec((1, LANES), lambda b, c: (0, 0)),
                  pl.BlockSpec((2 * SSD_HEADS, SSD_INNER), lambda b, c: (0, 0)),
                  pl.BlockSpec((1, SSD_GROUPS, SSD_STATE, gw), lambda b, c: (b, 0, 0, 0))],
        out_specs=[pl.BlockSpec((SSD_GROUPS, SSD_CHUNK, gw), lambda b, c: (0, cpos(b, c), 0)),
                   pl.BlockSpec((1, SSD_GROUPS, SSD_STATE, gw), lambda b, c: (b, 0, 0, 0))],
        out_shape=[jax.ShapeDtypeStruct((SSD_GROUPS, rows, gw), BF16),
                   jax.ShapeDtypeStruct((nb, SSD_GROUPS, SSD_STATE, gw), F32)],
        scratch_shapes=[pltpu.VMEM((SSD_GROUPS, SSD_STATE, gw), F32)],
        compiler_params=_cparams(("arbitrary", "arbitrary")),
    )(xbc, xbc, xbc, dt, dt_bias.reshape(1, LANES), a_log.reshape(1, LANES),
      jnp.asarray(expand).astype(BF16), init)


def _ssdnorm_kernel(yf_ref, yb_ref, x_ref, z_ref, d_ref, w_ref, o_ref):
    nsl = x_ref.shape[0]
    x = jnp.concatenate([x_ref[i] for i in range(nsl)], axis=1).astype(F32)
    z = jnp.concatenate([z_ref[i] for i in range(nsl)], axis=1).astype(F32)
    y = yf_ref[0].astype(F32) + yb_ref[0].astype(F32) + x * d_ref[0]
    gx = y * _silu(z)
    ms = jnp.mean(gx * gx, axis=-1, keepdims=True)
    o_ref[...] = (gx * lax.rsqrt(ms + 1e-5) * w_ref[0]).astype(o_ref.dtype)


def _ssd_norm(yf, yb, xbc, proj, slab_z, d_ch, w, tr):
    g, rows, gw = yf.shape
    nsl = gw // LANES
    return pl.pallas_call(
        _ssdnorm_kernel,
        grid=(g, rows // tr),
        in_specs=[pl.BlockSpec((1, tr, gw), lambda q, i: (q, i, 0)),
                  pl.BlockSpec((1, tr, gw), lambda q, i: (q, i, 0)),
                  pl.BlockSpec((nsl, tr, LANES), lambda q, i: (q, i, 0)),
                  pl.BlockSpec((nsl, tr, LANES), lambda q, i: (slab_z // nsl + q, i, 0)),
                  pl.BlockSpec((1, 1, gw), lambda q, i: (q, 0, 0)),
                  pl.BlockSpec((1, 1, gw), lambda q, i: (q, 0, 0))],
        out_specs=pl.BlockSpec((tr, gw), lambda q, i: (i, q)),
        out_shape=jax.ShapeDtypeStruct((rows, g * gw), BF16),
        compiler_params=_cparams(("arbitrary", "arbitrary")),
    )(yf, yb, xbc, proj, d_ch.reshape(g, 1, gw), w.reshape(g, 1, gw))


def _merge_kernel(yhy_ref, yss_ref, whh_ref, wss_ref, gh_ref, gs_ref, o_ref):
    a1 = jnp.concatenate([yhy_ref[i] for i in range(yhy_ref.shape[0])], axis=1).astype(BF16)
    p1 = jnp.dot(a1, whh_ref[...], preferred_element_type=F32)
    p2 = jnp.dot(yss_ref[...], wss_ref[...], preferred_element_type=F32)
    gh = jnp.concatenate([gh_ref[i] for i in range(gh_ref.shape[0])], axis=1).astype(F32)
    gs = jnp.concatenate([gs_ref[i] for i in range(gs_ref.shape[0])], axis=1).astype(F32)
    o_ref[...] = (_sigmoid(gh) * p1 + _sigmoid(gs) * p2).astype(o_ref.dtype)


def _branch_merge(yhy, yss, w_hy, w_ssd, proj, slab_gh, slab_gs, tm, tn):
    nsl_k = yhy.shape[0]
    rows = yhy.shape[1]
    n = w_hy.shape[1]
    ts = tn // LANES
    return pl.pallas_call(
        _merge_kernel,
        grid=(rows // tm, n // tn),
        in_specs=[pl.BlockSpec((nsl_k, tm, LANES), lambda i, j: (0, i, 0)),
                  pl.BlockSpec((tm, yss.shape[1]), lambda i, j: (i, 0)),
                  pl.BlockSpec((w_hy.shape[0], tn), lambda i, j: (0, j)),
                  pl.BlockSpec((w_ssd.shape[0], tn), lambda i, j: (0, j)),
                  pl.BlockSpec((ts, tm, LANES), lambda i, j: (slab_gh // ts + j, i, 0)),
                  pl.BlockSpec((ts, tm, LANES), lambda i, j: (slab_gs // ts + j, i, 0))],
        out_specs=pl.BlockSpec((tm, tn), lambda i, j: (i, j)),
        out_shape=jax.ShapeDtypeStruct((rows, n), BF16),
        compiler_params=_cparams(("arbitrary", "arbitrary")),
    )(yhy, yss, w_hy, w_ssd, proj, proj)


def _outproj_ln_kernel(a_ref, w_ref, x_ref, gt_ref, g_ref, b_ref, sc_ref, sh_ref, h_ref, m_ref, mt_ref):
    y = jnp.dot(a_ref[0], w_ref[...], preferred_element_type=F32)
    h = _normalize_rows(DEEPNORM_ALPHA * x_ref[0] + gt_ref[0] * y, 1e-6) * g_ref[...] + b_ref[...]
    h_ref[0] = h
    m = _normalize_rows(h, 1e-6) * (1.0 + sc_ref[0]) + sh_ref[0]
    m_ref[0] = m.astype(m_ref.dtype)
    mt_ref[...] = m.T.astype(mt_ref.dtype)


def _outproj_ln(merged, w_out, x, gate, ln_g, ln_b, scale2, shift2, tm):
    b, l, d = x.shape
    nt = l // tm
    vec = pl.BlockSpec((1, 1, d), lambda i, j: (i, 0, 0))
    cvec = pl.BlockSpec((1, d), lambda i, j: (0, 0))
    tile = pl.BlockSpec((1, tm, d), lambda i, j: (i, j, 0))
    return pl.pallas_call(
        _outproj_ln_kernel,
        grid=(b, nt),
        in_specs=[tile, pl.BlockSpec((d, d), lambda i, j: (0, 0)), tile, vec, cvec, cvec, vec, vec],
        out_specs=[tile, tile, pl.BlockSpec((d, tm), lambda i, j: (0, i * nt + j))],
        out_shape=[jax.ShapeDtypeStruct((b, l, d), F32), jax.ShapeDtypeStruct((b, l, d), BF16),
                   jax.ShapeDtypeStruct((d, b * l), BF16)],
        compiler_params=_cparams(("arbitrary", "arbitrary")),
    )(merged, w_out, x, gate, ln_g.reshape(1, d), ln_b.reshape(1, d), scale2, shift2)


def _resid_ln_kernel(x_ref, y_ref, gt_ref, g_ref, b_ref, o_ref):
    o_ref[0] = _normalize_rows(DEEPNORM_ALPHA * x_ref[0] + gt_ref[0] * y_ref[0], 1e-6) * g_ref[...] + b_ref[...]


def _resid_ln(x, y, gate, ln_g, ln_b, tm):
    b, l, d = x.shape
    tile = pl.BlockSpec((1, tm, d), lambda i, j: (i, j, 0))
    return pl.pallas_call(
        _resid_ln_kernel,
        grid=(b, l // tm),
        in_specs=[tile, tile, pl.BlockSpec((1, 1, d), lambda i, j: (i, 0, 0)),
                  pl.BlockSpec((1, d), lambda i, j: (0, 0)), pl.BlockSpec((1, d), lambda i, j: (0, 0))],
        out_specs=tile,
        out_shape=jax.ShapeDtypeStruct((b, l, d), F32),
        compiler_params=_cparams(("arbitrary", "arbitrary")),
    )(x, y, gate, ln_g.reshape(1, d), ln_b.reshape(1, d))


def _peer_scores_kernel(a_ref, wq_ref, k_ref, s_ref):
    q = jnp.dot(a_ref[...], wq_ref[...], preferred_element_type=F32)
    hd = PEER_DKEY // 2
    nt = (((1,), (1,)), ((), ()))
    for h in range(PEER_HEADS):
        qn = _normalize_rows(q[:, h * PEER_DKEY:(h + 1) * PEER_DKEY], 1e-6).astype(BF16)
        s_ref[h, 0] = lax.dot_general(k_ref[h, 0], qn[:, :hd], nt, preferred_element_type=F32)
        s_ref[h, 1] = lax.dot_general(k_ref[h, 1], qn[:, hd:], nt, preferred_element_type=F32)


def _peer_scores(m2, wq, subkeys, tm):
    t, d = m2.shape
    return pl.pallas_call(
        _peer_scores_kernel,
        grid=(t // tm,),
        in_specs=[pl.BlockSpec((tm, d), lambda i: (i, 0)),
                  pl.BlockSpec(wq.shape, lambda i: (0, 0)),
                  pl.BlockSpec(subkeys.shape, lambda i: (0, 0, 0, 0))],
        out_specs=pl.BlockSpec((PEER_HEADS, 2, PEER_NKEYS, tm), lambda i: (0, 0, 0, i)),
        out_shape=jax.ShapeDtypeStruct((PEER_HEADS, 2, PEER_NKEYS, t), F32),
        compiler_params=_cparams(("arbitrary",)),
    )(m2, wq, subkeys)


def _top_values(x, k):
    out = []
    for _ in range(k):
        m = jnp.max(x, axis=0, keepdims=True)
        out.append(m)
        x = jnp.where(x == m, -jnp.inf, x)
    return out


def _top_values_ranked(x, k):
    out = []
    big = 2.0 ** 100
    for a in range(k):
        m = jnp.max(x, axis=0, keepdims=True)
        out.append(m)
        x = jnp.where(x == m, -big * (1.0 + a / 64.0), x)
    rank = jnp.where(x <= -big, (x * (-1.0 / big) - 1.0) * 64.0, float(k))
    return out, rank


def _peer_stats_kernel(s_ref, rk_ref, lim_ref, p1_ref, p2_ref):
    k = PEER_TOPK
    for h in range(PEER_HEADS):
        s1 = s_ref[h, 0]
        s2 = s_ref[h, 1]
        v1, r1 = _top_values_ranked(s1, k)
        v2, rk = _top_values_ranked(s2, k)
        v2m = jnp.concatenate(v2, axis=0)
        grid = [v1[a] + v2m[:k // (a + 1)] for a in range(k)]
        n_cand = sum(g.shape[0] for g in grid)
        pad = jnp.full((-n_cand % 8, s1.shape[1]), -jnp.inf, F32)
        best = _top_values(jnp.concatenate(grid + [pad], axis=0), k)
        z = best[0] * 0.0
        for c in best:
            z = z + jnp.exp(c - best[0])
        lim = jnp.zeros_like(s1)
        for a in range(k):
            n_sel = jnp.sum(jnp.where(grid[a] >= best[k - 1], 1.0, 0.0), axis=0, keepdims=True)
            lim = jnp.where(r1 == float(a), n_sel, lim)
        lim_ref[h] = lim
        rk_ref[h] = rk.astype(rk_ref.dtype)
        p1_ref[h] = jnp.exp(s1 - v1[0])
        p2_ref[h] = (0.5 * jnp.exp(s2 - v2[0]) / z).astype(p2_ref.dtype)


def _peer_stats(scores, tm):
    t = scores.shape[-1]
    hk = pl.BlockSpec((PEER_HEADS, PEER_NKEYS, tm), lambda i: (0, 0, i))
    f32 = jax.ShapeDtypeStruct((PEER_HEADS, PEER_NKEYS, t), F32)
    b16 = jax.ShapeDtypeStruct((PEER_HEADS, PEER_NKEYS, t), BF16)
    return pl.pallas_call(
        _peer_stats_kernel,
        grid=(t // tm,),
        in_specs=[pl.BlockSpec((PEER_HEADS, 2, PEER_NKEYS, tm), lambda i: (0, 0, 0, i))],
        out_specs=[hk, hk, hk, hk],
        out_shape=[b16, f32, f32, b16],
        compiler_params=_cparams(("arbitrary",)),
    )(scores)


def _peer_dense_kernel(ht_ref, u_ref, vt_ref, rk_in, lim_ref, p1_ref, p2_in, o_ref, acc_ref, g_ref, rk_ref, p2_ref,
                       *, ne1):
    j = pl.program_id(1)

    @pl.when(j == 0)
    def _():
        acc_ref[...] = jnp.zeros_like(acc_ref)
        rk_ref[...] = rk_in[...]
        p2_ref[...] = p2_in[...]

    zero = jnp.zeros((), BF16)
    for r in range(ne1):
        e1 = j * ne1 + r
        rows = slice(r * PEER_NKEYS, (r + 1) * PEER_NKEYS)
        lim_rows = [lim_ref[h, pl.ds(e1, 1), :].astype(BF16) for h in range(PEER_HEADS)]
        p1_rows = [p1_ref[h, pl.ds(e1, 1), :].astype(BF16) for h in range(PEER_HEADS)]
        for tt in range(g_ref.shape[1] // LANES):
            ls = slice(tt * LANES, (tt + 1) * LANES)
            gacc = None
            for h in range(PEER_HEADS):
                hs = slice(h * PEER_NKEYS, (h + 1) * PEER_NKEYS)
                w = jnp.where(rk_ref[hs, ls] < lim_rows[h][:, ls], p2_ref[hs, ls] * p1_rows[h][:, ls], zero)
                gacc = w if gacc is None else gacc + w
            g_ref[rows, ls] = gacc
    act = jnp.dot(u_ref[...], ht_ref[...], preferred_element_type=F32)
    gelu2 = act * (1.0 + jnp.tanh(math.sqrt(2.0 / math.pi) * (act + 0.044715 * (act * act * act))))
    pt = gelu2.astype(BF16) * g_ref[...]
    acc_ref[...] += jnp.dot(vt_ref[...], pt, preferred_element_type=F32)

    @pl.when(j == pl.num_programs(1) - 1)
    def _():
        o_ref[...] = acc_ref[...].T


def _peer_dense(m2t, u_b, vt_b, rk, lim, p1, p2, tm, te):
    d, t = m2t.shape
    ne = u_b.shape[0]
    hk = pl.BlockSpec((PEER_HEADS, PEER_NKEYS, tm), lambda i, j: (0, 0, i))
    flat = pl.BlockSpec((PEER_HEADS * PEER_NKEYS, tm), lambda i, j: (0, i))
    return pl.pallas_call(
        functools.partial(_peer_dense_kernel, ne1=te // PEER_NKEYS),
        grid=(t // tm, ne // te),
        in_specs=[pl.BlockSpec((d, tm), lambda i, j: (0, i)),
                  pl.BlockSpec((te, d), lambda i, j: (j, 0)),
                  pl.BlockSpec((d, te), lambda i, j: (0, j)),
                  flat, hk, hk, flat],
        out_specs=pl.BlockSpec((tm, d), lambda i, j: (i, 0)),
        out_shape=jax.ShapeDtypeStruct((t, d), F32),
        scratch_shapes=[pltpu.VMEM((d, tm), F32), pltpu.VMEM((te, tm), BF16),
                        pltpu.VMEM((PEER_HEADS * PEER_NKEYS, tm), BF16),
                        pltpu.VMEM((PEER_HEADS * PEER_NKEYS, tm), BF16)],
        compiler_params=_cparams(("arbitrary", "arbitrary")),
    )(m2t, u_b, vt_b, rk.reshape(PEER_HEADS * PEER_NKEYS, t), lim, p1, p2.reshape(PEER_HEADS * PEER_NKEYS, t))


SL_X = 0
SL_B = SL_X + SSD_INNER // LANES
SL_C = SL_B + SSD_GROUPS * SSD_STATE // LANES
SL_Z = SL_C + SSD_GROUPS * SSD_STATE // LANES
SL_GH = SL_Z + SSD_INNER // LANES
SL_GS = SL_GH + D_MODEL // LANES


def kernel(x, c, ctx, c_ctx, ada_w, ada_b, w_in, hy_conv_w, hy_conv_b, hy_ffn_w1, hy_ffn_b1, hy_freq1,
           hy_ffn_w2, hy_ffn_b2, hy_freq2, hy_ffn_w3, hy_skip, ssd_conv_w, ssd_conv_b, ssd_a_log,
           ssd_dt_bias, ssd_d, ssd_norm_w, w_branch_hy, w_branch_ssd, w_out, ln1_g, ln1_b, peer_wq,
           peer_subkeys, peer_u, peer_v, ln2_g, ln2_b):
    nb, seq, d = x.shape
    lc = ctx.shape[1]
    t = nb * seq
    p = dict(hy_ffn_w1=hy_ffn_w1[0], hy_ffn_b1=hy_ffn_b1[0], hy_freq1=hy_freq1[0], hy_ffn_w2=hy_ffn_w2[0],
             hy_ffn_b2=hy_ffn_b2[0], hy_freq2=hy_freq2[0], hy_ffn_w3=hy_ffn_w3[0])

    cc = jnp.zeros((8, d), F32).at[:nb].set(c).at[nb].set(c_ctx)
    mod = _ada(cc, ada_w[0], ada_b[0])
    sh1, sc1, gt1, sh2, sc2, gt2 = [mod[:nb, i * d:(i + 1) * d].reshape(nb, 1, d) for i in range(6)]
    csh1 = mod[nb:nb + 1, 0:d].reshape(1, 1, d)
    csc1 = mod[nb:nb + 1, d:2 * d].reshape(1, 1, d)

    w = w_in[0]
    w_hy = w[:, :OFF_X].astype(BF16)
    w_rest = jnp.concatenate([w[:, OFF_X:OFF_DT], w[:, OFF_Z:]], axis=1).astype(BF16)
    w_dt = w[:, OFF_DT:OFF_Z].astype(BF16)

    m_ctx = _modulate(ctx, csc1, csh1, 256).reshape(nb * lc, d)
    pc = _matmul(m_ctx, w_rest, 512, 1024, slab=True, ncols=OFF_C - OFF_X)
    dtc = _matmul(m_ctx, w_dt, 512, LANES)
    nxb = (OFF_C - OFF_X) // LANES
    cw = ssd_conv_w[0].reshape(3, -1, LANES).transpose(1, 0, 2)
    cbias = ssd_conv_b[0].reshape(-1, 1, LANES)
    xbc_ctx = _conv3(pc, 0, nxb, cw[:nxb], cbias[:nxb], lc, True)
    gw = SSD_HPG * SSD_HEAD_DIM
    zero_state = jnp.zeros((nb, SSD_GROUPS, SSD_STATE, gw), F32)
    dt_bias = ssd_dt_bias[0]
    a_log = ssd_a_log[0]
    xs_sl = SSD_INNER // LANES
    _, s_f = _ssd_scan(xbc_ctx, 0, xs_sl, xs_sl, dtc, dt_bias, a_log, zero_state, lc, False)
    _, s_b = _ssd_scan(xbc_ctx, 0, xs_sl, xs_sl, dtc, dt_bias, a_log, zero_state, lc, True)

    m1 = _modulate(x, sc1, sh1, 256).reshape(t, d)
    proj_hy = _matmul(m1, w_hy, 1024, 1024, slab=True)
    proj = _matmul(m1, w_rest, 1024, 1024, out_dtype=BF16, slab=True)
    dtp = _matmul(m1, w_dt, 1024, LANES)

    hw = hy_conv_w[0].reshape(3, -1, LANES).transpose(1, 0, 2)
    hb = hy_conv_b[0].reshape(-1, 1, LANES)
    nh = HY_WIDTH // LANES
    ug = _hyconv(proj_hy, 3 * nh, hw, hb, seq)
    hspec = _hyena_filter_spectra(p, seq)
    zg = _longconv(ug, ug, 0, nh, hy_skip[0, 0], hspec, 0, seq, False)
    y_hy = _longconv(zg, ug, 0, 2 * nh, hy_skip[0, 1], hspec, 1, seq, True)

    xbc = _conv3(proj, SL_X, SL_Z - SL_X, cw, cbias, seq, True)
    y_f, _ = _ssd_scan(xbc, 0, xs_sl, xs_sl + SSD_GROUPS, dtp, dt_bias, a_log, s_f, seq, False)
    y_b, _ = _ssd_scan(xbc, 0, xs_sl, xs_sl + SSD_GROUPS, dtp, dt_bias, a_log, s_b, seq, True)
    d_ch = jnp.repeat(ssd_d[0, 0] + ssd_d[0, 1], SSD_HEAD_DIM)
    y_ss = _ssd_norm(y_f, y_b, xbc, proj, SL_Z, d_ch, ssd_norm_w[0], 512)

    merged = _branch_merge(y_hy, y_ss, w_branch_hy[0].astype(BF16), w_branch_ssd[0].astype(BF16),
                           proj, SL_GH, SL_GS, 512, 1024)
    h1, m2, m2t = _outproj_ln(merged.reshape(nb, seq, d), w_out[0].astype(BF16), x, gt1, ln1_g[0], ln1_b[0],
                              sc2, sh2, 512)

    m2f = m2.reshape(t, d)
    scores = _peer_scores(m2f, peer_wq[0].astype(BF16), peer_subkeys[0].astype(BF16), 256)
    rk, lim, p1, p2 = _peer_stats(scores, 256)
    ffn = _peer_dense(m2t, peer_u[0].astype(BF16), peer_v[0].T.astype(BF16), rk, lim, p1, p2, 512, 1024)
    return _resid_ln(h1, ffn.reshape(nb, seq, d), gt2, ln2_g[0], ln2_b[0], 256)
```

```python
import functools
import math

import numpy as np
import jax
import jax.numpy as jnp
from jax import lax
from jax.experimental import pallas as pl
from jax.experimental.pallas import tpu as pltpu

F32 = jnp.float32
BF16 = jnp.bfloat16

D_MODEL = 2048
DEPTH = 1
CTX_LEN = 256
HY_WIDTH = D_MODEL
HY_BANDS = 16
HY_EMB = 2 * HY_BANDS + 1
HY_FFN = 64
HY_MIN_DECAY = math.log(1e-2) / 1.5
HY_MAX_DECAY = math.log(1e-2) / 0.3
SSD_INNER = 2 * D_MODEL
SSD_HEAD_DIM = 64
SSD_HEADS = SSD_INNER // SSD_HEAD_DIM
SSD_STATE = 128
SSD_GROUPS = 8
SSD_HPG = SSD_HEADS // SSD_GROUPS
SSD_CHUNK = 128
OFF_X = 3 * HY_WIDTH
OFF_B = OFF_X + SSD_INNER
OFF_C = OFF_B + SSD_GROUPS * SSD_STATE
OFF_DT = OFF_C + SSD_GROUPS * SSD_STATE
OFF_Z = OFF_DT + 2 * SSD_HEADS
OFF_GATE = OFF_Z + SSD_INNER
N_COLS = OFF_GATE + 2 * D_MODEL
PEER_HEADS = 8
PEER_NKEYS = 128
PEER_EXPERTS = PEER_NKEYS * PEER_NKEYS
PEER_TOPK = 16
PEER_DKEY = 256
DEEPNORM_ALPHA = (2.0 * DEPTH) ** 0.25

LANES = 128
MM_ROWS = 512
VMEM_LIMIT = 56 * 1024 * 1024

FFT_N1 = 128
SA_PITCH_PAD = 8
SB_PITCH = FFT_N1 + 8


def _cparams(sem, vmem=VMEM_LIMIT):
    return pltpu.CompilerParams(dimension_semantics=sem, vmem_limit_bytes=vmem)


def _silu(x):
    return x * (1.0 / (1.0 + jnp.exp(-x)))


def _sigmoid(x):
    return 1.0 / (1.0 + jnp.exp(-x))


def _softplus(x):
    return jnp.maximum(x, 0.0) + jnp.log(1.0 + jnp.exp(-jnp.abs(x)))


def _gelu_tanh(x):
    return 0.5 * x * (1.0 + jnp.tanh(math.sqrt(2.0 / math.pi) * (x + 0.044715 * (x * x * x))))


def _normalize_rows(x, eps):
    mu = jnp.mean(x, axis=-1, keepdims=True)
    xc = x - mu
    var = jnp.mean(xc * xc, axis=-1, keepdims=True)
    return xc * lax.rsqrt(var + eps)


def _ada_kernel(c_ref, w_ref, b_ref, o_ref):
    a = _silu(c_ref[...]).astype(BF16)
    o_ref[...] = jnp.dot(a, w_ref[...].astype(BF16), preferred_element_type=F32) + b_ref[...]


def _ada(cc, ada_w, ada_b):
    rows, d = cc.shape
    n = ada_w.shape[1]
    tn = 1024
    return pl.pallas_call(
        _ada_kernel,
        grid=(n // tn,),
        in_specs=[pl.BlockSpec((rows, d), lambda j: (0, 0)),
                  pl.BlockSpec((d, tn), lambda j: (0, j)),
                  pl.BlockSpec((1, tn), lambda j: (0, j))],
        out_specs=pl.BlockSpec((rows, tn), lambda j: (0, j)),
        out_shape=jax.ShapeDtypeStruct((rows, n), F32),
        compiler_params=_cparams(("arbitrary",)),
    )(cc, ada_w, ada_b.reshape(1, n))


def _mod_kernel(x_ref, sc_ref, sh_ref, o_ref):
    xn = _normalize_rows(x_ref[0], 1e-6)
    o_ref[0] = (xn * (1.0 + sc_ref[0]) + sh_ref[0]).astype(o_ref.dtype)


def _modulate(x, scale, shift, tr):
    b, l, d = x.shape
    per_batch = scale.shape[0] == b
    smap = (lambda i, j: (i, 0, 0)) if per_batch else (lambda i, j: (0, 0, 0))
    return pl.pallas_call(
        _mod_kernel,
        grid=(b, l // tr),
        in_specs=[pl.BlockSpec((1, tr, d), lambda i, j: (i, j, 0)),
                  pl.BlockSpec((1, 1, d), smap),
                  pl.BlockSpec((1, 1, d), smap)],
        out_specs=pl.BlockSpec((1, tr, d), lambda i, j: (i, j, 0)),
        out_shape=jax.ShapeDtypeStruct((b, l, d), BF16),
        compiler_params=_cparams(("arbitrary", "arbitrary")),
    )(x, scale, shift)


def _mm_kernel(a_ref, b_ref, o_ref, *, slab):
    acc = jnp.dot(a_ref[...], b_ref[...], preferred_element_type=F32)
    if slab:
        for s in range(o_ref.shape[0]):
            o_ref[s] = acc[:, s * LANES:(s + 1) * LANES].astype(o_ref.dtype)
    else:
        o_ref[...] = acc.astype(o_ref.dtype)


def _matmul(a, b, tm, tn, out_dtype=F32, slab=False, ncols=None):
    m, k = a.shape
    n = b.shape[1] if ncols is None else ncols
    if slab:
        out_shape = jax.ShapeDtypeStruct((n // LANES, m, LANES), out_dtype)
        out_spec = pl.BlockSpec((tn // LANES, tm, LANES), lambda i, j: (j, i, 0))
    else:
        out_shape = jax.ShapeDtypeStruct((m, n), out_dtype)
        out_spec = pl.BlockSpec((tm, tn), lambda i, j: (i, j))
    return pl.pallas_call(
        functools.partial(_mm_kernel, slab=slab),
        grid=(m // tm, n // tn),
        in_specs=[pl.BlockSpec((tm, k), lambda i, j: (i, 0)),
                  pl.BlockSpec((k, tn), lambda i, j: (0, j))],
        out_specs=out_spec,
        out_shape=out_shape,
        compiler_params=_cparams(("arbitrary", "arbitrary")),
    )(a, b)


def _conv3_kernel(x_ref, w_ref, b_ref, o_ref, *, silu):
    x = x_ref[0].astype(F32)
    l = x.shape[0]
    row = lax.broadcasted_iota(jnp.int32, x.shape, 0)
    prev = jnp.where(row == 0, 0.0, pltpu.roll(x, 1, 0))
    nxt = jnp.where(row == l - 1, 0.0, pltpu.roll(x, l - 1, 0))
    w = w_ref[0]
    y = prev * w[0:1] + x * w[1:2] + nxt * w[2:3] + b_ref[0]
    if silu:
        y = _silu(y)
    o_ref[0] = y.astype(o_ref.dtype)


def _conv3(p_slab, slab0, nslab, w, bias, seq, silu):
    rows = p_slab.shape[1]
    nb = rows // seq
    return pl.pallas_call(
        functools.partial(_conv3_kernel, silu=silu),
        grid=(nslab, nb),
        in_specs=[pl.BlockSpec((1, seq, LANES), lambda s, b: (slab0 + s, b, 0)),
                  pl.BlockSpec((1, 3, LANES), lambda s, b: (s, 0, 0)),
                  pl.BlockSpec((1, 1, LANES), lambda s, b: (s, 0, 0))],
        out_specs=pl.BlockSpec((1, seq, LANES), lambda s, b: (s, b, 0)),
        out_shape=jax.ShapeDtypeStruct((nslab, rows, LANES), BF16),
        compiler_params=_cparams(("arbitrary", "arbitrary")),
    )(p_slab, w, bias)


def _hyconv_kernel(x_ref, w_ref, b_ref, o_ref, *, seq, nb):
    nj = seq // FFT_N1

    def gather(n1):
        return jnp.concatenate(
            [x_ref[0, pl.ds(n1 + b * seq, nj, stride=FFT_N1), :] for b in range(nb)], axis=0)

    w = w_ref[0]
    w0, w1, w2, bias = w[0:1], w[1:2], w[2:3], b_ref[0]
    jrow = lax.broadcasted_iota(jnp.int32, (nb * nj, LANES), 0) % nj

    def shift_down(g):
        return jnp.where(jrow == 0, 0.0, pltpu.roll(g, 1, 0))

    def shift_up(g):
        return jnp.where(jrow == nj - 1, 0.0, pltpu.roll(g, nb * nj - 1, 0))

    g_first = gather(0)
    g_second = gather(1)
    g_last = gather(FFT_N1 - 1)
    dt = o_ref.dtype
    o_ref[0, 0] = (shift_down(g_last) * w0 + g_first * w1 + g_second * w2 + bias).astype(dt)

    def body(n1, carry):
        g_prev, g_cur = carry
        g_next = gather(n1 + 1)
        o_ref[0, n1] = (g_prev * w0 + g_cur * w1 + g_next * w2 + bias).astype(dt)
        return g_cur, g_next

    g_prev, g_cur = lax.fori_loop(1, FFT_N1 - 1, body, (g_first, g_second), unroll=6)
    o_ref[0, FFT_N1 - 1] = (g_prev * w0 + g_cur * w1 + shift_up(g_first) * w2 + bias).astype(dt)


def _hyconv(p_slab, nslab, w, bias, seq):
    rows = p_slab.shape[1]
    nb = rows // seq
    nj = seq // FFT_N1
    return pl.pallas_call(
        functools.partial(_hyconv_kernel, seq=seq, nb=nb),
        grid=(nslab,),
        in_specs=[pl.BlockSpec((1, rows, LANES), lambda s: (s, 0, 0)),
                  pl.BlockSpec((1, 3, LANES), lambda s: (s, 0, 0)),
                  pl.BlockSpec((1, 1, LANES), lambda s: (s, 0, 0))],
        out_specs=pl.BlockSpec((1, FFT_N1, nb * nj, LANES), lambda s: (s, 0, 0, 0)),
        out_shape=jax.ShapeDtypeStruct((nslab, FFT_N1, nb * nj, LANES), BF16),
        compiler_params=_cparams(("arbitrary",)),
    )(p_slab, w, bias)


@functools.lru_cache(maxsize=None)
def _dft_tables(seq):
    nj = seq // FFT_N1
    n2 = nj + nj // 2
    n = FFT_N1 * n2
    out0 = nj // 2
    n1 = np.arange(FFT_N1, dtype=np.float64)
    k2 = np.arange(n2, dtype=np.float64)
    jn = np.arange(nj, dtype=np.float64)
    ang = -2.0 * np.pi * (k2[None, :, None] * jn[None, None, :] / n2 + n1[:, None, None] * k2[None, :, None] / n)
    cr, ci = np.cos(ang), np.sin(ang)
    fa = np.concatenate([np.concatenate([cr, -ci], axis=2), np.concatenate([ci, cr], axis=2)], axis=1)
    k1 = np.arange(FFT_N1, dtype=np.float64)
    angb = -2.0 * np.pi * np.outer(k1, n1) / FFT_N1
    dr, di = np.cos(angb), np.sin(angb)
    fb = np.block([[dr, -di], [di, dr]])
    fbi = np.block([[dr, di], [-di, dr]])
    angi = 2.0 * np.pi * (n1[:, None, None] * k2[None, None, :] / n + (out0 + jn)[None, :, None] * k2[None, None, :] / n2)
    er, ei = np.cos(angi) / n, np.sin(angi) / n
    fai = np.concatenate([np.concatenate([er, -ei], axis=2), np.concatenate([ei, er], axis=2)], axis=1)
    return (fa.astype(np.float32), fb.astype(np.float32), fbi.astype(np.float32), fai.astype(np.float32), n2)


@functools.lru_cache(maxsize=None)
def _filter_consts(seq):
    t = np.linspace(0.0, 1.0, seq, dtype=np.float32)[:, None].astype(np.float64)
    w = 2.0 * np.pi * np.arange(seq, dtype=np.float32)[:, None].astype(np.float64) / seq
    bands = np.linspace(1e-4, HY_BANDS - 1, HY_BANDS, dtype=np.float32)[None, :].astype(np.float64)
    feats = np.concatenate([t, np.cos(bands * w), -np.sin(bands * w)], axis=-1)
    feats_p = np.zeros((seq, LANES), np.float32)
    feats_p[:, :HY_EMB] = feats
    dist = (np.abs(np.arange(seq) - seq // 2).astype(np.float32) / np.float32(seq / 2.0)).reshape(seq, 1)
    deltas = np.abs(np.linspace(HY_MIN_DECAY, HY_MAX_DECAY, HY_WIDTH, dtype=np.float32))
    return feats_p, dist.astype(np.float32), deltas.astype(np.float32)


def _dot_f32(a, b):
    def split(x):
        hi = x.astype(BF16)
        r1 = x - hi.astype(F32)
        mid = r1.astype(BF16)
        lo = (r1 - mid.astype(F32)).astype(BF16)
        return hi, mid, lo
    a0, a1, a2 = split(a)
    b0, b1, b2 = split(b)
    d = functools.partial(jnp.dot, preferred_element_type=F32)
    return (d(a0, b0) + (d(a0, b1) + d(a1, b0)) + (d(a0, b2) + d(a1, b1) + d(a2, b0)))


def _filter_kernel(feats_ref, w1_ref, b1_ref, f1_ref, w2_ref, b2_ref, f2_ref, w3_ref, dist_ref, delta_ref,
                   fa_ref, fb_ref, o_ref, hid_ref, h_ref, sar_ref, sai_ref, *, seq, n2, pitch):
    nj = seq // FFT_N1

    @pl.when((pl.program_id(0) == 0) & (pl.program_id(1) == 0))
    def _():
        h1 = jnp.sin(f1_ref[...] * (_dot_f32(feats_ref[...], w1_ref[...]) + b1_ref[...]))
        hid_ref[...] = jnp.sin(f2_ref[...] * (_dot_f32(h1, w2_ref[...]) + b2_ref[...])).astype(BF16)

    ns = h_ref.shape[0]
    lanes = lambda f: jnp.concatenate([f(s) for s in range(ns)], axis=1)
    w3 = lanes(lambda s: w3_ref[0, s]).astype(BF16)
    h = jnp.dot(hid_ref[...], w3, preferred_element_type=F32)
    h = h * jnp.exp(-dist_ref[...] * lanes(lambda s: delta_ref[s]))
    h = h / (jnp.sum(jnp.abs(h), axis=0, keepdims=True) + 1e-6)
    for s in range(ns):
        h_ref[s] = h[:, s * LANES:(s + 1) * LANES]

    def stage_a(n1, c):
        g = lanes(lambda s: h_ref[s, pl.ds(n1, nj, stride=FFT_N1), :]).astype(BF16)
        y = jnp.dot(fa_ref[n1], g, preferred_element_type=F32)
        base = pl.multiple_of(n1 * pitch, 8)
        for s in range(ns):
            sar_ref[s, pl.ds(base, n2), :] = y[:n2, s * LANES:(s + 1) * LANES]
            sai_ref[s, pl.ds(base, n2), :] = y[n2:, s * LANES:(s + 1) * LANES]
        return c

    lax.fori_loop(0, FFT_N1, stage_a, 0, unroll=8)

    def stage_b(k2, c):
        yr = lanes(lambda s: sar_ref[s, pl.ds(k2, FFT_N1, stride=pitch), :])
        yi = lanes(lambda s: sai_ref[s, pl.ds(k2, FFT_N1, stride=pitch), :])
        op = jnp.concatenate([yr, yi], axis=0).astype(BF16)
        res = jnp.dot(fb_ref[...], op, preferred_element_type=F32).astype(o_ref.dtype)
        for s in range(ns):
            o_ref[0, s, k2] = res[:, s * LANES:(s + 1) * LANES]
        return c

    lax.fori_loop(0, n2, stage_b, 0, unroll=4)


def _hyena_filter_spectra(p, seq):
    fa, fb, _, _, n2 = _dft_tables(seq)
    nj = seq // FFT_N1
    pitch = n2 + SA_PITCH_PAD
    feats, dist, deltas = _filter_consts(seq)
    nsl = HY_WIDTH // LANES

    def padk(w, rows):
        return jnp.zeros((rows, w.shape[1]), F32).at[:w.shape[0]].set(w)

    def padn(v):
        return jnp.zeros((1, LANES), F32).at[0, :v.shape[0]].set(v)

    w1 = jnp.zeros((LANES, LANES), F32).at[:HY_EMB, :HY_FFN].set(p['hy_ffn_w1'])
    w2 = jnp.zeros((LANES, LANES), F32).at[:HY_FFN, :HY_FFN].set(p['hy_ffn_w2'])
    w3 = padk(p['hy_ffn_w3'], LANES).reshape(LANES, 2, nsl, LANES).transpose(1, 2, 0, 3)
    fa_real = jnp.asarray(fa[:, :, :nj]).astype(BF16)
    const = lambda shape: pl.BlockSpec(shape, lambda o, s: (0,) * len(shape))
    ns = 2
    return pl.pallas_call(
        functools.partial(_filter_kernel, seq=seq, n2=n2, pitch=pitch),
        grid=(2, nsl // ns),
        in_specs=[const((seq, LANES)), const((LANES, LANES)), const((1, LANES)), const((1, LANES)),
                  const((LANES, LANES)), const((1, LANES)), const((1, LANES)),
                  pl.BlockSpec((1, ns, LANES, LANES), lambda o, s: (o, s, 0, 0)),
                  const((seq, 1)),
                  pl.BlockSpec((ns, 1, LANES), lambda o, s: (s, 0, 0)),
                  const((FFT_N1, 2 * n2, nj)), const((2 * FFT_N1, 2 * FFT_N1))],
        out_specs=pl.BlockSpec((1, ns, n2, 2 * FFT_N1, LANES), lambda o, s: (o, s, 0, 0, 0)),
        out_shape=jax.ShapeDtypeStruct((2, nsl, n2, 2 * FFT_N1, LANES), BF16),
        scratch_shapes=[pltpu.VMEM((seq, LANES), BF16), pltpu.VMEM((ns, seq, LANES), F32),
                        pltpu.VMEM((ns, FFT_N1 * pitch, LANES), F32),
                        pltpu.VMEM((ns, FFT_N1 * pitch, LANES), F32)],
        compiler_params=_cparams(("arbitrary", "arbitrary")),
    )(jnp.asarray(feats), w1, padn(p['hy_ffn_b1']), padn(p['hy_freq1']), w2, padn(p['hy_ffn_b2']),
      padn(p['hy_freq2']), w3, jnp.asarray(dist), jnp.asarray(deltas).reshape(nsl, 1, LANES),
      fa_real, jnp.asarray(fb).astype(BF16))


def _longconv_kernel(x_ref, g_ref, skip_ref, h_ref, fa_ref, fb_ref, fbi_ref, fai_ref, o_ref,
                     sar_ref, sai_ref, sbr_ref, sbi_ref, *, seq, n2, pitch, natural_out):
    nj = seq // FFT_N1

    def stage_a(n1, c):
        y = jnp.dot(fa_ref[n1], x_ref[0, n1].astype(BF16), preferred_element_type=F32)
        base = pl.multiple_of(n1 * pitch, 8)
        sar_ref[pl.ds(base, n2), :] = y[:n2]
        sai_ref[pl.ds(base, n2), :] = y[n2:]
        return c

    lax.fori_loop(0, FFT_N1, stage_a, 0, unroll=8)

    def stage_b(kp, c):
        ks = (2 * kp, 2 * kp + 1)
        yr = jnp.concatenate([sar_ref[pl.ds(k, FFT_N1, stride=pitch), :] for k in ks], axis=1)
        yi = jnp.concatenate([sai_ref[pl.ds(k, FFT_N1, stride=pitch), :] for k in ks], axis=1)
        xf = jnp.dot(fb_ref[...], jnp.concatenate([yr, yi], axis=0).astype(BF16), preferred_element_type=F32)
        xr, xi = xf[:FFT_N1], xf[FFT_N1:]
        hf = jnp.concatenate([h_ref[0, k] for k in ks], axis=1).astype(F32)
        hr, hi = hf[:FFT_N1], hf[FFT_N1:]
        z = jnp.concatenate([xr * hr - xi * hi, xr * hi + xi * hr], axis=0).astype(BF16)
        w = jnp.dot(fbi_ref[...], z, preferred_element_type=F32)
        for q, k in enumerate(ks):
            base = pl.multiple_of(k * SB_PITCH, 8)
            sbr_ref[pl.ds(base, FFT_N1), :] = w[:FFT_N1, q * LANES:(q + 1) * LANES]
            sbi_ref[pl.ds(base, FFT_N1), :] = w[FFT_N1:, q * LANES:(q + 1) * LANES]
        return c

    lax.fori_loop(0, n2 // 2, stage_b, 0, unroll=2)

    skip = skip_ref[0]

    def stage_c(n1, c):
        wr = sbr_ref[pl.ds(n1, n2, stride=SB_PITCH), :]
        wi = sbi_ref[pl.ds(n1, n2, stride=SB_PITCH), :]
        y = jnp.dot(fai_ref[n1], jnp.concatenate([wr, wi], axis=0).astype(BF16), preferred_element_type=F32)
        res = g_ref[0, n1].astype(F32) * (y + x_ref[0, n1].astype(F32) * skip)
        if natural_out:
            for b in range(res.shape[0] // nj):
                o_ref[0, pl.ds(n1 + b * seq, nj, stride=FFT_N1), :] = res[b * nj:(b + 1) * nj]
        else:
            o_ref[0, n1] = res.astype(o_ref.dtype)
        return c

    lax.fori_loop(0, FFT_N1, stage_c, 0, unroll=8)


def _longconv(xg, gg, slab_x, slab_g, skip, hspec, order, seq, natural_out):
    fa, fb, fbi, fai, n2 = _dft_tables(seq)
    pitch = n2 + SA_PITCH_PAD
    nsl = HY_WIDTH // LANES
    rows = xg.shape[2]
    nb = rows // (seq // FFT_N1)
    const = lambda shape: pl.BlockSpec(shape, lambda s: (0,) * len(shape))
    if natural_out:
        out_shape = jax.ShapeDtypeStruct((nsl, nb * seq, LANES), F32)
        out_spec = pl.BlockSpec((1, nb * seq, LANES), lambda s: (s, 0, 0))
    else:
        out_shape = jax.ShapeDtypeStruct((nsl, FFT_N1, rows, LANES), BF16)
        out_spec = pl.BlockSpec((1, FFT_N1, rows, LANES), lambda s: (s, 0, 0, 0))
    return pl.pallas_call(
        functools.partial(_longconv_kernel, seq=seq, n2=n2, pitch=pitch, natural_out=natural_out),
        grid=(nsl,),
        in_specs=[pl.BlockSpec((1, FFT_N1, rows, LANES), lambda s: (slab_x + s, 0, 0, 0)),
                  pl.BlockSpec((1, FFT_N1, rows, LANES), lambda s: (slab_g + s, 0, 0, 0)),
                  pl.BlockSpec((1, 1, LANES), lambda s: (s, 0, 0)),
                  pl.BlockSpec((None, 1, n2, 2 * FFT_N1, LANES), lambda s: (order, s, 0, 0, 0)),
                  const(fa.shape), const(fb.shape), const(fbi.shape), const(fai.shape)],
        out_specs=out_spec,
        out_shape=out_shape,
        scratch_shapes=[pltpu.VMEM((FFT_N1 * pitch, LANES), F32), pltpu.VMEM((FFT_N1 * pitch, LANES), F32),
                        pltpu.VMEM((n2 * SB_PITCH, LANES), F32), pltpu.VMEM((n2 * SB_PITCH, LANES), F32)],
        compiler_params=_cparams(("arbitrary",)),
    )(xg, gg, skip.reshape(nsl, 1, LANES), hspec,
      jnp.asarray(fa).astype(BF16), jnp.asarray(fb).astype(BF16),
      jnp.asarray(fbi).astype(BF16), jnp.asarray(fai).astype(BF16))


def _dot_exact_lhs(lhs_bf16, x):
    hi = x.astype(BF16)
    r1 = x - hi.astype(F32)
    mid = r1.astype(BF16)
    lo = (r1 - mid.astype(F32)).astype(BF16)
    d = functools.partial(jnp.dot, preferred_element_type=F32)
    return d(lhs_bf16, hi) + (d(lhs_bf16, mid) + d(lhs_bf16, lo))


def _ssd_kernel(x_ref, b_ref, c_ref, dt_ref, bias_ref, alog_ref, e_ref, init_ref, y_ref, fin_ref, st_ref,
                *, rev, nc):
    ci = pl.program_id(1)
    gw = SSD_HPG * SSD_HEAD_DIM
    dot = functools.partial(jnp.dot, preferred_element_type=F32)

    @pl.when(ci == 0)
    def _():
        st_ref[...] = init_ref[0]

    off = SSD_HEADS if rev else 0
    dt_all = _softplus(dt_ref[...] + bias_ref[...])
    a_all = dt_all * (-jnp.exp(alog_ref[...]))
    r_i = lax.broadcasted_iota(jnp.int32, (SSD_CHUNK, SSD_CHUNK), 0)
    c_i = lax.broadcasted_iota(jnp.int32, (SSD_CHUNK, SSD_CHUNK), 1)
    valid = (c_i >= r_i) if rev else (c_i <= r_i)
    tri = jnp.where(valid, 1.0, 0.0).astype(BF16)
    acum = _dot_exact_lhs(tri, a_all)
    src_t = (acum - jnp.log(dt_all)).T
    tot = acum[0:1, :] if rev else acum[SSD_CHUNK - 1:SSD_CHUNK, :]
    dtw_b = (dt_all * jnp.exp(tot - acum)).astype(BF16)
    cd8 = jnp.broadcast_to(jnp.exp(tot), (8, LANES))
    cd_hi = cd8.astype(BF16)
    cd_r = cd8 - cd_hi.astype(F32)
    cd_mid = cd_r.astype(BF16)
    cd_lo = (cd_r - cd_mid.astype(F32)).astype(BF16)
    lo_half = c_i < SSD_HEAD_DIM
    zero_b = jnp.zeros((), BF16)

    for g in range(SSD_GROUPS):
        bt = b_ref[g].astype(F32).T.astype(BF16)
        cg = c_ref[g].astype(BF16)
        cb = dot(cg, bt)
        for sl in range(gw // LANES):
            col = slice(sl * LANES, (sl + 1) * LANES)
            e_s = e_ref[:, g * gw + sl * LANES:g * gw + (sl + 1) * LANES]
            xb = x_ref[(gw // LANES) * g + sl]
            ms, ea = [], []
            for hh in range(2):
                h = off + SSD_HPG * g + 2 * sl + hh
                a_col = jnp.broadcast_to(acum[:, h:h + 1], (SSD_CHUNK, SSD_CHUNK))
                seg = jnp.where(valid, a_col - src_t[h:h + 1, :], -1e30)
                ms.append((cb * jnp.exp(seg)).astype(BF16))
                ea.append(jnp.exp(a_col))
            rhs = jnp.concatenate([jnp.where(lo_half, xb, zero_b), jnp.where(lo_half, zero_b, xb)], axis=0)
            yd = dot(jnp.concatenate(ms, axis=1), rhs)
            sgs = st_ref[g, :, col]
            yoff = dot(cg, sgs.astype(BF16)) * jnp.where(lo_half, ea[0], ea[1])
            y_ref[g, :, col] = (yd + yoff).astype(y_ref.dtype)
            wm = (xb.astype(F32) * dot(dtw_b, e_s)).astype(BF16)
            cd = (dot(cd_hi, e_s) + (dot(cd_mid, e_s) + dot(cd_lo, e_s)))[0:1]
            st_ref[g, :, col] = sgs * cd + dot(bt, wm)

    @pl.when(ci == nc - 1)
    def _():
        fin_ref[0] = st_ref[...]


def _ssd_scan(xbc, slab_x, slab_b, slab_c, dt, dt_bias, a_log, init, seq, rev):
    rows = xbc.shape[1]
    nb = rows // seq
    nc = seq // SSD_CHUNK
    gw = SSD_HPG * SSD_HEAD_DIM
    xs = SSD_INNER // LANES
    cpos = (lambda b, c: b * nc + (nc - 1 - c)) if rev else (lambda b, c: b * nc + c)
    expand = np.zeros((2 * SSD_HEADS, SSD_INNER), np.float32)
    for h in range(SSD_HEADS):
        expand[(SSD_HEADS if rev else 0) + h, h * SSD_HEAD_DIM:(h + 1) * SSD_HEAD_DIM] = 1.0
    return pl.pallas_call(
        functools.partial(_ssd_kernel, rev=rev, nc=nc),
        grid=(nb, nc),
        in_specs=[pl.BlockSpec((xs, SSD_CHUNK, LANES), lambda b, c: (slab_x // xs, cpos(b, c), 0)),
                  pl.BlockSpec((SSD_GROUPS, SSD_CHUNK, LANES), lambda b, c: (slab_b // SSD_GROUPS, cpos(b, c), 0)),
                  pl.BlockSpec((SSD_GROUPS, SSD_CHUNK, LANES), lambda b, c: (slab_c // SSD_GROUPS, cpos(b, c), 0)),
                  pl.BlockSpec((SSD_CHUNK, LANES), lambda b, c: (cpos(b, c), 0)),
                  pl.BlockSpec((1, LANES), lambda b, c: (0, 0)),
                  pl.BlockSpec((1, LANES), lambda b, c: (0, 0)),
                  pl.BlockSpec((2 * SSD_HEADS, SSD_INNER), lambda b, c: (0, 0)),
                  pl.BlockSpec((1, SSD_GROUPS, SSD_STATE, gw), lambda b, c: (b, 0, 0, 0))],
        out_specs=[pl.BlockSpec((SSD_GROUPS, SSD_CHUNK, gw), lambda b, c: (0, cpos(b, c), 0)),
                   pl.BlockSpec((1, SSD_GROUPS, SSD_STATE, gw), lambda b, c: (b, 0, 0, 0))],
        out_shape=[jax.ShapeDtypeStruct((SSD_GROUPS, rows, gw), BF16),
                   jax.ShapeDtypeStruct((nb, SSD_GROUPS, SSD_STATE, gw), F32)],
        scratch_shapes=[pltpu.VMEM((SSD_GROUPS, SSD_STATE, gw), F32)],
        compiler_params=_cparams(("arbitrary", "arbitrary")),
    )(xbc, xbc, xbc, dt, dt_bias.reshape(1, LANES), a_log.reshape(1, LANES),
      jnp.asarray(expand).astype(BF16), init)


def _ssdnorm_kernel(yf_ref, yb_ref, x_ref, z_ref, d_ref, w_ref, o_ref):
    nsl = x_ref.shape[0]
    x = jnp.concatenate([x_ref[i] for i in range(nsl)], axis=1).astype(F32)
    z = jnp.concatenate([z_ref[i] for i in range(nsl)], axis=1).astype(F32)
    y = yf_ref[0].astype(F32) + yb_ref[0].astype(F32) + x * d_ref[0]
    gx = y * _silu(z)
    ms = jnp.mean(gx * gx, axis=-1, keepdims=True)
    o_ref[...] = (gx * lax.rsqrt(ms + 1e-5) * w_ref[0]).astype(o_ref.dtype)


def _ssd_norm(yf, yb, xbc, proj, slab_z, d_ch, w, tr):
    g, rows, gw = yf.shape
    nsl = gw // LANES
    return pl.pallas_call(
        _ssdnorm_kernel,
        grid=(g, rows // tr),
        in_specs=[pl.BlockSpec((1, tr, gw), lambda q, i: (q, i, 0)),
                  pl.BlockSpec((1, tr, gw), lambda q, i: (q, i, 0)),
                  pl.BlockSpec((nsl, tr, LANES), lambda q, i: (q, i, 0)),
                  pl.BlockSpec((nsl, tr, LANES), lambda q, i: (slab_z // nsl + q, i, 0)),
                  pl.BlockSpec((1, 1, gw), lambda q, i: (q, 0, 0)),
                  pl.BlockSpec((1, 1, gw), lambda q, i: (q, 0, 0))],
        out_specs=pl.BlockSpec((tr, gw), lambda q, i: (i, q)),
        out_shape=jax.ShapeDtypeStruct((rows, g * gw), BF16),
        compiler_params=_cparams(("arbitrary", "arbitrary")),
    )(yf, yb, xbc, proj, d_ch.reshape(g, 1, gw), w.reshape(g, 1, gw))


def _merge_kernel(yhy_ref, yss_ref, whh_ref, wss_ref, gh_ref, gs_ref, o_ref):
    a1 = jnp.concatenate([yhy_ref[i] for i in range(yhy_ref.shape[0])], axis=1).astype(BF16)
    p1 = jnp.dot(a1, whh_ref[...], preferred_element_type=F32)
    p2 = jnp.dot(yss_ref[...], wss_ref[...], preferred_element_type=F32)
    gh = jnp.concatenate([gh_ref[i] for i in range(gh_ref.shape[0])], axis=1).astype(F32)
    gs = jnp.concatenate([gs_ref[i] for i in range(gs_ref.shape[0])], axis=1).astype(F32)
    o_ref[...] = (_sigmoid(gh) * p1 + _sigmoid(gs) * p2).astype(o_ref.dtype)


def _branch_merge(yhy, yss, w_hy, w_ssd, proj, slab_gh, slab_gs, tm, tn):
    nsl_k = yhy.shape[0]
    rows = yhy.shape[1]
    n = w_hy.shape[1]
    ts = tn // LANES
    return pl.pallas_call(
        _merge_kernel,
        grid=(rows // tm, n // tn),
        in_specs=[pl.BlockSpec((nsl_k, tm, LANES), lambda i, j: (0, i, 0)),
                  pl.BlockSpec((tm, yss.shape[1]), lambda i, j: (i, 0)),
                  pl.BlockSpec((w_hy.shape[0], tn), lambda i, j: (0, j)),
                  pl.BlockSpec((w_ssd.shape[0], tn), lambda i, j: (0, j)),
                  pl.BlockSpec((ts, tm, LANES), lambda i, j: (slab_gh // ts + j, i, 0)),
                  pl.BlockSpec((ts, tm, LANES), lambda i, j: (slab_gs // ts + j, i, 0))],
        out_specs=pl.BlockSpec((tm, tn), lambda i, j: (i, j)),
        out_shape=jax.ShapeDtypeStruct((rows, n), BF16),
        compiler_params=_cparams(("arbitrary", "arbitrary")),
    )(yhy, yss, w_hy, w_ssd, proj, proj)


def _outproj_ln_kernel(a_ref, w_ref, x_ref, gt_ref, g_ref, b_ref, sc_ref, sh_ref, h_ref, m_ref, mt_ref):
    y = jnp.dot(a_ref[0], w_ref[...], preferred_element_type=F32)
    h = _normalize_rows(DEEPNORM_ALPHA * x_ref[0] + gt_ref[0] * y, 1e-6) * g_ref[...] + b_ref[...]
    h_ref[0] = h
    m = _normalize_rows(h, 1e-6) * (1.0 + sc_ref[0]) + sh_ref[0]
    m_ref[0] = m.astype(m_ref.dtype)
    mt_ref[...] = m.T.astype(mt_ref.dtype)


def _outproj_ln(merged, w_out, x, gate, ln_g, ln_b, scale2, shift2, tm):
    b, l, d = x.shape
    nt = l // tm
    vec = pl.BlockSpec((1, 1, d), lambda i, j: (i, 0, 0))
    cvec = pl.BlockSpec((1, d), lambda i, j: (0, 0))
    tile = pl.BlockSpec((1, tm, d), lambda i, j: (i, j, 0))
    return pl.pallas_call(
        _outproj_ln_kernel,
        grid=(b, nt),
        in_specs=[tile, pl.BlockSpec((d, d), lambda i, j: (0, 0)), tile, vec, cvec, cvec, vec, vec],
        out_specs=[tile, tile, pl.BlockSpec((d, tm), lambda i, j: (0, i * nt + j))],
        out_shape=[jax.ShapeDtypeStruct((b, l, d), F32), jax.ShapeDtypeStruct((b, l, d), BF16),
                   jax.ShapeDtypeStruct((d, b * l), BF16)],
        compiler_params=_cparams(("arbitrary", "arbitrary")),
    )(merged, w_out, x, gate, ln_g.reshape(1, d), ln_b.reshape(1, d), scale2, shift2)


def _resid_ln_kernel(x_ref, y_ref, gt_ref, g_ref, b_ref, o_ref):
    o_ref[0] = _normalize_rows(DEEPNORM_ALPHA * x_ref[0] + gt_ref[0] * y_ref[0], 1e-6) * g_ref[...] + b_ref[...]


def _resid_ln(x, y, gate, ln_g, ln_b, tm):
    b, l, d = x.shape
    tile = pl.BlockSpec((1, tm, d), lambda i, j: (i, j, 0))
    return pl.pallas_call(
        _resid_ln_kernel,
        grid=(b, l // tm),
        in_specs=[tile, tile, pl.BlockSpec((1, 1, d), lambda i, j: (i, 0, 0)),
                  pl.BlockSpec((1, d), lambda i, j: (0, 0)), pl.BlockSpec((1, d), lambda i, j: (0, 0))],
        out_specs=tile,
        out_shape=jax.ShapeDtypeStruct((b, l, d), F32),
        compiler_params=_cparams(("arbitrary", "arbitrary")),
    )(x, y, gate, ln_g.reshape(1, d), ln_b.reshape(1, d))


def _peer_scores_kernel(a_ref, wq_ref, k_ref, s_ref):
    q = jnp.dot(a_ref[...], wq_ref[...], preferred_element_type=F32)
    hd = PEER_DKEY // 2
    nt = (((1,), (1,)), ((), ()))
    for h in range(PEER_HEADS):
        qn = _normalize_rows(q[:, h * PEER_DKEY:(h + 1) * PEER_DKEY], 1e-6).astype(BF16)
        s_ref[h, 0] = lax.dot_general(k_ref[h, 0], qn[:, :hd], nt, preferred_element_type=F32)
        s_ref[h, 1] = lax.dot_general(k_ref[h, 1], qn[:, hd:], nt, preferred_element_type=F32)


def _peer_scores(m2, wq, subkeys, tm):
    t, d = m2.shape
    return pl.pallas_call(
        _peer_scores_kernel,
        grid=(t // tm,),
        in_specs=[pl.BlockSpec((tm, d), lambda i: (i, 0)),
                  pl.BlockSpec(wq.shape, lambda i: (0, 0)),
                  pl.BlockSpec(subkeys.shape, lambda i: (0, 0, 0, 0))],
        out_specs=pl.BlockSpec((PEER_HEADS, 2, PEER_NKEYS, tm), lambda i: (0, 0, 0, i)),
        out_shape=jax.ShapeDtypeStruct((PEER_HEADS, 2, PEER_NKEYS, t), F32),
        compiler_params=_cparams(("arbitrary",)),
    )(m2, wq, subkeys)


def _top_values(x, k):
    out = []
    for _ in range(k):
        m = jnp.max(x, axis=0, keepdims=True)
        out.append(m)
        x = jnp.where(x == m, -jnp.inf, x)
    return out


def _top_values_ranked(x, k):
    out = []
    big = 2.0 ** 100
    for a in range(k):
        m = jnp.max(x, axis=0, keepdims=True)
        out.append(m)
        x = jnp.where(x == m, -big * (1.0 + a / 64.0), x)
    rank = jnp.where(x <= -big, (x * (-1.0 / big) - 1.0) * 64.0, float(k))
    return out, rank


def _peer_stats_kernel(s_ref, rk_ref, lim_ref, p1_ref, p2_ref):
    k = PEER_TOPK
    for h in range(PEER_HEADS):
        s1 = s_ref[h, 0]
        s2 = s_ref[h, 1]
        v1, r1 = _top_values_ranked(s1, k)
        v2, rk = _top_values_ranked(s2, k)
        v2m = jnp.concatenate(v2, axis=0)
        grid = [v1[a] + v2m[:k // (a + 1)] for a in range(k)]
        n_cand = sum(g.shape[0] for g in grid)
        pad = jnp.full((-n_cand % 8, s1.shape[1]), -jnp.inf, F32)
        best = _top_values(jnp.concatenate(grid + [pad], axis=0), k)
        z = best[0] * 0.0
        for c in best:
            z = z + jnp.exp(c - best[0])
        lim = jnp.zeros_like(s1)
        for a in range(k):
            n_sel = jnp.sum(jnp.where(grid[a] >= best[k - 1], 1.0, 0.0), axis=0, keepdims=True)
            lim = jnp.where(r1 == float(a), n_sel, lim)
        lim_ref[h] = lim
        rk_ref[h] = rk.astype(rk_ref.dtype)
        p1_ref[h] = jnp.exp(s1 - v1[0])
        p2_ref[h] = (0.5 * jnp.exp(s2 - v2[0]) / z).astype(p2_ref.dtype)


def _peer_stats(scores, tm):
    t = scores.shape[-1]
    hk = pl.BlockSpec((PEER_HEADS, PEER_NKEYS, tm), lambda i: (0, 0, i))
    f32 = jax.ShapeDtypeStruct((PEER_HEADS, PEER_NKEYS, t), F32)
    b16 = jax.ShapeDtypeStruct((PEER_HEADS, PEER_NKEYS, t), BF16)
    return pl.pallas_call(
        _peer_stats_kernel,
        grid=(t // tm,),
        in_specs=[pl.BlockSpec((PEER_HEADS, 2, PEER_NKEYS, tm), lambda i: (0, 0, 0, i))],
        out_specs=[hk, hk, hk, hk],
        out_shape=[b16, f32, f32, b16],
        compiler_params=_cparams(("arbitrary",)),
    )(scores)


def _peer_dense_kernel(ht_ref, u_ref, vt_ref, rk_in, lim_ref, p1_ref, p2_in, o_ref, acc_ref, g_ref, rk_ref, p2_ref,
                       *, ne1):
    j = pl.program_id(1)

    @pl.when(j == 0)
    def _():
        acc_ref[...] = jnp.zeros_like(acc_ref)
        rk_ref[...] = rk_in[...]
        p2_ref[...] = p2_in[...]

    zero = jnp.zeros((), BF16)
    for r in range(ne1):
        e1 = j * ne1 + r
        rows = slice(r * PEER_NKEYS, (r + 1) * PEER_NKEYS)
        lim_rows = [lim_ref[h, pl.ds(e1, 1), :].astype(BF16) for h in range(PEER_HEADS)]
        p1_rows = [p1_ref[h, pl.ds(e1, 1), :].astype(BF16) for h in range(PEER_HEADS)]
        for tt in range(g_ref.shape[1] // LANES):
            ls = slice(tt * LANES, (tt + 1) * LANES)
            gacc = None
            for h in range(PEER_HEADS):
                hs = slice(h * PEER_NKEYS, (h + 1) * PEER_NKEYS)
                w = jnp.where(rk_ref[hs, ls] < lim_rows[h][:, ls], p2_ref[hs, ls] * p1_rows[h][:, ls], zero)
                gacc = w if gacc is None else gacc + w
            g_ref[rows, ls] = gacc
    for q in range(u_ref.shape[0] // MM_ROWS):
        rs = slice(q * MM_ROWS, (q + 1) * MM_ROWS)
        act = jnp.dot(u_ref[rs, :], ht_ref[...], preferred_element_type=F32)
        gelu2 = act * (1.0 + jnp.tanh(math.sqrt(2.0 / math.pi) * (act + 0.044715 * (act * act * act))))
        g_ref[rs, :] = gelu2.astype(BF16) * g_ref[rs, :]
    pt = g_ref[...]
    for q in range(vt_ref.shape[0] // MM_ROWS):
        rs = slice(q * MM_ROWS, (q + 1) * MM_ROWS)
        acc_ref[rs, :] += jnp.dot(vt_ref[rs, :], pt, preferred_element_type=F32)

    @pl.when(j == pl.num_programs(1) - 1)
    def _():
        o_ref[...] = acc_ref[...].T


def _peer_dense(m2t, u_b, vt_b, rk, lim, p1, p2, tm, te):
    d, t = m2t.shape
    ne = u_b.shape[0]
    hk = pl.BlockSpec((PEER_HEADS, PEER_NKEYS, tm), lambda i, j: (0, 0, i))
    flat = pl.BlockSpec((PEER_HEADS * PEER_NKEYS, tm), lambda i, j: (0, i))
    return pl.pallas_call(
        functools.partial(_peer_dense_kernel, ne1=te // PEER_NKEYS),
        grid=(t // tm, ne // te),
        in_specs=[pl.BlockSpec((d, tm), lambda i, j: (0, i)),
                  pl.BlockSpec((te, d), lambda i, j: (j, 0)),
                  pl.BlockSpec((None, d, te), lambda i, j: (j, 0, 0)),
                  flat, hk, hk, flat],
        out_specs=pl.BlockSpec((tm, d), lambda i, j: (i, 0)),
        out_shape=jax.ShapeDtypeStruct((t, d), F32),
        scratch_shapes=[pltpu.VMEM((d, tm), F32), pltpu.VMEM((te, tm), BF16),
                        pltpu.VMEM((PEER_HEADS * PEER_NKEYS, tm), BF16),
                        pltpu.VMEM((PEER_HEADS * PEER_NKEYS, tm), BF16)],
        compiler_params=_cparams(("arbitrary", "arbitrary")),
    )(m2t, u_b, vt_b, rk.reshape(PEER_HEADS * PEER_NKEYS, t), lim, p1, p2.reshape(PEER_HEADS * PEER_NKEYS, t))


SL_X = 0
SL_B = SL_X + SSD_INNER // LANES
SL_C = SL_B + SSD_GROUPS * SSD_STATE // LANES
SL_Z = SL_C + SSD_GROUPS * SSD_STATE // LANES
SL_GH = SL_Z + SSD_INNER // LANES
SL_GS = SL_GH + D_MODEL // LANES


def kernel(x, c, ctx, c_ctx, ada_w, ada_b, w_in, hy_conv_w, hy_conv_b, hy_ffn_w1, hy_ffn_b1, hy_freq1,
           hy_ffn_w2, hy_ffn_b2, hy_freq2, hy_ffn_w3, hy_skip, ssd_conv_w, ssd_conv_b, ssd_a_log,
           ssd_dt_bias, ssd_d, ssd_norm_w, w_branch_hy, w_branch_ssd, w_out, ln1_g, ln1_b, peer_wq,
           peer_subkeys, peer_u, peer_v, ln2_g, ln2_b):
    nb, seq, d = x.shape
    lc = ctx.shape[1]
    t = nb * seq
    p = dict(hy_ffn_w1=hy_ffn_w1[0], hy_ffn_b1=hy_ffn_b1[0], hy_freq1=hy_freq1[0], hy_ffn_w2=hy_ffn_w2[0],
             hy_ffn_b2=hy_ffn_b2[0], hy_freq2=hy_freq2[0], hy_ffn_w3=hy_ffn_w3[0])

    cc = jnp.zeros((8, d), F32).at[:nb].set(c).at[nb].set(c_ctx)
    mod = _ada(cc, ada_w[0], ada_b[0])
    sh1, sc1, gt1, sh2, sc2, gt2 = [mod[:nb, i * d:(i + 1) * d].reshape(nb, 1, d) for i in range(6)]
    csh1 = mod[nb:nb + 1, 0:d].reshape(1, 1, d)
    csc1 = mod[nb:nb + 1, d:2 * d].reshape(1, 1, d)

    w = w_in[0]
    w_hy = w[:, :OFF_X].astype(BF16)
    w_rest = jnp.concatenate([w[:, OFF_X:OFF_DT], w[:, OFF_Z:]], axis=1).astype(BF16)
    w_dt = w[:, OFF_DT:OFF_Z].astype(BF16)

    m_ctx = _modulate(ctx, csc1, csh1, 256).reshape(nb * lc, d)
    pc = _matmul(m_ctx, w_rest, 512, 1024, slab=True, ncols=OFF_C - OFF_X)
    dtc = _matmul(m_ctx, w_dt, 512, LANES)
    nxb = (OFF_C - OFF_X) // LANES
    cw = ssd_conv_w[0].reshape(3, -1, LANES).transpose(1, 0, 2)
    cbias = ssd_conv_b[0].reshape(-1, 1, LANES)
    xbc_ctx = _conv3(pc, 0, nxb, cw[:nxb], cbias[:nxb], lc, True)
    gw = SSD_HPG * SSD_HEAD_DIM
    zero_state = jnp.zeros((nb, SSD_GROUPS, SSD_STATE, gw), F32)
    dt_bias = ssd_dt_bias[0]
    a_log = ssd_a_log[0]
    xs_sl = SSD_INNER // LANES
    _, s_f = _ssd_scan(xbc_ctx, 0, xs_sl, xs_sl, dtc, dt_bias, a_log, zero_state, lc, False)
    _, s_b = _ssd_scan(xbc_ctx, 0, xs_sl, xs_sl, dtc, dt_bias, a_log, zero_state, lc, True)

    m1 = _modulate(x, sc1, sh1, 256).reshape(t, d)
    proj_hy = _matmul(m1, w_hy, 1024, 1024, slab=True)
    proj = _matmul(m1, w_rest, 1024, 1024, out_dtype=BF16, slab=True)
    dtp = _matmul(m1, w_dt, 1024, LANES)

    hw = hy_conv_w[0].reshape(3, -1, LANES).transpose(1, 0, 2)
    hb = hy_conv_b[0].reshape(-1, 1, LANES)
    nh = HY_WIDTH // LANES
    ug = _hyconv(proj_hy, 3 * nh, hw, hb, seq)
    hspec = _hyena_filter_spectra(p, seq)
    zg = _longconv(ug, ug, 0, nh, hy_skip[0, 0], hspec, 0, seq, False)
    y_hy = _longconv(zg, ug, 0, 2 * nh, hy_skip[0, 1], hspec, 1, seq, True)

    xbc = _conv3(proj, SL_X, SL_Z - SL_X, cw, cbias, seq, True)
    y_f, _ = _ssd_scan(xbc, 0, xs_sl, xs_sl + SSD_GROUPS, dtp, dt_bias, a_log, s_f, seq, False)
    y_b, _ = _ssd_scan(xbc, 0, xs_sl, xs_sl + SSD_GROUPS, dtp, dt_bias, a_log, s_b, seq, True)
    d_ch = jnp.repeat(ssd_d[0, 0] + ssd_d[0, 1], SSD_HEAD_DIM)
    y_ss = _ssd_norm(y_f, y_b, xbc, proj, SL_Z, d_ch, ssd_norm_w[0], 512)

    merged = _branch_merge(y_hy, y_ss, w_branch_hy[0].astype(BF16), w_branch_ssd[0].astype(BF16),
                           proj, SL_GH, SL_GS, 512, 1024)
    h1, m2, m2t = _outproj_ln(merged.reshape(nb, seq, d), w_out[0].astype(BF16), x, gt1, ln1_g[0], ln1_b[0],
                              sc2, sh2, 512)

    m2f = m2.reshape(t, d)
    scores = _peer_scores(m2f, peer_wq[0].astype(BF16), peer_subkeys[0].astype(BF16), 256)
    rk, lim, p1, p2 = _peer_stats(scores, 256)
    te = 1024
    vt = peer_v[0].reshape(PEER_EXPERTS // te, te, d).transpose(0, 2, 1).astype(BF16)
    ffn = _peer_dense(m2t, peer_u[0].astype(BF16), vt, rk, lim, p1, p2, 512, te)
    return _resid_ln(h1, ffn.reshape(nb, seq, d), gt2, ln2_g[0], ln2_b[0], 256)
```

```python
import functools
import math

import numpy as np
import jax
import jax.numpy as jnp
from jax import lax
from jax.experimental import pallas as pl
from jax.experimental.pallas import tpu as pltpu

F32 = jnp.float32
BF16 = jnp.bfloat16

D_MODEL = 2048
DEPTH = 1
CTX_LEN = 256
HY_WIDTH = D_MODEL
HY_BANDS = 16
HY_EMB = 2 * HY_BANDS + 1
HY_FFN = 64
HY_MIN_DECAY = math.log(1e-2) / 1.5
HY_MAX_DECAY = math.log(1e-2) / 0.3
SSD_INNER = 2 * D_MODEL
SSD_HEAD_DIM = 64
SSD_HEADS = SSD_INNER // SSD_HEAD_DIM
SSD_STATE = 128
SSD_GROUPS = 8
SSD_HPG = SSD_HEADS // SSD_GROUPS
SSD_CHUNK = 128
OFF_X = 3 * HY_WIDTH
OFF_B = OFF_X + SSD_INNER
OFF_C = OFF_B + SSD_GROUPS * SSD_STATE
OFF_DT = OFF_C + SSD_GROUPS * SSD_STATE
OFF_Z = OFF_DT + 2 * SSD_HEADS
OFF_GATE = OFF_Z + SSD_INNER
N_COLS = OFF_GATE + 2 * D_MODEL
PEER_HEADS = 8
PEER_NKEYS = 128
PEER_EXPERTS = PEER_NKEYS * PEER_NKEYS
PEER_TOPK = 16
PEER_DKEY = 256
DEEPNORM_ALPHA = (2.0 * DEPTH) ** 0.25

LANES = 128
VMEM_LIMIT = 56 * 1024 * 1024

FFT_N1 = 128
SA_PITCH_PAD = 8
SB_PITCH = FFT_N1 + 8


def _cparams(sem, vmem=VMEM_LIMIT):
    return pltpu.CompilerParams(dimension_semantics=sem, vmem_limit_bytes=vmem)


def _silu(x):
    return x * (1.0 / (1.0 + jnp.exp(-x)))


def _sigmoid(x):
    return 1.0 / (1.0 + jnp.exp(-x))


def _softplus(x):
    return jnp.maximum(x, 0.0) + jnp.log(1.0 + jnp.exp(-jnp.abs(x)))


def _gelu_tanh(x):
    return 0.5 * x * (1.0 + jnp.tanh(math.sqrt(2.0 / math.pi) * (x + 0.044715 * (x * x * x))))


def _normalize_rows(x, eps):
    mu = jnp.mean(x, axis=-1, keepdims=True)
    xc = x - mu
    var = jnp.mean(xc * xc, axis=-1, keepdims=True)
    return xc * lax.rsqrt(var + eps)


def _ada_kernel(c_ref, w_ref, b_ref, o_ref):
    a = _silu(c_ref[...]).astype(BF16)
    o_ref[...] = jnp.dot(a, w_ref[...].astype(BF16), preferred_element_type=F32) + b_ref[...]


def _ada(cc, ada_w, ada_b):
    rows, d = cc.shape
    n = ada_w.shape[1]
    tn = 1024
    return pl.pallas_call(
        _ada_kernel,
        grid=(n // tn,),
        in_specs=[pl.BlockSpec((rows, d), lambda j: (0, 0)),
                  pl.BlockSpec((d, tn), lambda j: (0, j)),
                  pl.BlockSpec((1, tn), lambda j: (0, j))],
        out_specs=pl.BlockSpec((rows, tn), lambda j: (0, j)),
        out_shape=jax.ShapeDtypeStruct((rows, n), F32),
        compiler_params=_cparams(("arbitrary",)),
    )(cc, ada_w, ada_b.reshape(1, n))


def _mod_kernel(x_ref, sc_ref, sh_ref, o_ref):
    xn = _normalize_rows(x_ref[0], 1e-6)
    o_ref[0] = (xn * (1.0 + sc_ref[0]) + sh_ref[0]).astype(o_ref.dtype)


def _modulate(x, scale, shift, tr):
    b, l, d = x.shape
    per_batch = scale.shape[0] == b
    smap = (lambda i, j: (i, 0, 0)) if per_batch else (lambda i, j: (0, 0, 0))
    return pl.pallas_call(
        _mod_kernel,
        grid=(b, l // tr),
        in_specs=[pl.BlockSpec((1, tr, d), lambda i, j: (i, j, 0)),
                  pl.BlockSpec((1, 1, d), smap),
                  pl.BlockSpec((1, 1, d), smap)],
        out_specs=pl.BlockSpec((1, tr, d), lambda i, j: (i, j, 0)),
        out_shape=jax.ShapeDtypeStruct((b, l, d), BF16),
        compiler_params=_cparams(("arbitrary", "arbitrary")),
    )(x, scale, shift)


def _mm_kernel(a_ref, b_ref, o_ref, *, slab):
    acc = jnp.dot(a_ref[...], b_ref[...], preferred_element_type=F32)
    if slab:
        for s in range(o_ref.shape[0]):
            o_ref[s] = acc[:, s * LANES:(s + 1) * LANES].astype(o_ref.dtype)
    else:
        o_ref[...] = acc.astype(o_ref.dtype)


def _matmul(a, b, tm, tn, out_dtype=F32, slab=False, ncols=None):
    m, k = a.shape
    n = b.shape[1] if ncols is None else ncols
    if slab:
        out_shape = jax.ShapeDtypeStruct((n // LANES, m, LANES), out_dtype)
        out_spec = pl.BlockSpec((tn // LANES, tm, LANES), lambda i, j: (j, i, 0))
    else:
        out_shape = jax.ShapeDtypeStruct((m, n), out_dtype)
        out_spec = pl.BlockSpec((tm, tn), lambda i, j: (i, j))
    return pl.pallas_call(
        functools.partial(_mm_kernel, slab=slab),
        grid=(m // tm, n // tn),
        in_specs=[pl.BlockSpec((tm, k), lambda i, j: (i, 0)),
                  pl.BlockSpec((k, tn), lambda i, j: (0, j))],
        out_specs=out_spec,
        out_shape=out_shape,
        compiler_params=_cparams(("arbitrary", "arbitrary")),
    )(a, b)


def _conv3_kernel(x_ref, w_ref, b_ref, o_ref, *, silu):
    x = x_ref[0].astype(F32)
    l = x.shape[0]
    row = lax.broadcasted_iota(jnp.int32, x.shape, 0)
    prev = jnp.where(row == 0, 0.0, pltpu.roll(x, 1, 0))
    nxt = jnp.where(row == l - 1, 0.0, pltpu.roll(x, l - 1, 0))
    w = w_ref[0]
    y = prev * w[0:1] + x * w[1:2] + nxt * w[2:3] + b_ref[0]
    if silu:
        y = _silu(y)
    o_ref[0] = y.astype(o_ref.dtype)


def _conv3(p_slab, slab0, nslab, w, bias, seq, silu):
    rows = p_slab.shape[1]
    nb = rows // seq
    return pl.pallas_call(
        functools.partial(_conv3_kernel, silu=silu),
        grid=(nslab, nb),
        in_specs=[pl.BlockSpec((1, seq, LANES), lambda s, b: (slab0 + s, b, 0)),
                  pl.BlockSpec((1, 3, LANES), lambda s, b: (s, 0, 0)),
                  pl.BlockSpec((1, 1, LANES), lambda s, b: (s, 0, 0))],
        out_specs=pl.BlockSpec((1, seq, LANES), lambda s, b: (s, b, 0)),
        out_shape=jax.ShapeDtypeStruct((nslab, rows, LANES), BF16),
        compiler_params=_cparams(("arbitrary", "arbitrary")),
    )(p_slab, w, bias)


def _hyconv_kernel(x_ref, w_ref, b_ref, o_ref, *, seq, nb):
    nj = seq // FFT_N1

    def gather(n1):
        return jnp.concatenate(
            [x_ref[0, pl.ds(n1 + b * seq, nj, stride=FFT_N1), :] for b in range(nb)], axis=0)

    w = w_ref[0]
    w0, w1, w2, bias = w[0:1], w[1:2], w[2:3], b_ref[0]
    jrow = lax.broadcasted_iota(jnp.int32, (nb * nj, LANES), 0) % nj

    def shift_down(g):
        return jnp.where(jrow == 0, 0.0, pltpu.roll(g, 1, 0))

    def shift_up(g):
        return jnp.where(jrow == nj - 1, 0.0, pltpu.roll(g, nb * nj - 1, 0))

    g_first = gather(0)
    g_second = gather(1)
    g_last = gather(FFT_N1 - 1)
    dt = o_ref.dtype
    o_ref[0, 0] = (shift_down(g_last) * w0 + g_first * w1 + g_second * w2 + bias).astype(dt)

    def body(n1, carry):
        g_prev, g_cur = carry
        g_next = gather(n1 + 1)
        o_ref[0, n1] = (g_prev * w0 + g_cur * w1 + g_next * w2 + bias).astype(dt)
        return g_cur, g_next

    g_prev, g_cur = lax.fori_loop(1, FFT_N1 - 1, body, (g_first, g_second), unroll=6)
    o_ref[0, FFT_N1 - 1] = (g_prev * w0 + g_cur * w1 + shift_up(g_first) * w2 + bias).astype(dt)


def _hyconv(p_slab, nslab, w, bias, seq):
    rows = p_slab.shape[1]
    nb = rows // seq
    nj = seq // FFT_N1
    return pl.pallas_call(
        functools.partial(_hyconv_kernel, seq=seq, nb=nb),
        grid=(nslab,),
        in_specs=[pl.BlockSpec((1, rows, LANES), lambda s: (s, 0, 0)),
                  pl.BlockSpec((1, 3, LANES), lambda s: (s, 0, 0)),
                  pl.BlockSpec((1, 1, LANES), lambda s: (s, 0, 0))],
        out_specs=pl.BlockSpec((1, FFT_N1, nb * nj, LANES), lambda s: (s, 0, 0, 0)),
        out_shape=jax.ShapeDtypeStruct((nslab, FFT_N1, nb * nj, LANES), BF16),
        compiler_params=_cparams(("arbitrary",)),
    )(p_slab, w, bias)


@functools.lru_cache(maxsize=None)
def _dft_tables(seq):
    nj = seq // FFT_N1
    n2 = nj + nj // 2
    n = FFT_N1 * n2
    out0 = nj // 2
    n1 = np.arange(FFT_N1, dtype=np.float64)
    k2 = np.arange(n2, dtype=np.float64)
    jn = np.arange(nj, dtype=np.float64)
    ang = -2.0 * np.pi * (k2[None, :, None] * jn[None, None, :] / n2 + n1[:, None, None] * k2[None, :, None] / n)
    cr, ci = np.cos(ang), np.sin(ang)
    fa = np.concatenate([np.concatenate([cr, -ci], axis=2), np.concatenate([ci, cr], axis=2)], axis=1)
    k1 = np.arange(FFT_N1, dtype=np.float64)
    angb = -2.0 * np.pi * np.outer(k1, n1) / FFT_N1
    dr, di = np.cos(angb), np.sin(angb)
    fb = np.block([[dr, -di], [di, dr]])
    fbi = np.block([[dr, di], [-di, dr]])
    angi = 2.0 * np.pi * (n1[:, None, None] * k2[None, None, :] / n + (out0 + jn)[None, :, None] * k2[None, None, :] / n2)
    er, ei = np.cos(angi) / n, np.sin(angi) / n
    fai = np.concatenate([np.concatenate([er, -ei], axis=2), np.concatenate([ei, er], axis=2)], axis=1)
    return (fa.astype(np.float32), fb.astype(np.float32), fbi.astype(np.float32), fai.astype(np.float32), n2)


@functools.lru_cache(maxsize=None)
def _filter_consts(seq):
    t = np.linspace(0.0, 1.0, seq, dtype=np.float32)[:, None].astype(np.float64)
    w = 2.0 * np.pi * np.arange(seq, dtype=np.float32)[:, None].astype(np.float64) / seq
    bands = np.linspace(1e-4, HY_BANDS - 1, HY_BANDS, dtype=np.float32)[None, :].astype(np.float64)
    feats = np.concatenate([t, np.cos(bands * w), -np.sin(bands * w)], axis=-1)
    feats_p = np.zeros((seq, LANES), np.float32)
    feats_p[:, :HY_EMB] = feats
    dist = (np.abs(np.arange(seq) - seq // 2).astype(np.float32) / np.float32(seq / 2.0)).reshape(seq, 1)
    deltas = np.abs(np.linspace(HY_MIN_DECAY, HY_MAX_DECAY, HY_WIDTH, dtype=np.float32))
    return feats_p, dist.astype(np.float32), deltas.astype(np.float32)


def _dot_f32(a, b):
    def split(x):
        hi = x.astype(BF16)
        r1 = x - hi.astype(F32)
        mid = r1.astype(BF16)
        lo = (r1 - mid.astype(F32)).astype(BF16)
        return hi, mid, lo
    a0, a1, a2 = split(a)
    b0, b1, b2 = split(b)
    d = functools.partial(jnp.dot, preferred_element_type=F32)
    return (d(a0, b0) + (d(a0, b1) + d(a1, b0)) + (d(a0, b2) + d(a1, b1) + d(a2, b0)))


def _filter_kernel(feats_ref, w1_ref, b1_ref, f1_ref, w2_ref, b2_ref, f2_ref, w3_ref, dist_ref, delta_ref,
                   fa_ref, fb_ref, o_ref, hid_ref, h_ref, sar_ref, sai_ref, *, seq, n2, pitch):
    nj = seq // FFT_N1

    @pl.when((pl.program_id(0) == 0) & (pl.program_id(1) == 0))
    def _():
        h1 = jnp.sin(f1_ref[...] * (_dot_f32(feats_ref[...], w1_ref[...]) + b1_ref[...]))
        hid_ref[...] = jnp.sin(f2_ref[...] * (_dot_f32(h1, w2_ref[...]) + b2_ref[...])).astype(BF16)

    ns = h_ref.shape[0]
    lanes = lambda f: jnp.concatenate([f(s) for s in range(ns)], axis=1)
    w3 = lanes(lambda s: w3_ref[0, s]).astype(BF16)
    h = jnp.dot(hid_ref[...], w3, preferred_element_type=F32)
    h = h * jnp.exp(-dist_ref[...] * lanes(lambda s: delta_ref[s]))
    h = h / (jnp.sum(jnp.abs(h), axis=0, keepdims=True) + 1e-6)
    for s in range(ns):
        h_ref[s] = h[:, s * LANES:(s + 1) * LANES]

    def stage_a(n1, c):
        g = lanes(lambda s: h_ref[s, pl.ds(n1, nj, stride=FFT_N1), :]).astype(BF16)
        y = jnp.dot(fa_ref[n1], g, preferred_element_type=F32)
        base = pl.multiple_of(n1 * pitch, 8)
        for s in range(ns):
            sar_ref[s, pl.ds(base, n2), :] = y[:n2, s * LANES:(s + 1) * LANES]
            sai_ref[s, pl.ds(base, n2), :] = y[n2:, s * LANES:(s + 1) * LANES]
        return c

    lax.fori_loop(0, FFT_N1, stage_a, 0, unroll=8)

    def stage_b(k2, c):
        yr = lanes(lambda s: sar_ref[s, pl.ds(k2, FFT_N1, stride=pitch), :])
        yi = lanes(lambda s: sai_ref[s, pl.ds(k2, FFT_N1, stride=pitch), :])
        op = jnp.concatenate([yr, yi], axis=0).astype(BF16)
        res = jnp.dot(fb_ref[...], op, preferred_element_type=F32).astype(o_ref.dtype)
        for s in range(ns):
            o_ref[0, s, k2] = res[:, s * LANES:(s + 1) * LANES]
        return c

    lax.fori_loop(0, n2, stage_b, 0, unroll=4)


def _hyena_filter_spectra(p, seq):
    fa, fb, _, _, n2 = _dft_tables(seq)
    nj = seq // FFT_N1
    pitch = n2 + SA_PITCH_PAD
    feats, dist, deltas = _filter_consts(seq)
    nsl = HY_WIDTH // LANES

    def padk(w, rows):
        return jnp.zeros((rows, w.shape[1]), F32).at[:w.shape[0]].set(w)

    def padn(v):
        return jnp.zeros((1, LANES), F32).at[0, :v.shape[0]].set(v)

    w1 = jnp.zeros((LANES, LANES), F32).at[:HY_EMB, :HY_FFN].set(p['hy_ffn_w1'])
    w2 = jnp.zeros((LANES, LANES), F32).at[:HY_FFN, :HY_FFN].set(p['hy_ffn_w2'])
    w3 = padk(p['hy_ffn_w3'], LANES).reshape(LANES, 2, nsl, LANES).transpose(1, 2, 0, 3)
    fa_real = jnp.asarray(fa[:, :, :nj]).astype(BF16)
    const = lambda shape: pl.BlockSpec(shape, lambda o, s: (0,) * len(shape))
    ns = 2
    return pl.pallas_call(
        functools.partial(_filter_kernel, seq=seq, n2=n2, pitch=pitch),
        grid=(2, nsl // ns),
        in_specs=[const((seq, LANES)), const((LANES, LANES)), const((1, LANES)), const((1, LANES)),
                  const((LANES, LANES)), const((1, LANES)), const((1, LANES)),
                  pl.BlockSpec((1, ns, LANES, LANES), lambda o, s: (o, s, 0, 0)),
                  const((seq, 1)),
                  pl.BlockSpec((ns, 1, LANES), lambda o, s: (s, 0, 0)),
                  const((FFT_N1, 2 * n2, nj)), const((2 * FFT_N1, 2 * FFT_N1))],
        out_specs=pl.BlockSpec((1, ns, n2, 2 * FFT_N1, LANES), lambda o, s: (o, s, 0, 0, 0)),
        out_shape=jax.ShapeDtypeStruct((2, nsl, n2, 2 * FFT_N1, LANES), BF16),
        scratch_shapes=[pltpu.VMEM((seq, LANES), BF16), pltpu.VMEM((ns, seq, LANES), F32),
                        pltpu.VMEM((ns, FFT_N1 * pitch, LANES), F32),
                        pltpu.VMEM((ns, FFT_N1 * pitch, LANES), F32)],
        compiler_params=_cparams(("arbitrary", "arbitrary")),
    )(jnp.asarray(feats), w1, padn(p['hy_ffn_b1']), padn(p['hy_freq1']), w2, padn(p['hy_ffn_b2']),
      padn(p['hy_freq2']), w3, jnp.asarray(dist), jnp.asarray(deltas).reshape(nsl, 1, LANES),
      fa_real, jnp.asarray(fb).astype(BF16))


def _longconv_kernel(x_ref, g_ref, skip_ref, h_ref, fa_ref, fb_ref, fbi_ref, fai_ref, o_ref,
                     sar_ref, sai_ref, sbr_ref, sbi_ref, *, seq, n2, pitch, natural_out):
    nj = seq // FFT_N1

    def stage_a(n1, c):
        y = jnp.dot(fa_ref[n1], x_ref[0, n1].astype(BF16), preferred_element_type=F32)
        base = pl.multiple_of(n1 * pitch, 8)
        sar_ref[pl.ds(base, n2), :] = y[:n2]
        sai_ref[pl.ds(base, n2), :] = y[n2:]
        return c

    lax.fori_loop(0, FFT_N1, stage_a, 0, unroll=8)

    def stage_b(kp, c):
        ks = (2 * kp, 2 * kp + 1)
        yr = jnp.concatenate([sar_ref[pl.ds(k, FFT_N1, stride=pitch), :] for k in ks], axis=1)
        yi = jnp.concatenate([sai_ref[pl.ds(k, FFT_N1, stride=pitch), :] for k in ks], axis=1)
        xf = jnp.dot(fb_ref[...], jnp.concatenate([yr, yi], axis=0).astype(BF16), preferred_element_type=F32)
        xr, xi = xf[:FFT_N1], xf[FFT_N1:]
        hf = jnp.concatenate([h_ref[0, k] for k in ks], axis=1).astype(F32)
        hr, hi = hf[:FFT_N1], hf[FFT_N1:]
        z = jnp.concatenate([xr * hr - xi * hi, xr * hi + xi * hr], axis=0).astype(BF16)
        w = jnp.dot(fbi_ref[...], z, preferred_element_type=F32)
        for q, k in enumerate(ks):
            base = pl.multiple_of(k * SB_PITCH, 8)
            sbr_ref[pl.ds(base, FFT_N1), :] = w[:FFT_N1, q * LANES:(q + 1) * LANES]
            sbi_ref[pl.ds(base, FFT_N1), :] = w[FFT_N1:, q * LANES:(q + 1) * LANES]
        return c

    lax.fori_loop(0, n2 // 2, stage_b, 0, unroll=2)

    skip = skip_ref[0]

    def stage_c(n1, c):
        wr = sbr_ref[pl.ds(n1, n2, stride=SB_PITCH), :]
        wi = sbi_ref[pl.ds(n1, n2, stride=SB_PITCH), :]
        y = jnp.dot(fai_ref[n1], jnp.concatenate([wr, wi], axis=0).astype(BF16), preferred_element_type=F32)
        res = g_ref[0, n1].astype(F32) * (y + x_ref[0, n1].astype(F32) * skip)
        if natural_out:
            for b in range(res.shape[0] // nj):
                o_ref[0, pl.ds(n1 + b * seq, nj, stride=FFT_N1), :] = res[b * nj:(b + 1) * nj]
        else:
            o_ref[0, n1] = res.astype(o_ref.dtype)
        return c

    lax.fori_loop(0, FFT_N1, stage_c, 0, unroll=8)


def _longconv(xg, gg, slab_x, slab_g, skip, hspec, order, seq, natural_out):
    fa, fb, fbi, fai, n2 = _dft_tables(seq)
    pitch = n2 + SA_PITCH_PAD
    nsl = HY_WIDTH // LANES
    rows = xg.shape[2]
    nb = rows // (seq // FFT_N1)
    const = lambda shape: pl.BlockSpec(shape, lambda s: (0,) * len(shape))
    if natural_out:
        out_shape = jax.ShapeDtypeStruct((nsl, nb * seq, LANES), F32)
        out_spec = pl.BlockSpec((1, nb * seq, LANES), lambda s: (s, 0, 0))
    else:
        out_shape = jax.ShapeDtypeStruct((nsl, FFT_N1, rows, LANES), BF16)
        out_spec = pl.BlockSpec((1, FFT_N1, rows, LANES), lambda s: (s, 0, 0, 0))
    return pl.pallas_call(
        functools.partial(_longconv_kernel, seq=seq, n2=n2, pitch=pitch, natural_out=natural_out),
        grid=(nsl,),
        in_specs=[pl.BlockSpec((1, FFT_N1, rows, LANES), lambda s: (slab_x + s, 0, 0, 0)),
                  pl.BlockSpec((1, FFT_N1, rows, LANES), lambda s: (slab_g + s, 0, 0, 0)),
                  pl.BlockSpec((1, 1, LANES), lambda s: (s, 0, 0)),
                  pl.BlockSpec((None, 1, n2, 2 * FFT_N1, LANES), lambda s: (order, s, 0, 0, 0)),
                  const(fa.shape), const(fb.shape), const(fbi.shape), const(fai.shape)],
        out_specs=out_spec,
        out_shape=out_shape,
        scratch_shapes=[pltpu.VMEM((FFT_N1 * pitch, LANES), F32), pltpu.VMEM((FFT_N1 * pitch, LANES), F32),
                        pltpu.VMEM((n2 * SB_PITCH, LANES), F32), pltpu.VMEM((n2 * SB_PITCH, LANES), F32)],
        compiler_params=_cparams(("arbitrary",)),
    )(xg, gg, skip.reshape(nsl, 1, LANES), hspec,
      jnp.asarray(fa).astype(BF16), jnp.asarray(fb).astype(BF16),
      jnp.asarray(fbi).astype(BF16), jnp.asarray(fai).astype(BF16))


def _dot_exact_lhs(lhs_bf16, x):
    hi = x.astype(BF16)
    r1 = x - hi.astype(F32)
    mid = r1.astype(BF16)
    lo = (r1 - mid.astype(F32)).astype(BF16)
    d = functools.partial(jnp.dot, preferred_element_type=F32)
    return d(lhs_bf16, hi) + (d(lhs_bf16, mid) + d(lhs_bf16, lo))


def _ssd_chunk_groups(x_ref, b_ref, c_ref, dt_ref, bias_ref, alog_ref, e_ref, y_ref, st_ref, rev):
    gw = SSD_HPG * SSD_HEAD_DIM
    dot = functools.partial(jnp.dot, preferred_element_type=F32)
    off = SSD_HEADS if rev else 0
    dt_all = _softplus(dt_ref[...] + bias_ref[...])
    a_all = dt_all * (-jnp.exp(alog_ref[...]))
    r_i = lax.broadcasted_iota(jnp.int32, (SSD_CHUNK, SSD_CHUNK), 0)
    c_i = lax.broadcasted_iota(jnp.int32, (SSD_CHUNK, SSD_CHUNK), 1)
    valid = (c_i >= r_i) if rev else (c_i <= r_i)
    tri = jnp.where(valid, 1.0, 0.0).astype(BF16)
    acum = _dot_exact_lhs(tri, a_all)
    src_t = (acum - jnp.log(dt_all)).T
    tot = acum[0:1, :] if rev else acum[SSD_CHUNK - 1:SSD_CHUNK, :]
    dtw_b = (dt_all * jnp.exp(tot - acum)).astype(BF16)
    cd8 = jnp.broadcast_to(jnp.exp(tot), (8, LANES))
    cd_hi = cd8.astype(BF16)
    cd_r = cd8 - cd_hi.astype(F32)
    cd_mid = cd_r.astype(BF16)
    cd_lo = (cd_r - cd_mid.astype(F32)).astype(BF16)
    lo_half = c_i < SSD_HEAD_DIM
    zero_b = jnp.zeros((), BF16)

    def group(g):
        bt = b_ref[g].astype(F32).T.astype(BF16)
        cg = c_ref[g].astype(BF16)
        cb = dot(cg, bt)
        for sl in range(gw // LANES):
            col = slice(sl * LANES, (sl + 1) * LANES)
            e_s = e_ref[:, g * gw + sl * LANES:g * gw + (sl + 1) * LANES]
            xb = x_ref[(gw // LANES) * g + sl]
            ms, ea = [], []
            for hh in range(2):
                h = off + SSD_HPG * g + 2 * sl + hh
                a_col = jnp.broadcast_to(acum[:, h:h + 1], (SSD_CHUNK, SSD_CHUNK))
                seg = jnp.where(valid, a_col - src_t[h:h + 1, :], -1e30)
                ms.append((cb * jnp.exp(seg)).astype(BF16))
                ea.append(jnp.exp(a_col))
            rhs = jnp.concatenate([jnp.where(lo_half, xb, zero_b), jnp.where(lo_half, zero_b, xb)], axis=0)
            yd = dot(jnp.concatenate(ms, axis=1), rhs)
            sgs = st_ref[g, :, col]
            yoff = dot(cg, sgs.astype(BF16)) * jnp.where(lo_half, ea[0], ea[1])
            y_ref[g, :, col] = (yd + yoff).astype(y_ref.dtype)
            wm = (xb.astype(F32) * dot(dtw_b, e_s)).astype(BF16)
            cd = (dot(cd_hi, e_s) + (dot(cd_mid, e_s) + dot(cd_lo, e_s)))[0:1]
            st_ref[g, :, col] = sgs * cd + dot(bt, wm)

    return group


def _ssd_kernel(xf_ref, bf_ref, cf_ref, dtf_ref, xr_ref, br_ref, cr_ref, dtr_ref, bias_ref, alog_ref, ef_ref, er_ref,
                initf_ref, initr_ref, yf_ref, yr_ref, finf_ref, finr_ref, stf_ref, str_ref, *, nc):
    ci = pl.program_id(1)

    @pl.when(ci == 0)
    def _():
        stf_ref[...] = initf_ref[0]
        str_ref[...] = initr_ref[0]

    fwd = _ssd_chunk_groups(xf_ref, bf_ref, cf_ref, dtf_ref, bias_ref, alog_ref, ef_ref, yf_ref, stf_ref, False)
    bwd = _ssd_chunk_groups(xr_ref, br_ref, cr_ref, dtr_ref, bias_ref, alog_ref, er_ref, yr_ref, str_ref, True)
    for g in range(SSD_GROUPS):
        fwd(g)
        bwd(g)

    @pl.when(ci == nc - 1)
    def _():
        finf_ref[0] = stf_ref[...]
        finr_ref[0] = str_ref[...]


def _ssd_scan(xbc, slab_x, slab_b, slab_c, dt, dt_bias, a_log, init_f, init_r, seq):
    rows = xbc.shape[1]
    nb = rows // seq
    nc = seq // SSD_CHUNK
    gw = SSD_HPG * SSD_HEAD_DIM
    xs = SSD_INNER // LANES
    pos_f = lambda b, c: b * nc + c
    pos_r = lambda b, c: b * nc + (nc - 1 - c)
    expand = np.zeros((2, 2 * SSD_HEADS, SSD_INNER), np.float32)
    for r in range(2):
        for h in range(SSD_HEADS):
            expand[r, r * SSD_HEADS + h, h * SSD_HEAD_DIM:(h + 1) * SSD_HEAD_DIM] = 1.0

    def data_specs(pos):
        return [pl.BlockSpec((xs, SSD_CHUNK, LANES), lambda b, c: (slab_x // xs, pos(b, c), 0)),
                pl.BlockSpec((SSD_GROUPS, SSD_CHUNK, LANES), lambda b, c: (slab_b // SSD_GROUPS, pos(b, c), 0)),
                pl.BlockSpec((SSD_GROUPS, SSD_CHUNK, LANES), lambda b, c: (slab_c // SSD_GROUPS, pos(b, c), 0)),
                pl.BlockSpec((SSD_CHUNK, LANES), lambda b, c: (pos(b, c), 0))]

    vec = pl.BlockSpec((1, LANES), lambda b, c: (0, 0))
    emat = pl.BlockSpec((2 * SSD_HEADS, SSD_INNER), lambda b, c: (0, 0))
    state = pl.BlockSpec((1, SSD_GROUPS, SSD_STATE, gw), lambda b, c: (b, 0, 0, 0))
    y_sds = jax.ShapeDtypeStruct((SSD_GROUPS, rows, gw), BF16)
    s_sds = jax.ShapeDtypeStruct((nb, SSD_GROUPS, SSD_STATE, gw), F32)
    e_b = jnp.asarray(expand).astype(BF16)
    return pl.pallas_call(
        functools.partial(_ssd_kernel, nc=nc),
        grid=(nb, nc),
        in_specs=data_specs(pos_f) + data_specs(pos_r) + [vec, vec, emat, emat, state, state],
        out_specs=[pl.BlockSpec((SSD_GROUPS, SSD_CHUNK, gw), lambda b, c: (0, pos_f(b, c), 0)),
                   pl.BlockSpec((SSD_GROUPS, SSD_CHUNK, gw), lambda b, c: (0, pos_r(b, c), 0)),
                   state, state],
        out_shape=[y_sds, y_sds, s_sds, s_sds],
        scratch_shapes=[pltpu.VMEM((SSD_GROUPS, SSD_STATE, gw), F32), pltpu.VMEM((SSD_GROUPS, SSD_STATE, gw), F32)],
        compiler_params=_cparams(("arbitrary", "arbitrary")),
    )(xbc, xbc, xbc, dt, xbc, xbc, xbc, dt, dt_bias.reshape(1, LANES), a_log.reshape(1, LANES),
      e_b[0], e_b[1], init_f, init_r)


def _ssdnorm_kernel(yf_ref, yb_ref, x_ref, z_ref, d_ref, w_ref, o_ref):
    nsl = x_ref.shape[0]
    x = jnp.concatenate([x_ref[i] for i in range(nsl)], axis=1).astype(F32)
    z = jnp.concatenate([z_ref[i] for i in range(nsl)], axis=1).astype(F32)
    y = yf_ref[0].astype(F32) + yb_ref[0].astype(F32) + x * d_ref[0]
    gx = y * _silu(z)
    ms = jnp.mean(gx * gx, axis=-1, keepdims=True)
    o_ref[...] = (gx * lax.rsqrt(ms + 1e-5) * w_ref[0]).astype(o_ref.dtype)


def _ssd_norm(yf, yb, xbc, proj, slab_z, d_ch, w, tr):
    g, rows, gw = yf.shape
    nsl = gw // LANES
    return pl.pallas_call(
        _ssdnorm_kernel,
        grid=(g, rows // tr),
        in_specs=[pl.BlockSpec((1, tr, gw), lambda q, i: (q, i, 0)),
                  pl.BlockSpec((1, tr, gw), lambda q, i: (q, i, 0)),
                  pl.BlockSpec((nsl, tr, LANES), lambda q, i: (q, i, 0)),
                  pl.BlockSpec((nsl, tr, LANES), lambda q, i: (slab_z // nsl + q, i, 0)),
                  pl.BlockSpec((1, 1, gw), lambda q, i: (q, 0, 0)),
                  pl.BlockSpec((1, 1, gw), lambda q, i: (q, 0, 0))],
        out_specs=pl.BlockSpec((tr, gw), lambda q, i: (i, q)),
        out_shape=jax.ShapeDtypeStruct((rows, g * gw), BF16),
        compiler_params=_cparams(("arbitrary", "arbitrary")),
    )(yf, yb, xbc, proj, d_ch.reshape(g, 1, gw), w.reshape(g, 1, gw))


def _merge_kernel(yhy_ref, yss_ref, whh_ref, wss_ref, gh_ref, gs_ref, o_ref):
    a1 = jnp.concatenate([yhy_ref[i] for i in range(yhy_ref.shape[0])], axis=1).astype(BF16)
    p1 = jnp.dot(a1, whh_ref[...], preferred_element_type=F32)
    p2 = jnp.dot(yss_ref[...], wss_ref[...], preferred_element_type=F32)
    gh = jnp.concatenate([gh_ref[i] for i in range(gh_ref.shape[0])], axis=1).astype(F32)
    gs = jnp.concatenate([gs_ref[i] for i in range(gs_ref.shape[0])], axis=1).astype(F32)
    o_ref[...] = (_sigmoid(gh) * p1 + _sigmoid(gs) * p2).astype(o_ref.dtype)


def _branch_merge(yhy, yss, w_hy, w_ssd, proj, slab_gh, slab_gs, tm, tn):
    nsl_k = yhy.shape[0]
    rows = yhy.shape[1]
    n = w_hy.shape[1]
    ts = tn // LANES
    return pl.pallas_call(
        _merge_kernel,
        grid=(rows // tm, n // tn),
        in_specs=[pl.BlockSpec((nsl_k, tm, LANES), lambda i, j: (0, i, 0)),
                  pl.BlockSpec((tm, yss.shape[1]), lambda i, j: (i, 0)),
                  pl.BlockSpec((w_hy.shape[0], tn), lambda i, j: (0, j)),
                  pl.BlockSpec((w_ssd.shape[0], tn), lambda i, j: (0, j)),
                  pl.BlockSpec((ts, tm, LANES), lambda i, j: (slab_gh // ts + j, i, 0)),
                  pl.BlockSpec((ts, tm, LANES), lambda i, j: (slab_gs // ts + j, i, 0))],
        out_specs=pl.BlockSpec((tm, tn), lambda i, j: (i, j)),
        out_shape=jax.ShapeDtypeStruct((rows, n), BF16),
        compiler_params=_cparams(("arbitrary", "arbitrary")),
    )(yhy, yss, w_hy, w_ssd, proj, proj)


def _outproj_ln_kernel(a_ref, w_ref, x_ref, gt_ref, g_ref, b_ref, sc_ref, sh_ref, h_ref, m_ref, mt_ref):
    y = jnp.dot(a_ref[0], w_ref[...], preferred_element_type=F32)
    h = _normalize_rows(DEEPNORM_ALPHA * x_ref[0] + gt_ref[0] * y, 1e-6) * g_ref[...] + b_ref[...]
    h_ref[0] = h
    m = _normalize_rows(h, 1e-6) * (1.0 + sc_ref[0]) + sh_ref[0]
    m_ref[0] = m.astype(m_ref.dtype)
    mt_ref[...] = m.T.astype(mt_ref.dtype)


def _outproj_ln(merged, w_out, x, gate, ln_g, ln_b, scale2, shift2, tm):
    b, l, d = x.shape
    nt = l // tm
    vec = pl.BlockSpec((1, 1, d), lambda i, j: (i, 0, 0))
    cvec = pl.BlockSpec((1, d), lambda i, j: (0, 0))
    tile = pl.BlockSpec((1, tm, d), lambda i, j: (i, j, 0))
    return pl.pallas_call(
        _outproj_ln_kernel,
        grid=(b, nt),
        in_specs=[tile, pl.BlockSpec((d, d), lambda i, j: (0, 0)), tile, vec, cvec, cvec, vec, vec],
        out_specs=[tile, tile, pl.BlockSpec((d, tm), lambda i, j: (0, i * nt + j))],
        out_shape=[jax.ShapeDtypeStruct((b, l, d), F32), jax.ShapeDtypeStruct((b, l, d), BF16),
                   jax.ShapeDtypeStruct((d, b * l), BF16)],
        compiler_params=_cparams(("arbitrary", "arbitrary")),
    )(merged, w_out, x, gate, ln_g.reshape(1, d), ln_b.reshape(1, d), scale2, shift2)


def _resid_ln_kernel(x_ref, y_ref, gt_ref, g_ref, b_ref, o_ref):
    o_ref[0] = _normalize_rows(DEEPNORM_ALPHA * x_ref[0] + gt_ref[0] * y_ref[0], 1e-6) * g_ref[...] + b_ref[...]


def _resid_ln(x, y, gate, ln_g, ln_b, tm):
    b, l, d = x.shape
    tile = pl.BlockSpec((1, tm, d), lambda i, j: (i, j, 0))
    return pl.pallas_call(
        _resid_ln_kernel,
        grid=(b, l // tm),
        in_specs=[tile, tile, pl.BlockSpec((1, 1, d), lambda i, j: (i, 0, 0)),
                  pl.BlockSpec((1, d), lambda i, j: (0, 0)), pl.BlockSpec((1, d), lambda i, j: (0, 0))],
        out_specs=tile,
        out_shape=jax.ShapeDtypeStruct((b, l, d), F32),
        compiler_params=_cparams(("arbitrary", "arbitrary")),
    )(x, y, gate, ln_g.reshape(1, d), ln_b.reshape(1, d))


def _peer_scores_kernel(a_ref, wq_ref, k_ref, s_ref):
    q = jnp.dot(a_ref[...], wq_ref[...], preferred_element_type=F32)
    hd = PEER_DKEY // 2
    nt = (((1,), (1,)), ((), ()))
    for h in range(PEER_HEADS):
        qn = _normalize_rows(q[:, h * PEER_DKEY:(h + 1) * PEER_DKEY], 1e-6).astype(BF16)
        s_ref[h, 0] = lax.dot_general(k_ref[h, 0], qn[:, :hd], nt, preferred_element_type=F32)
        s_ref[h, 1] = lax.dot_general(k_ref[h, 1], qn[:, hd:], nt, preferred_element_type=F32)


def _peer_scores(m2, wq, subkeys, tm):
    t, d = m2.shape
    return pl.pallas_call(
        _peer_scores_kernel,
        grid=(t // tm,),
        in_specs=[pl.BlockSpec((tm, d), lambda i: (i, 0)),
                  pl.BlockSpec(wq.shape, lambda i: (0, 0)),
                  pl.BlockSpec(subkeys.shape, lambda i: (0, 0, 0, 0))],
        out_specs=pl.BlockSpec((PEER_HEADS, 2, PEER_NKEYS, tm), lambda i: (0, 0, 0, i)),
        out_shape=jax.ShapeDtypeStruct((PEER_HEADS, 2, PEER_NKEYS, t), F32),
        compiler_params=_cparams(("arbitrary",)),
    )(m2, wq, subkeys)


def _top_values(x, k):
    out = []
    for _ in range(k):
        m = jnp.max(x, axis=0, keepdims=True)
        out.append(m)
        x = jnp.where(x == m, -jnp.inf, x)
    return out


def _top_values_ranked(x, k):
    out = []
    big = 2.0 ** 100
    for a in range(k):
        m = jnp.max(x, axis=0, keepdims=True)
        out.append(m)
        x = jnp.where(x == m, -big * (1.0 + a / 64.0), x)
    rank = jnp.where(x <= -big, (x * (-1.0 / big) - 1.0) * 64.0, float(k))
    return out, rank


def _peer_stats_kernel(s_ref, rk_ref, lim_ref, p1_ref, p2_ref):
    k = PEER_TOPK
    for h in range(PEER_HEADS):
        s1 = s_ref[h, 0]
        s2 = s_ref[h, 1]
        v1, r1 = _top_values_ranked(s1, k)
        v2, rk = _top_values_ranked(s2, k)
        v2m = jnp.concatenate(v2, axis=0)
        grid = [v1[a] + v2m[:k // (a + 1)] for a in range(k)]
        n_cand = sum(g.shape[0] for g in grid)
        pad = jnp.full((-n_cand % 8, s1.shape[1]), -jnp.inf, F32)
        best = _top_values(jnp.concatenate(grid + [pad], axis=0), k)
        z = best[0] * 0.0
        for c in best:
            z = z + jnp.exp(c - best[0])
        lim = jnp.zeros_like(s1)
        for a in range(k):
            n_sel = jnp.sum(jnp.where(grid[a] >= best[k - 1], 1.0, 0.0), axis=0, keepdims=True)
            lim = jnp.where(r1 == float(a), n_sel, lim)
        lim_ref[h] = lim
        rk_ref[h] = rk.astype(rk_ref.dtype)
        p1_ref[h] = jnp.exp(s1 - v1[0])
        p2_ref[h] = (0.5 * jnp.exp(s2 - v2[0]) / z).astype(p2_ref.dtype)


def _peer_stats(scores, tm):
    t = scores.shape[-1]
    hk = pl.BlockSpec((PEER_HEADS, PEER_NKEYS, tm), lambda i: (0, 0, i))
    f32 = jax.ShapeDtypeStruct((PEER_HEADS, PEER_NKEYS, t), F32)
    b16 = jax.ShapeDtypeStruct((PEER_HEADS, PEER_NKEYS, t), BF16)
    return pl.pallas_call(
        _peer_stats_kernel,
        grid=(t // tm,),
        in_specs=[pl.BlockSpec((PEER_HEADS, 2, PEER_NKEYS, tm), lambda i: (0, 0, 0, i))],
        out_specs=[hk, hk, hk, hk],
        out_shape=[b16, f32, f32, b16],
        compiler_params=_cparams(("arbitrary",)),
    )(scores)


def _peer_dense_kernel(ht_ref, u_ref, vt_ref, rk_in, lim_ref, p1_ref, p2_in, o_ref, acc_ref, g_ref, rk_ref, p2_ref,
                       *, ne1):
    j = pl.program_id(1)

    @pl.when(j == 0)
    def _():
        acc_ref[...] = jnp.zeros_like(acc_ref)
        rk_ref[...] = rk_in[...]
        p2_ref[...] = p2_in[...]

    zero = jnp.zeros((), BF16)
    for r in range(ne1):
        e1 = j * ne1 + r
        rows = slice(r * PEER_NKEYS, (r + 1) * PEER_NKEYS)
        lim_rows = [lim_ref[h, pl.ds(e1, 1), :].astype(BF16) for h in range(PEER_HEADS)]
        p1_rows = [p1_ref[h, pl.ds(e1, 1), :].astype(BF16) for h in range(PEER_HEADS)]
        for tt in range(g_ref.shape[1] // LANES):
            ls = slice(tt * LANES, (tt + 1) * LANES)
            gacc = None
            for h in range(PEER_HEADS):
                hs = slice(h * PEER_NKEYS, (h + 1) * PEER_NKEYS)
                w = jnp.where(rk_ref[hs, ls] < lim_rows[h][:, ls], p2_ref[hs, ls] * p1_rows[h][:, ls], zero)
                gacc = w if gacc is None else gacc + w
            g_ref[rows, ls] = gacc
    act = jnp.dot(u_ref[...], ht_ref[...], preferred_element_type=F32)
    gelu2 = act * (1.0 + jnp.tanh(math.sqrt(2.0 / math.pi) * (act + 0.044715 * (act * act * act))))
    pt = gelu2.astype(BF16) * g_ref[...]
    acc_ref[...] += jnp.dot(vt_ref[...], pt, preferred_element_type=F32)

    @pl.when(j == pl.num_programs(1) - 1)
    def _():
        o_ref[...] = acc_ref[...].T


def _peer_dense(m2t, u_b, vt_b, rk, lim, p1, p2, tm, te):
    d, t = m2t.shape
    ne = u_b.shape[0]
    hk = pl.BlockSpec((PEER_HEADS, PEER_NKEYS, tm), lambda i, j: (0, 0, i))
    flat = pl.BlockSpec((PEER_HEADS * PEER_NKEYS, tm), lambda i, j: (0, i))
    return pl.pallas_call(
        functools.partial(_peer_dense_kernel, ne1=te // PEER_NKEYS),
        grid=(t // tm, ne // te),
        in_specs=[pl.BlockSpec((d, tm), lambda i, j: (0, i)),
                  pl.BlockSpec((te, d), lambda i, j: (j, 0)),
                  pl.BlockSpec((d, te), lambda i, j: (0, j)),
                  flat, hk, hk, flat],
        out_specs=pl.BlockSpec((tm, d), lambda i, j: (i, 0)),
        out_shape=jax.ShapeDtypeStruct((t, d), F32),
        scratch_shapes=[pltpu.VMEM((d, tm), F32), pltpu.VMEM((te, tm), BF16),
                        pltpu.VMEM((PEER_HEADS * PEER_NKEYS, tm), BF16),
                        pltpu.VMEM((PEER_HEADS * PEER_NKEYS, tm), BF16)],
        compiler_params=_cparams(("arbitrary", "arbitrary")),
    )(m2t, u_b, vt_b, rk.reshape(PEER_HEADS * PEER_NKEYS, t), lim, p1, p2.reshape(PEER_HEADS * PEER_NKEYS, t))


SL_X = 0
SL_B = SL_X + SSD_INNER // LANES
SL_C = SL_B + SSD_GROUPS * SSD_STATE // LANES
SL_Z = SL_C + SSD_GROUPS * SSD_STATE // LANES
SL_GH = SL_Z + SSD_INNER // LANES
SL_GS = SL_GH + D_MODEL // LANES


def kernel(x, c, ctx, c_ctx, ada_w, ada_b, w_in, hy_conv_w, hy_conv_b, hy_ffn_w1, hy_ffn_b1, hy_freq1,
           hy_ffn_w2, hy_ffn_b2, hy_freq2, hy_ffn_w3, hy_skip, ssd_conv_w, ssd_conv_b, ssd_a_log,
           ssd_dt_bias, ssd_d, ssd_norm_w, w_branch_hy, w_branch_ssd, w_out, ln1_g, ln1_b, peer_wq,
           peer_subkeys, peer_u, peer_v, ln2_g, ln2_b):
    nb, seq, d = x.shape
    lc = ctx.shape[1]
    t = nb * seq
    p = dict(hy_ffn_w1=hy_ffn_w1[0], hy_ffn_b1=hy_ffn_b1[0], hy_freq1=hy_freq1[0], hy_ffn_w2=hy_ffn_w2[0],
             hy_ffn_b2=hy_ffn_b2[0], hy_freq2=hy_freq2[0], hy_ffn_w3=hy_ffn_w3[0])

    cc = jnp.zeros((8, d), F32).at[:nb].set(c).at[nb].set(c_ctx)
    mod = _ada(cc, ada_w[0], ada_b[0])
    sh1, sc1, gt1, sh2, sc2, gt2 = [mod[:nb, i * d:(i + 1) * d].reshape(nb, 1, d) for i in range(6)]
    csh1 = mod[nb:nb + 1, 0:d].reshape(1, 1, d)
    csc1 = mod[nb:nb + 1, d:2 * d].reshape(1, 1, d)

    w = w_in[0]
    w_hy = w[:, :OFF_X].astype(BF16)
    w_rest = jnp.concatenate([w[:, OFF_X:OFF_DT], w[:, OFF_Z:]], axis=1).astype(BF16)
    w_dt = w[:, OFF_DT:OFF_Z].astype(BF16)

    m_ctx = _modulate(ctx, csc1, csh1, 256).reshape(nb * lc, d)
    pc = _matmul(m_ctx, w_rest, 512, 1024, slab=True, ncols=OFF_C - OFF_X)
    dtc = _matmul(m_ctx, w_dt, 512, LANES)
    nxb = (OFF_C - OFF_X) // LANES
    cw = ssd_conv_w[0].reshape(3, -1, LANES).transpose(1, 0, 2)
    cbias = ssd_conv_b[0].reshape(-1, 1, LANES)
    xbc_ctx = _conv3(pc, 0, nxb, cw[:nxb], cbias[:nxb], lc, True)
    gw = SSD_HPG * SSD_HEAD_DIM
    zero_state = jnp.zeros((nb, SSD_GROUPS, SSD_STATE, gw), F32)
    dt_bias = ssd_dt_bias[0]
    a_log = ssd_a_log[0]
    xs_sl = SSD_INNER // LANES
    _, _, s_f, s_b = _ssd_scan(xbc_ctx, 0, xs_sl, xs_sl, dtc, dt_bias, a_log, zero_state, zero_state, lc)

    m1 = _modulate(x, sc1, sh1, 256).reshape(t, d)
    proj_hy = _matmul(m1, w_hy, 1024, 1024, slab=True)
    proj = _matmul(m1, w_rest, 1024, 1024, out_dtype=BF16, slab=True)
    dtp = _matmul(m1, w_dt, 1024, LANES)

    hw = hy_conv_w[0].reshape(3, -1, LANES).transpose(1, 0, 2)
    hb = hy_conv_b[0].reshape(-1, 1, LANES)
    nh = HY_WIDTH // LANES
    ug = _hyconv(proj_hy, 3 * nh, hw, hb, seq)
    hspec = _hyena_filter_spectra(p, seq)
    zg = _longconv(ug, ug, 0, nh, hy_skip[0, 0], hspec, 0, seq, False)
    y_hy = _longconv(zg, ug, 0, 2 * nh, hy_skip[0, 1], hspec, 1, seq, True)

    xbc = _conv3(proj, SL_X, SL_Z - SL_X, cw, cbias, seq, True)
    y_f, y_b, _, _ = _ssd_scan(xbc, 0, xs_sl, xs_sl + SSD_GROUPS, dtp, dt_bias, a_log, s_f, s_b, seq)
    d_ch = jnp.repeat(ssd_d[0, 0] + ssd_d[0, 1], SSD_HEAD_DIM)
    y_ss = _ssd_norm(y_f, y_b, xbc, proj, SL_Z, d_ch, ssd_norm_w[0], 512)

    merged = _branch_merge(y_hy, y_ss, w_branch_hy[0].astype(BF16), w_branch_ssd[0].astype(BF16),
                           proj, SL_GH, SL_GS, 512, 1024)
    h1, m2, m2t = _outproj_ln(merged.reshape(nb, seq, d), w_out[0].astype(BF16), x, gt1, ln1_g[0], ln1_b[0],
                              sc2, sh2, 512)

    m2f = m2.reshape(t, d)
    scores = _peer_scores(m2f, peer_wq[0].astype(BF16), peer_subkeys[0].astype(BF16), 256)
    rk, lim, p1, p2 = _peer_stats(scores, 256)
    ffn = _peer_dense(m2t, peer_u[0].astype(BF16), peer_v[0].T.astype(BF16), rk, lim, p1, p2, 512, 1024)
    return _resid_ln(h1, ffn.reshape(nb, seq, d), gt2, ln2_g[0], ln2_b[0], 256)
```

```python
import functools
import math

import numpy as np
import jax
import jax.numpy as jnp
from jax import lax
from jax.experimental import pallas as pl
from jax.experimental.pallas import tpu as pltpu

F32 = jnp.float32
BF16 = jnp.bfloat16

D_MODEL = 2048
DEPTH = 1
CTX_LEN = 256
HY_WIDTH = D_MODEL
HY_BANDS = 16
HY_EMB = 2 * HY_BANDS + 1
HY_FFN = 64
HY_MIN_DECAY = math.log(1e-2) / 1.5
HY_MAX_DECAY = math.log(1e-2) / 0.3
SSD_INNER = 2 * D_MODEL
SSD_HEAD_DIM = 64
SSD_HEADS = SSD_INNER // SSD_HEAD_DIM
SSD_STATE = 128
SSD_GROUPS = 8
SSD_HPG = SSD_HEADS // SSD_GROUPS
SSD_CHUNK = 128
OFF_X = 3 * HY_WIDTH
OFF_B = OFF_X + SSD_INNER
OFF_C = OFF_B + SSD_GROUPS * SSD_STATE
OFF_DT = OFF_C + SSD_GROUPS * SSD_STATE
OFF_Z = OFF_DT + 2 * SSD_HEADS
OFF_GATE = OFF_Z + SSD_INNER
N_COLS = OFF_GATE + 2 * D_MODEL
PEER_HEADS = 8
PEER_NKEYS = 128
PEER_EXPERTS = PEER_NKEYS * PEER_NKEYS
PEER_TOPK = 16
PEER_DKEY = 256
DEEPNORM_ALPHA = (2.0 * DEPTH) ** 0.25

LANES = 128
VMEM_LIMIT = 56 * 1024 * 1024

FFT_N1 = 128
SA_PITCH_PAD = 8
SB_PITCH = FFT_N1 + 8


def _cparams(sem, vmem=VMEM_LIMIT):
    return pltpu.CompilerParams(dimension_semantics=sem, vmem_limit_bytes=vmem)


def _silu(x):
    return x * (1.0 / (1.0 + jnp.exp(-x)))


def _sigmoid(x):
    return 1.0 / (1.0 + jnp.exp(-x))


def _softplus(x):
    return jnp.maximum(x, 0.0) + jnp.log(1.0 + jnp.exp(-jnp.abs(x)))


def _gelu_tanh(x):
    return 0.5 * x * (1.0 + jnp.tanh(math.sqrt(2.0 / math.pi) * (x + 0.044715 * (x * x * x))))


def _normalize_rows(x, eps):
    mu = jnp.mean(x, axis=-1, keepdims=True)
    xc = x - mu
    var = jnp.mean(xc * xc, axis=-1, keepdims=True)
    return xc * lax.rsqrt(var + eps)


def _ada_kernel(c_ref, w_ref, b_ref, o_ref):
    a = _silu(c_ref[...]).astype(BF16)
    o_ref[...] = jnp.dot(a, w_ref[...].astype(BF16), preferred_element_type=F32) + b_ref[...]


def _ada(cc, ada_w, ada_b):
    rows, d = cc.shape
    n = ada_w.shape[1]
    tn = 1024
    return pl.pallas_call(
        _ada_kernel,
        grid=(n // tn,),
        in_specs=[pl.BlockSpec((rows, d), lambda j: (0, 0)),
                  pl.BlockSpec((d, tn), lambda j: (0, j)),
                  pl.BlockSpec((1, tn), lambda j: (0, j))],
        out_specs=pl.BlockSpec((rows, tn), lambda j: (0, j)),
        out_shape=jax.ShapeDtypeStruct((rows, n), F32),
        compiler_params=_cparams(("arbitrary",)),
    )(cc, ada_w, ada_b.reshape(1, n))


def _mod_kernel(x_ref, sc_ref, sh_ref, o_ref):
    xn = _normalize_rows(x_ref[0], 1e-6)
    o_ref[0] = (xn * (1.0 + sc_ref[0]) + sh_ref[0]).astype(o_ref.dtype)


def _modulate(x, scale, shift, tr):
    b, l, d = x.shape
    per_batch = scale.shape[0] == b
    smap = (lambda i, j: (i, 0, 0)) if per_batch else (lambda i, j: (0, 0, 0))
    return pl.pallas_call(
        _mod_kernel,
        grid=(b, l // tr),
        in_specs=[pl.BlockSpec((1, tr, d), lambda i, j: (i, j, 0)),
                  pl.BlockSpec((1, 1, d), smap),
                  pl.BlockSpec((1, 1, d), smap)],
        out_specs=pl.BlockSpec((1, tr, d), lambda i, j: (i, j, 0)),
        out_shape=jax.ShapeDtypeStruct((b, l, d), BF16),
        compiler_params=_cparams(("arbitrary", "arbitrary")),
    )(x, scale, shift)


def _mm_kernel(a_ref, b_ref, o_ref, *, slab):
    acc = jnp.dot(a_ref[...], b_ref[...], preferred_element_type=F32)
    if slab:
        for s in range(o_ref.shape[0]):
            o_ref[s] = acc[:, s * LANES:(s + 1) * LANES].astype(o_ref.dtype)
    else:
        o_ref[...] = acc.astype(o_ref.dtype)


def _matmul(a, b, tm, tn, out_dtype=F32, slab=False, ncols=None):
    m, k = a.shape
    n = b.shape[1] if ncols is None else ncols
    if slab:
        out_shape = jax.ShapeDtypeStruct((n // LANES, m, LANES), out_dtype)
        out_spec = pl.BlockSpec((tn // LANES, tm, LANES), lambda i, j: (j, i, 0))
    else:
        out_shape = jax.ShapeDtypeStruct((m, n), out_dtype)
        out_spec = pl.BlockSpec((tm, tn), lambda i, j: (i, j))
    return pl.pallas_call(
        functools.partial(_mm_kernel, slab=slab),
        grid=(m // tm, n // tn),
        in_specs=[pl.BlockSpec((tm, k), lambda i, j: (i, 0)),
                  pl.BlockSpec((k, tn), lambda i, j: (0, j))],
        out_specs=out_spec,
        out_shape=out_shape,
        compiler_params=_cparams(("arbitrary", "arbitrary")),
    )(a, b)


def _conv3_kernel(x_ref, w_ref, b_ref, o_ref, *, silu):
    for i in range(x_ref.shape[0]):
        x = x_ref[i].astype(F32)
        l = x.shape[0]
        row = lax.broadcasted_iota(jnp.int32, x.shape, 0)
        prev = jnp.where(row == 0, 0.0, pltpu.roll(x, 1, 0))
        nxt = jnp.where(row == l - 1, 0.0, pltpu.roll(x, l - 1, 0))
        w = w_ref[i]
        y = prev * w[0:1] + x * w[1:2] + nxt * w[2:3] + b_ref[i]
        if silu:
            y = _silu(y)
        o_ref[i] = y.astype(o_ref.dtype)


def _conv3(p_slab, slab0, nslab, w, bias, seq, silu):
    rows = p_slab.shape[1]
    nb = rows // seq
    sp = 4
    return pl.pallas_call(
        functools.partial(_conv3_kernel, silu=silu),
        grid=(nslab // sp, nb),
        in_specs=[pl.BlockSpec((sp, seq, LANES), lambda s, b: (slab0 // sp + s, b, 0)),
                  pl.BlockSpec((sp, 3, LANES), lambda s, b: (s, 0, 0)),
                  pl.BlockSpec((sp, 1, LANES), lambda s, b: (s, 0, 0))],
        out_specs=pl.BlockSpec((sp, seq, LANES), lambda s, b: (s, b, 0)),
        out_shape=jax.ShapeDtypeStruct((nslab, rows, LANES), BF16),
        compiler_params=_cparams(("arbitrary", "arbitrary")),
    )(p_slab, w, bias)


def _hyconv_kernel(x_ref, w_ref, b_ref, o_ref, *, seq, nb):
    nj = seq // FFT_N1

    def gather(n1):
        return jnp.concatenate(
            [x_ref[0, pl.ds(n1 + b * seq, nj, stride=FFT_N1), :] for b in range(nb)], axis=0)

    w = w_ref[0]
    w0, w1, w2, bias = w[0:1], w[1:2], w[2:3], b_ref[0]
    jrow = lax.broadcasted_iota(jnp.int32, (nb * nj, LANES), 0) % nj

    def shift_down(g):
        return jnp.where(jrow == 0, 0.0, pltpu.roll(g, 1, 0))

    def shift_up(g):
        return jnp.where(jrow == nj - 1, 0.0, pltpu.roll(g, nb * nj - 1, 0))

    g_first = gather(0)
    g_second = gather(1)
    g_last = gather(FFT_N1 - 1)
    dt = o_ref.dtype
    o_ref[0, 0] = (shift_down(g_last) * w0 + g_first * w1 + g_second * w2 + bias).astype(dt)

    def body(n1, carry):
        g_prev, g_cur = carry
        g_next = gather(n1 + 1)
        o_ref[0, n1] = (g_prev * w0 + g_cur * w1 + g_next * w2 + bias).astype(dt)
        return g_cur, g_next

    g_prev, g_cur = lax.fori_loop(1, FFT_N1 - 1, body, (g_first, g_second), unroll=6)
    o_ref[0, FFT_N1 - 1] = (g_prev * w0 + g_cur * w1 + shift_up(g_first) * w2 + bias).astype(dt)


def _hyconv(p_slab, nslab, w, bias, seq):
    rows = p_slab.shape[1]
    nb = rows // seq
    nj = seq // FFT_N1
    return pl.pallas_call(
        functools.partial(_hyconv_kernel, seq=seq, nb=nb),
        grid=(nslab,),
        in_specs=[pl.BlockSpec((1, rows, LANES), lambda s: (s, 0, 0)),
                  pl.BlockSpec((1, 3, LANES), lambda s: (s, 0, 0)),
                  pl.BlockSpec((1, 1, LANES), lambda s: (s, 0, 0))],
        out_specs=pl.BlockSpec((1, FFT_N1, nb * nj, LANES), lambda s: (s, 0, 0, 0)),
        out_shape=jax.ShapeDtypeStruct((nslab, FFT_N1, nb * nj, LANES), BF16),
        compiler_params=_cparams(("arbitrary",)),
    )(p_slab, w, bias)


@functools.lru_cache(maxsize=None)
def _dft_tables(seq):
    nj = seq // FFT_N1
    n2 = nj + nj // 2
    n = FFT_N1 * n2
    out0 = nj // 2
    n1 = np.arange(FFT_N1, dtype=np.float64)
    k2 = np.arange(n2, dtype=np.float64)
    jn = np.arange(nj, dtype=np.float64)
    ang = -2.0 * np.pi * (k2[None, :, None] * jn[None, None, :] / n2 + n1[:, None, None] * k2[None, :, None] / n)
    cr, ci = np.cos(ang), np.sin(ang)
    fa = np.concatenate([np.concatenate([cr, -ci], axis=2), np.concatenate([ci, cr], axis=2)], axis=1)
    k1 = np.arange(FFT_N1, dtype=np.float64)
    angb = -2.0 * np.pi * np.outer(k1, n1) / FFT_N1
    dr, di = np.cos(angb), np.sin(angb)
    fb = np.block([[dr, -di], [di, dr]])
    fbi = np.block([[dr, di], [-di, dr]])
    angi = 2.0 * np.pi * (n1[:, None, None] * k2[None, None, :] / n + (out0 + jn)[None, :, None] * k2[None, None, :] / n2)
    er, ei = np.cos(angi) / n, np.sin(angi) / n
    fai = np.concatenate([np.concatenate([er, -ei], axis=2), np.concatenate([ei, er], axis=2)], axis=1)
    return (fa.astype(np.float32), fb.astype(np.float32), fbi.astype(np.float32), fai.astype(np.float32), n2)


@functools.lru_cache(maxsize=None)
def _filter_consts(seq):
    t = np.linspace(0.0, 1.0, seq, dtype=np.float32)[:, None].astype(np.float64)
    w = 2.0 * np.pi * np.arange(seq, dtype=np.float32)[:, None].astype(np.float64) / seq
    bands = np.linspace(1e-4, HY_BANDS - 1, HY_BANDS, dtype=np.float32)[None, :].astype(np.float64)
    feats = np.concatenate([t, np.cos(bands * w), -np.sin(bands * w)], axis=-1)
    feats_p = np.zeros((seq, LANES), np.float32)
    feats_p[:, :HY_EMB] = feats
    dist = (np.abs(np.arange(seq) - seq // 2).astype(np.float32) / np.float32(seq / 2.0)).reshape(seq, 1)
    deltas = np.abs(np.linspace(HY_MIN_DECAY, HY_MAX_DECAY, HY_WIDTH, dtype=np.float32))
    return feats_p, dist.astype(np.float32), deltas.astype(np.float32)


def _dot_f32(a, b):
    def split(x):
        hi = x.astype(BF16)
        r1 = x - hi.astype(F32)
        mid = r1.astype(BF16)
        lo = (r1 - mid.astype(F32)).astype(BF16)
        return hi, mid, lo
    a0, a1, a2 = split(a)
    b0, b1, b2 = split(b)
    d = functools.partial(jnp.dot, preferred_element_type=F32)
    return (d(a0, b0) + (d(a0, b1) + d(a1, b0)) + (d(a0, b2) + d(a1, b1) + d(a2, b0)))


def _filter_kernel(feats_ref, w1_ref, b1_ref, f1_ref, w2_ref, b2_ref, f2_ref, w3_ref, dist_ref, delta_ref,
                   fa_ref, fb_ref, o_ref, hid_ref, h_ref, sar_ref, sai_ref, *, seq, n2, pitch):
    nj = seq // FFT_N1

    @pl.when((pl.program_id(0) == 0) & (pl.program_id(1) == 0))
    def _():
        h1 = jnp.sin(f1_ref[...] * (_dot_f32(feats_ref[...], w1_ref[...]) + b1_ref[...]))
        hid_ref[...] = jnp.sin(f2_ref[...] * (_dot_f32(h1, w2_ref[...]) + b2_ref[...])).astype(BF16)

    ns = h_ref.shape[0]
    lanes = lambda f: jnp.concatenate([f(s) for s in range(ns)], axis=1)
    w3 = lanes(lambda s: w3_ref[0, s]).astype(BF16)
    h = jnp.dot(hid_ref[...], w3, preferred_element_type=F32)
    h = h * jnp.exp(-dist_ref[...] * lanes(lambda s: delta_ref[s]))
    h = h / (jnp.sum(jnp.abs(h), axis=0, keepdims=True) + 1e-6)
    for s in range(ns):
        h_ref[s] = h[:, s * LANES:(s + 1) * LANES]

    def stage_a(n1, c):
        g = lanes(lambda s: h_ref[s, pl.ds(n1, nj, stride=FFT_N1), :]).astype(BF16)
        y = jnp.dot(fa_ref[n1], g, preferred_element_type=F32)
        base = pl.multiple_of(n1 * pitch, 8)
        for s in range(ns):
            sar_ref[s, pl.ds(base, n2), :] = y[:n2, s * LANES:(s + 1) * LANES]
            sai_ref[s, pl.ds(base, n2), :] = y[n2:, s * LANES:(s + 1) * LANES]
        return c

    lax.fori_loop(0, FFT_N1, stage_a, 0, unroll=8)

    def stage_b(k2, c):
        yr = lanes(lambda s: sar_ref[s, pl.ds(k2, FFT_N1, stride=pitch), :])
        yi = lanes(lambda s: sai_ref[s, pl.ds(k2, FFT_N1, stride=pitch), :])
        op = jnp.concatenate([yr, yi], axis=0).astype(BF16)
        res = jnp.dot(fb_ref[...], op, preferred_element_type=F32).astype(o_ref.dtype)
        for s in range(ns):
            o_ref[0, s, k2] = res[:, s * LANES:(s + 1) * LANES]
        return c

    lax.fori_loop(0, n2, stage_b, 0, unroll=4)


def _hyena_filter_spectra(p, seq):
    fa, fb, _, _, n2 = _dft_tables(seq)
    nj = seq // FFT_N1
    pitch = n2 + SA_PITCH_PAD
    feats, dist, deltas = _filter_consts(seq)
    nsl = HY_WIDTH // LANES

    def padk(w, rows):
        return jnp.zeros((rows, w.shape[1]), F32).at[:w.shape[0]].set(w)

    def padn(v):
        return jnp.zeros((1, LANES), F32).at[0, :v.shape[0]].set(v)

    w1 = jnp.zeros((LANES, LANES), F32).at[:HY_EMB, :HY_FFN].set(p['hy_ffn_w1'])
    w2 = jnp.zeros((LANES, LANES), F32).at[:HY_FFN, :HY_FFN].set(p['hy_ffn_w2'])
    w3 = padk(p['hy_ffn_w3'], LANES).reshape(LANES, 2, nsl, LANES).transpose(1, 2, 0, 3)
    fa_real = jnp.asarray(fa[:, :, :nj]).astype(BF16)
    const = lambda shape: pl.BlockSpec(shape, lambda o, s: (0,) * len(shape))
    ns = 2
    return pl.pallas_call(
        functools.partial(_filter_kernel, seq=seq, n2=n2, pitch=pitch),
        grid=(2, nsl // ns),
        in_specs=[const((seq, LANES)), const((LANES, LANES)), const((1, LANES)), const((1, LANES)),
                  const((LANES, LANES)), const((1, LANES)), const((1, LANES)),
                  pl.BlockSpec((1, ns, LANES, LANES), lambda o, s: (o, s, 0, 0)),
                  const((seq, 1)),
                  pl.BlockSpec((ns, 1, LANES), lambda o, s: (s, 0, 0)),
                  const((FFT_N1, 2 * n2, nj)), const((2 * FFT_N1, 2 * FFT_N1))],
        out_specs=pl.BlockSpec((1, ns, n2, 2 * FFT_N1, LANES), lambda o, s: (o, s, 0, 0, 0)),
        out_shape=jax.ShapeDtypeStruct((2, nsl, n2, 2 * FFT_N1, LANES), BF16),
        scratch_shapes=[pltpu.VMEM((seq, LANES), BF16), pltpu.VMEM((ns, seq, LANES), F32),
                        pltpu.VMEM((ns, FFT_N1 * pitch, LANES), F32),
                        pltpu.VMEM((ns, FFT_N1 * pitch, LANES), F32)],
        compiler_params=_cparams(("arbitrary", "arbitrary")),
    )(jnp.asarray(feats), w1, padn(p['hy_ffn_b1']), padn(p['hy_freq1']), w2, padn(p['hy_ffn_b2']),
      padn(p['hy_freq2']), w3, jnp.asarray(dist), jnp.asarray(deltas).reshape(nsl, 1, LANES),
      fa_real, jnp.asarray(fb).astype(BF16))


def _longconv_kernel(x_ref, g_ref, skip_ref, h_ref, fa_ref, fb_ref, fbi_ref, fai_ref, o_ref,
                     sar_ref, sai_ref, sbr_ref, sbi_ref, *, seq, n2, pitch, natural_out):
    nj = seq // FFT_N1

    def stage_a(n1, c):
        y = jnp.dot(fa_ref[n1], x_ref[0, n1].astype(BF16), preferred_element_type=F32)
        base = pl.multiple_of(n1 * pitch, 8)
        sar_ref[pl.ds(base, n2), :] = y[:n2]
        sai_ref[pl.ds(base, n2), :] = y[n2:]
        return c

    lax.fori_loop(0, FFT_N1, stage_a, 0, unroll=8)

    def stage_b(kp, c):
        ks = (2 * kp, 2 * kp + 1)
        yr = jnp.concatenate([sar_ref[pl.ds(k, FFT_N1, stride=pitch), :] for k in ks], axis=1)
        yi = jnp.concatenate([sai_ref[pl.ds(k, FFT_N1, stride=pitch), :] for k in ks], axis=1)
        xf = jnp.dot(fb_ref[...], jnp.concatenate([yr, yi], axis=0).astype(BF16), preferred_element_type=F32)
        xr, xi = xf[:FFT_N1], xf[FFT_N1:]
        hf = jnp.concatenate([h_ref[0, k] for k in ks], axis=1).astype(F32)
        hr, hi = hf[:FFT_N1], hf[FFT_N1:]
        z = jnp.concatenate([xr * hr - xi * hi, xr * hi + xi * hr], axis=0).astype(BF16)
        w = jnp.dot(fbi_ref[...], z, preferred_element_type=F32)
        for q, k in enumerate(ks):
            base = pl.multiple_of(k * SB_PITCH, 8)
            sbr_ref[pl.ds(base, FFT_N1), :] = w[:FFT_N1, q * LANES:(q + 1) * LANES]
            sbi_ref[pl.ds(base, FFT_N1), :] = w[FFT_N1:, q * LANES:(q + 1) * LANES]
        return c

    lax.fori_loop(0, n2 // 2, stage_b, 0, unroll=2)

    skip = skip_ref[0]

    def stage_c(n1, c):
        wr = sbr_ref[pl.ds(n1, n2, stride=SB_PITCH), :]
        wi = sbi_ref[pl.ds(n1, n2, stride=SB_PITCH), :]
        y = jnp.dot(fai_ref[n1], jnp.concatenate([wr, wi], axis=0).astype(BF16), preferred_element_type=F32)
        res = g_ref[0, n1].astype(F32) * (y + x_ref[0, n1].astype(F32) * skip)
        if natural_out:
            for b in range(res.shape[0] // nj):
                o_ref[0, pl.ds(n1 + b * seq, nj, stride=FFT_N1), :] = res[b * nj:(b + 1) * nj]
        else:
            o_ref[0, n1] = res.astype(o_ref.dtype)
        return c

    lax.fori_loop(0, FFT_N1, stage_c, 0, unroll=8)


def _longconv(xg, gg, slab_x, slab_g, skip, hspec, order, seq, natural_out):
    fa, fb, fbi, fai, n2 = _dft_tables(seq)
    pitch = n2 + SA_PITCH_PAD
    nsl = HY_WIDTH // LANES
    rows = xg.shape[2]
    nb = rows // (seq // FFT_N1)
    const = lambda shape: pl.BlockSpec(shape, lambda s: (0,) * len(shape))
    if natural_out:
        out_shape = jax.ShapeDtypeStruct((nsl, nb * seq, LANES), F32)
        out_spec = pl.BlockSpec((1, nb * seq, LANES), lambda s: (s, 0, 0))
    else:
        out_shape = jax.ShapeDtypeStruct((nsl, FFT_N1, rows, LANES), BF16)
        out_spec = pl.BlockSpec((1, FFT_N1, rows, LANES), lambda s: (s, 0, 0, 0))
    return pl.pallas_call(
        functools.partial(_longconv_kernel, seq=seq, n2=n2, pitch=pitch, natural_out=natural_out),
        grid=(nsl,),
        in_specs=[pl.BlockSpec((1, FFT_N1, rows, LANES), lambda s: (slab_x + s, 0, 0, 0)),
                  pl.BlockSpec((1, FFT_N1, rows, LANES), lambda s: (slab_g + s, 0, 0, 0)),
                  pl.BlockSpec((1, 1, LANES), lambda s: (s, 0, 0)),
                  pl.BlockSpec((None, 1, n2, 2 * FFT_N1, LANES), lambda s: (order, s, 0, 0, 0)),
                  const(fa.shape), const(fb.shape), const(fbi.shape), const(fai.shape)],
        out_specs=out_spec,
        out_shape=out_shape,
        scratch_shapes=[pltpu.VMEM((FFT_N1 * pitch, LANES), F32), pltpu.VMEM((FFT_N1 * pitch, LANES), F32),
                        pltpu.VMEM((n2 * SB_PITCH, LANES), F32), pltpu.VMEM((n2 * SB_PITCH, LANES), F32)],
        compiler_params=_cparams(("arbitrary",)),
    )(xg, gg, skip.reshape(nsl, 1, LANES), hspec,
      jnp.asarray(fa).astype(BF16), jnp.asarray(fb).astype(BF16),
      jnp.asarray(fbi).astype(BF16), jnp.asarray(fai).astype(BF16))


def _dot_exact_lhs(lhs_bf16, x):
    hi = x.astype(BF16)
    r1 = x - hi.astype(F32)
    mid = r1.astype(BF16)
    lo = (r1 - mid.astype(F32)).astype(BF16)
    d = functools.partial(jnp.dot, preferred_element_type=F32)
    return d(lhs_bf16, hi) + (d(lhs_bf16, mid) + d(lhs_bf16, lo))


def _ssd_chunk_groups(x_ref, b_ref, c_ref, dt_ref, bias_ref, alog_ref, e_ref, y_ref, st_ref, rev):
    gw = SSD_HPG * SSD_HEAD_DIM
    dot = functools.partial(jnp.dot, preferred_element_type=F32)
    off = SSD_HEADS if rev else 0
    dt_all = _softplus(dt_ref[...] + bias_ref[...])
    a_all = dt_all * (-jnp.exp(alog_ref[...]))
    r_i = lax.broadcasted_iota(jnp.int32, (SSD_CHUNK, SSD_CHUNK), 0)
    c_i = lax.broadcasted_iota(jnp.int32, (SSD_CHUNK, SSD_CHUNK), 1)
    valid = (c_i >= r_i) if rev else (c_i <= r_i)
    tri = jnp.where(valid, 1.0, 0.0).astype(BF16)
    acum = _dot_exact_lhs(tri, a_all)
    src_t = (acum - jnp.log(dt_all)).T
    tot = acum[0:1, :] if rev else acum[SSD_CHUNK - 1:SSD_CHUNK, :]
    dtw_b = (dt_all * jnp.exp(tot - acum)).astype(BF16)
    cd8 = jnp.broadcast_to(jnp.exp(tot), (8, LANES))
    cd_hi = cd8.astype(BF16)
    cd_r = cd8 - cd_hi.astype(F32)
    cd_mid = cd_r.astype(BF16)
    cd_lo = (cd_r - cd_mid.astype(F32)).astype(BF16)
    lo_half = c_i < SSD_HEAD_DIM
    zero_b = jnp.zeros((), BF16)

    def group(g):
        bt = b_ref[g].astype(F32).T.astype(BF16)
        cg = c_ref[g].astype(BF16)
        cb = dot(cg, bt)
        for sl in range(gw // LANES):
            col = slice(sl * LANES, (sl + 1) * LANES)
            e_s = e_ref[:, g * gw + sl * LANES:g * gw + (sl + 1) * LANES]
            xb = x_ref[(gw // LANES) * g + sl]
            ms, ea = [], []
            for hh in range(2):
                h = off + SSD_HPG * g + 2 * sl + hh
                a_col = jnp.broadcast_to(acum[:, h:h + 1], (SSD_CHUNK, SSD_CHUNK))
                seg = jnp.where(valid, a_col - src_t[h:h + 1, :], -1e30)
                ms.append((cb * jnp.exp(seg)).astype(BF16))
                ea.append(jnp.exp(a_col))
            rhs = jnp.concatenate([jnp.where(lo_half, xb, zero_b), jnp.where(lo_half, zero_b, xb)], axis=0)
            yd = dot(jnp.concatenate(ms, axis=1), rhs)
            sgs = st_ref[g, :, col]
            yoff = dot(cg, sgs.astype(BF16)) * jnp.where(lo_half, ea[0], ea[1])
            y_ref[g, :, col] = (yd + yoff).astype(y_ref.dtype)
            wm = (xb.astype(F32) * dot(dtw_b, e_s)).astype(BF16)
            cd = (dot(cd_hi, e_s) + (dot(cd_mid, e_s) + dot(cd_lo, e_s)))[0:1]
            st_ref[g, :, col] = sgs * cd + dot(bt, wm)

    return group


def _ssd_kernel(xf_ref, bf_ref, cf_ref, dtf_ref, xr_ref, br_ref, cr_ref, dtr_ref, bias_ref, alog_ref, ef_ref, er_ref,
                initf_ref, initr_ref, yf_ref, yr_ref, finf_ref, finr_ref, stf_ref, str_ref, *, nc):
    ci = pl.program_id(1)

    @pl.when(ci == 0)
    def _():
        stf_ref[...] = initf_ref[0]
        str_ref[...] = initr_ref[0]

    fwd = _ssd_chunk_groups(xf_ref, bf_ref, cf_ref, dtf_ref, bias_ref, alog_ref, ef_ref, yf_ref, stf_ref, False)
    bwd = _ssd_chunk_groups(xr_ref, br_ref, cr_ref, dtr_ref, bias_ref, alog_ref, er_ref, yr_ref, str_ref, True)
    for g in range(SSD_GROUPS):
        fwd(g)
        bwd(g)

    @pl.when(ci == nc - 1)
    def _():
        finf_ref[0] = stf_ref[...]
        finr_ref[0] = str_ref[...]


def _ssd_scan(xbc, slab_x, slab_b, slab_c, dt, dt_bias, a_log, init_f, init_r, seq):
    rows = xbc.shape[1]
    nb = rows // seq
    nc = seq // SSD_CHUNK
    gw = SSD_HPG * SSD_HEAD_DIM
    xs = SSD_INNER // LANES
    pos_f = lambda b, c: b * nc + c
    pos_r = lambda b, c: b * nc + (nc - 1 - c)
    expand = np.zeros((2, 2 * SSD_HEADS, SSD_INNER), np.float32)
    for r in range(2):
        for h in range(SSD_HEADS):
            expand[r, r * SSD_HEADS + h, h * SSD_HEAD_DIM:(h + 1) * SSD_HEAD_DIM] = 1.0

    def data_specs(pos):
        return [pl.BlockSpec((xs, SSD_CHUNK, LANES), lambda b, c: (slab_x // xs, pos(b, c), 0)),
                pl.BlockSpec((SSD_GROUPS, SSD_CHUNK, LANES), lambda b, c: (slab_b // SSD_GROUPS, pos(b, c), 0)),
                pl.BlockSpec((SSD_GROUPS, SSD_CHUNK, LANES), lambda b, c: (slab_c // SSD_GROUPS, pos(b, c), 0)),
                pl.BlockSpec((SSD_CHUNK, LANES), lambda b, c: (pos(b, c), 0))]

    vec = pl.BlockSpec((1, LANES), lambda b, c: (0, 0))
    emat = pl.BlockSpec((2 * SSD_HEADS, SSD_INNER), lambda b, c: (0, 0))
    state = pl.BlockSpec((1, SSD_GROUPS, SSD_STATE, gw), lambda b, c: (b, 0, 0, 0))
    y_sds = jax.ShapeDtypeStruct((SSD_GROUPS, rows, gw), BF16)
    s_sds = jax.ShapeDtypeStruct((nb, SSD_GROUPS, SSD_STATE, gw), F32)
    e_b = jnp.asarray(expand).astype(BF16)
    return pl.pallas_call(
        functools.partial(_ssd_kernel, nc=nc),
        grid=(nb, nc),
        in_specs=data_specs(pos_f) + data_specs(pos_r) + [vec, vec, emat, emat, state, state],
        out_specs=[pl.BlockSpec((SSD_GROUPS, SSD_CHUNK, gw), lambda b, c: (0, pos_f(b, c), 0)),
                   pl.BlockSpec((SSD_GROUPS, SSD_CHUNK, gw), lambda b, c: (0, pos_r(b, c), 0)),
                   state, state],
        out_shape=[y_sds, y_sds, s_sds, s_sds],
        scratch_shapes=[pltpu.VMEM((SSD_GROUPS, SSD_STATE, gw), F32), pltpu.VMEM((SSD_GROUPS, SSD_STATE, gw), F32)],
        compiler_params=_cparams(("arbitrary", "arbitrary")),
    )(xbc, xbc, xbc, dt, xbc, xbc, xbc, dt, dt_bias.reshape(1, LANES), a_log.reshape(1, LANES),
      e_b[0], e_b[1], init_f, init_r)


def _ssdnorm_kernel(yf_ref, yb_ref, x_ref, z_ref, d_ref, w_ref, o_ref):
    nsl = x_ref.shape[0]
    x = jnp.concatenate([x_ref[i] for i in range(nsl)], axis=1).astype(F32)
    z = jnp.concatenate([z_ref[i] for i in range(nsl)], axis=1).astype(F32)
    y = yf_ref[0].astype(F32) + yb_ref[0].astype(F32) + x * d_ref[0]
    gx = y * _silu(z)
    ms = jnp.mean(gx * gx, axis=-1, keepdims=True)
    o_ref[...] = (gx * lax.rsqrt(ms + 1e-5) * w_ref[0]).astype(o_ref.dtype)


def _ssd_norm(yf, yb, xbc, proj, slab_z, d_ch, w, tr):
    g, rows, gw = yf.shape
    nsl = gw // LANES
    return pl.pallas_call(
        _ssdnorm_kernel,
        grid=(g, rows // tr),
        in_specs=[pl.BlockSpec((1, tr, gw), lambda q, i: (q, i, 0)),
                  pl.BlockSpec((1, tr, gw), lambda q, i: (q, i, 0)),
                  pl.BlockSpec((nsl, tr, LANES), lambda q, i: (q, i, 0)),
                  pl.BlockSpec((nsl, tr, LANES), lambda q, i: (slab_z // nsl + q, i, 0)),
                  pl.BlockSpec((1, 1, gw), lambda q, i: (q, 0, 0)),
                  pl.BlockSpec((1, 1, gw), lambda q, i: (q, 0, 0))],
        out_specs=pl.BlockSpec((tr, gw), lambda q, i: (i, q)),
        out_shape=jax.ShapeDtypeStruct((rows, g * gw), BF16),
        compiler_params=_cparams(("arbitrary", "arbitrary")),
    )(yf, yb, xbc, proj, d_ch.reshape(g, 1, gw), w.reshape(g, 1, gw))


def _merge_kernel(yhy_ref, yss_ref, whh_ref, wss_ref, gh_ref, gs_ref, o_ref):
    a1 = jnp.concatenate([yhy_ref[i] for i in range(yhy_ref.shape[0])], axis=1).astype(BF16)
    p1 = jnp.dot(a1, whh_ref[...], preferred_element_type=F32)
    p2 = jnp.dot(yss_ref[...], wss_ref[...], preferred_element_type=F32)
    gh = jnp.concatenate([gh_ref[i] for i in range(gh_ref.shape[0])], axis=1).astype(F32)
    gs = jnp.concatenate([gs_ref[i] for i in range(gs_ref.shape[0])], axis=1).astype(F32)
    o_ref[...] = (_sigmoid(gh) * p1 + _sigmoid(gs) * p2).astype(o_ref.dtype)


def _branch_merge(yhy, yss, w_hy, w_ssd, proj, slab_gh, slab_gs, tm, tn):
    nsl_k = yhy.shape[0]
    rows = yhy.shape[1]
    n = w_hy.shape[1]
    ts = tn // LANES
    return pl.pallas_call(
        _merge_kernel,
        grid=(rows // tm, n // tn),
        in_specs=[pl.BlockSpec((nsl_k, tm, LANES), lambda i, j: (0, i, 0)),
                  pl.BlockSpec((tm, yss.shape[1]), lambda i, j: (i, 0)),
                  pl.BlockSpec((w_hy.shape[0], tn), lambda i, j: (0, j)),
                  pl.BlockSpec((w_ssd.shape[0], tn), lambda i, j: (0, j)),
                  pl.BlockSpec((ts, tm, LANES), lambda i, j: (slab_gh // ts + j, i, 0)),
                  pl.BlockSpec((ts, tm, LANES), lambda i, j: (slab_gs // ts + j, i, 0))],
        out_specs=pl.BlockSpec((tm, tn), lambda i, j: (i, j)),
        out_shape=jax.ShapeDtypeStruct((rows, n), BF16),
        compiler_params=_cparams(("arbitrary", "arbitrary")),
    )(yhy, yss, w_hy, w_ssd, proj, proj)


def _outproj_ln_kernel(a_ref, w_ref, x_ref, gt_ref, g_ref, b_ref, sc_ref, sh_ref, h_ref, m_ref, mt_ref):
    y = jnp.dot(a_ref[0], w_ref[...], preferred_element_type=F32)
    h = _normalize_rows(DEEPNORM_ALPHA * x_ref[0] + gt_ref[0] * y, 1e-6) * g_ref[...] + b_ref[...]
    h_ref[0] = h
    m = _normalize_rows(h, 1e-6) * (1.0 + sc_ref[0]) + sh_ref[0]
    m_ref[0] = m.astype(m_ref.dtype)
    mt_ref[...] = m.T.astype(mt_ref.dtype)


def _outproj_ln(merged, w_out, x, gate, ln_g, ln_b, scale2, shift2, tm):
    b, l, d = x.shape
    nt = l // tm
    vec = pl.BlockSpec((1, 1, d), lambda i, j: (i, 0, 0))
    cvec = pl.BlockSpec((1, d), lambda i, j: (0, 0))
    tile = pl.BlockSpec((1, tm, d), lambda i, j: (i, j, 0))
    return pl.pallas_call(
        _outproj_ln_kernel,
        grid=(b, nt),
        in_specs=[tile, pl.BlockSpec((d, d), lambda i, j: (0, 0)), tile, vec, cvec, cvec, vec, vec],
        out_specs=[tile, tile, pl.BlockSpec((d, tm), lambda i, j: (0, i * nt + j))],
        out_shape=[jax.ShapeDtypeStruct((b, l, d), F32), jax.ShapeDtypeStruct((b, l, d), BF16),
                   jax.ShapeDtypeStruct((d, b * l), BF16)],
        compiler_params=_cparams(("arbitrary", "arbitrary")),
    )(merged, w_out, x, gate, ln_g.reshape(1, d), ln_b.reshape(1, d), scale2, shift2)


def _resid_ln_kernel(x_ref, y_ref, gt_ref, g_ref, b_ref, o_ref):
    o_ref[0] = _normalize_rows(DEEPNORM_ALPHA * x_ref[0] + gt_ref[0] * y_ref[0], 1e-6) * g_ref[...] + b_ref[...]


def _resid_ln(x, y, gate, ln_g, ln_b, tm):
    b, l, d = x.shape
    tile = pl.BlockSpec((1, tm, d), lambda i, j: (i, j, 0))
    return pl.pallas_call(
        _resid_ln_kernel,
        grid=(b, l // tm),
        in_specs=[tile, tile, pl.BlockSpec((1, 1, d), lambda i, j: (i, 0, 0)),
                  pl.BlockSpec((1, d), lambda i, j: (0, 0)), pl.BlockSpec((1, d), lambda i, j: (0, 0))],
        out_specs=tile,
        out_shape=jax.ShapeDtypeStruct((b, l, d), F32),
        compiler_params=_cparams(("arbitrary", "arbitrary")),
    )(x, y, gate, ln_g.reshape(1, d), ln_b.reshape(1, d))


def _peer_scores_kernel(a_ref, wq_ref, k_ref, s_ref):
    q = jnp.dot(a_ref[...], wq_ref[...], preferred_element_type=F32)
    hd = PEER_DKEY // 2
    nt = (((1,), (1,)), ((), ()))
    for h in range(PEER_HEADS):
        qn = _normalize_rows(q[:, h * PEER_DKEY:(h + 1) * PEER_DKEY], 1e-6).astype(BF16)
        s_ref[h, 0] = lax.dot_general(k_ref[h, 0], qn[:, :hd], nt, preferred_element_type=F32)
        s_ref[h, 1] = lax.dot_general(k_ref[h, 1], qn[:, hd:], nt, preferred_element_type=F32)


def _peer_scores(m2, wq, subkeys, tm):
    t, d = m2.shape
    return pl.pallas_call(
        _peer_scores_kernel,
        grid=(t // tm,),
        in_specs=[pl.BlockSpec((tm, d), lambda i: (i, 0)),
                  pl.BlockSpec(wq.shape, lambda i: (0, 0)),
                  pl.BlockSpec(subkeys.shape, lambda i: (0, 0, 0, 0))],
        out_specs=pl.BlockSpec((PEER_HEADS, 2, PEER_NKEYS, tm), lambda i: (0, 0, 0, i)),
        out_shape=jax.ShapeDtypeStruct((PEER_HEADS, 2, PEER_NKEYS, t), F32),
        compiler_params=_cparams(("arbitrary",)),
    )(m2, wq, subkeys)


def _top_values(x, k):
    out = []
    for _ in range(k):
        m = jnp.max(x, axis=0, keepdims=True)
        out.append(m)
        x = jnp.where(x == m, -jnp.inf, x)
    return out


def _top_values_ranked(x, k):
    out = []
    big = 2.0 ** 100
    for a in range(k):
        m = jnp.max(x, axis=0, keepdims=True)
        out.append(m)
        x = jnp.where(x == m, -big * (1.0 + a / 64.0), x)
    rank = jnp.where(x <= -big, (x * (-1.0 / big) - 1.0) * 64.0, float(k))
    return out, rank


def _peer_stats_kernel(s_ref, rk_ref, lim_ref, p1_ref, p2_ref):
    k = PEER_TOPK
    for h in range(PEER_HEADS):
        s1 = s_ref[h, 0]
        s2 = s_ref[h, 1]
        v1, r1 = _top_values_ranked(s1, k)
        v2, rk = _top_values_ranked(s2, k)
        v2m = jnp.concatenate(v2, axis=0)
        grid = [v1[a] + v2m[:k // (a + 1)] for a in range(k)]
        n_cand = sum(g.shape[0] for g in grid)
        pad = jnp.full((-n_cand % 8, s1.shape[1]), -jnp.inf, F32)
        best = _top_values(jnp.concatenate(grid + [pad], axis=0), k)
        z = best[0] * 0.0
        for c in best:
            z = z + jnp.exp(c - best[0])
        lim = jnp.zeros_like(s1)
        for a in range(k):
            n_sel = jnp.sum(jnp.where(grid[a] >= best[k - 1], 1.0, 0.0), axis=0, keepdims=True)
            lim = jnp.where(r1 == float(a), n_sel, lim)
        lim_ref[h] = lim
        rk_ref[h] = rk.astype(rk_ref.dtype)
        p1_ref[h] = jnp.exp(s1 - v1[0])
        p2_ref[h] = (0.5 * jnp.exp(s2 - v2[0]) / z).astype(p2_ref.dtype)


def _peer_stats(scores, tm):
    t = scores.shape[-1]
    hk = pl.BlockSpec((PEER_HEADS, PEER_NKEYS, tm), lambda i: (0, 0, i))
    f32 = jax.ShapeDtypeStruct((PEER_HEADS, PEER_NKEYS, t), F32)
    b16 = jax.ShapeDtypeStruct((PEER_HEADS, PEER_NKEYS, t), BF16)
    return pl.pallas_call(
        _peer_stats_kernel,
        grid=(t // tm,),
        in_specs=[pl.BlockSpec((PEER_HEADS, 2, PEER_NKEYS, tm), lambda i: (0, 0, 0, i))],
        out_specs=[hk, hk, hk, hk],
        out_shape=[b16, f32, f32, b16],
        compiler_params=_cparams(("arbitrary",)),
    )(scores)


def _peer_dense_kernel(ht_ref, u_ref, vt_ref, rk_in, lim_ref, p1_ref, p2_in, o_ref, acc_ref, g_ref, rk_ref, p2_ref,
                       *, ne1):
    j = pl.program_id(1)

    @pl.when(j == 0)
    def _():
        acc_ref[...] = jnp.zeros_like(acc_ref)
        rk_ref[...] = rk_in[...]
        p2_ref[...] = p2_in[...]

    zero = jnp.zeros((), BF16)
    for r in range(ne1):
        e1 = j * ne1 + r
        rows = slice(r * PEER_NKEYS, (r + 1) * PEER_NKEYS)
        lim_rows = [lim_ref[h, pl.ds(e1, 1), :].astype(BF16) for h in range(PEER_HEADS)]
        p1_rows = [p1_ref[h, pl.ds(e1, 1), :].astype(BF16) for h in range(PEER_HEADS)]
        for tt in range(g_ref.shape[1] // LANES):
            ls = slice(tt * LANES, (tt + 1) * LANES)
            gacc = None
            for h in range(PEER_HEADS):
                hs = slice(h * PEER_NKEYS, (h + 1) * PEER_NKEYS)
                w = jnp.where(rk_ref[hs, ls] < lim_rows[h][:, ls], p2_ref[hs, ls] * p1_rows[h][:, ls], zero)
                gacc = w if gacc is None else gacc + w
            g_ref[rows, ls] = gacc
    act = jnp.dot(u_ref[...], ht_ref[...], preferred_element_type=F32)
    gelu2 = act * (1.0 + jnp.tanh(math.sqrt(2.0 / math.pi) * (act + 0.044715 * (act * act * act))))
    pt = gelu2.astype(BF16) * g_ref[...]
    acc_ref[...] += jnp.dot(vt_ref[...], pt, preferred_element_type=F32)

    @pl.when(j == pl.num_programs(1) - 1)
    def _():
        o_ref[...] = acc_ref[...].T


def _peer_dense(m2t, u_b, vt_b, rk, lim, p1, p2, tm, te):
    d, t = m2t.shape
    ne = u_b.shape[0]
    hk = pl.BlockSpec((PEER_HEADS, PEER_NKEYS, tm), lambda i, j: (0, 0, i))
    flat = pl.BlockSpec((PEER_HEADS * PEER_NKEYS, tm), lambda i, j: (0, i))
    return pl.pallas_call(
        functools.partial(_peer_dense_kernel, ne1=te // PEER_NKEYS),
        grid=(t // tm, ne // te),
        in_specs=[pl.BlockSpec((d, tm), lambda i, j: (0, i)),
                  pl.BlockSpec((te, d), lambda i, j: (j, 0)),
                  pl.BlockSpec((d, te), lambda i, j: (0, j)),
                  flat, hk, hk, flat],
        out_specs=pl.BlockSpec((tm, d), lambda i, j: (i, 0)),
        out_shape=jax.ShapeDtypeStruct((t, d), F32),
        scratch_shapes=[pltpu.VMEM((d, tm), F32), pltpu.VMEM((te, tm), BF16),
                        pltpu.VMEM((PEER_HEADS * PEER_NKEYS, tm), BF16),
                        pltpu.VMEM((PEER_HEADS * PEER_NKEYS, tm), BF16)],
        compiler_params=_cparams(("arbitrary", "arbitrary")),
    )(m2t, u_b, vt_b, rk.reshape(PEER_HEADS * PEER_NKEYS, t), lim, p1, p2.reshape(PEER_HEADS * PEER_NKEYS, t))


SL_X = 0
SL_B = SL_X + SSD_INNER // LANES
SL_C = SL_B + SSD_GROUPS * SSD_STATE // LANES
SL_Z = SL_C + SSD_GROUPS * SSD_STATE // LANES
SL_GH = SL_Z + SSD_INNER // LANES
SL_GS = SL_GH + D_MODEL // LANES


def kernel(x, c, ctx, c_ctx, ada_w, ada_b, w_in, hy_conv_w, hy_conv_b, hy_ffn_w1, hy_ffn_b1, hy_freq1,
           hy_ffn_w2, hy_ffn_b2, hy_freq2, hy_ffn_w3, hy_skip, ssd_conv_w, ssd_conv_b, ssd_a_log,
           ssd_dt_bias, ssd_d, ssd_norm_w, w_branch_hy, w_branch_ssd, w_out, ln1_g, ln1_b, peer_wq,
           peer_subkeys, peer_u, peer_v, ln2_g, ln2_b):
    nb, seq, d = x.shape
    lc = ctx.shape[1]
    t = nb * seq
    p = dict(hy_ffn_w1=hy_ffn_w1[0], hy_ffn_b1=hy_ffn_b1[0], hy_freq1=hy_freq1[0], hy_ffn_w2=hy_ffn_w2[0],
             hy_ffn_b2=hy_ffn_b2[0], hy_freq2=hy_freq2[0], hy_ffn_w3=hy_ffn_w3[0])

    cc = jnp.zeros((8, d), F32).at[:nb].set(c).at[nb].set(c_ctx)
    mod = _ada(cc, ada_w[0], ada_b[0])
    sh1, sc1, gt1, sh2, sc2, gt2 = [mod[:nb, i * d:(i + 1) * d].reshape(nb, 1, d) for i in range(6)]
    csh1 = mod[nb:nb + 1, 0:d].reshape(1, 1, d)
    csc1 = mod[nb:nb + 1, d:2 * d].reshape(1, 1, d)

    w = w_in[0]
    w_hy = w[:, :OFF_X].astype(BF16)
    w_rest = jnp.concatenate([w[:, OFF_X:OFF_DT], w[:, OFF_Z:]], axis=1).astype(BF16)
    w_dt = w[:, OFF_DT:OFF_Z].astype(BF16)

    m_ctx = _modulate(ctx, csc1, csh1, 256).reshape(nb * lc, d)
    pc = _matmul(m_ctx, w_rest, 512, 1024, slab=True, ncols=OFF_C - OFF_X)
    dtc = _matmul(m_ctx, w_dt, 512, LANES)
    nxb = (OFF_C - OFF_X) // LANES
    cw = ssd_conv_w[0].reshape(3, -1, LANES).transpose(1, 0, 2)
    cbias = ssd_conv_b[0].reshape(-1, 1, LANES)
    xbc_ctx = _conv3(pc, 0, nxb, cw[:nxb], cbias[:nxb], lc, True)
    gw = SSD_HPG * SSD_HEAD_DIM
    zero_state = jnp.zeros((nb, SSD_GROUPS, SSD_STATE, gw), F32)
    dt_bias = ssd_dt_bias[0]
    a_log = ssd_a_log[0]
    xs_sl = SSD_INNER // LANES
    _, _, s_f, s_b = _ssd_scan(xbc_ctx, 0, xs_sl, xs_sl, dtc, dt_bias, a_log, zero_state, zero_state, lc)

    m1 = _modulate(x, sc1, sh1, 512).reshape(t, d)
    proj_hy = _matmul(m1, w_hy, 1024, 1024, slab=True)
    proj = _matmul(m1, w_rest, 1024, 1024, out_dtype=BF16, slab=True)
    dtp = _matmul(m1, w_dt, 1024, LANES)

    hw = hy_conv_w[0].reshape(3, -1, LANES).transpose(1, 0, 2)
    hb = hy_conv_b[0].reshape(-1, 1, LANES)
    nh = HY_WIDTH // LANES
    ug = _hyconv(proj_hy, 3 * nh, hw, hb, seq)
    hspec = _hyena_filter_spectra(p, seq)
    zg = _longconv(ug, ug, 0, nh, hy_skip[0, 0], hspec, 0, seq, False)
    y_hy = _longconv(zg, ug, 0, 2 * nh, hy_skip[0, 1], hspec, 1, seq, True)

    xbc = _conv3(proj, SL_X, SL_Z - SL_X, cw, cbias, seq, True)
    y_f, y_b, _, _ = _ssd_scan(xbc, 0, xs_sl, xs_sl + SSD_GROUPS, dtp, dt_bias, a_log, s_f, s_b, seq)
    d_ch = jnp.repeat(ssd_d[0, 0] + ssd_d[0, 1], SSD_HEAD_DIM)
    y_ss = _ssd_norm(y_f, y_b, xbc, proj, SL_Z, d_ch, ssd_norm_w[0], 1024)

    merged = _branch_merge(y_hy, y_ss, w_branch_hy[0].astype(BF16), w_branch_ssd[0].astype(BF16),
                           proj, SL_GH, SL_GS, 512, 1024)
    h1, m2, m2t = _outproj_ln(merged.reshape(nb, seq, d), w_out[0].astype(BF16), x, gt1, ln1_g[0], ln1_b[0],
                              sc2, sh2, 512)

    m2f = m2.reshape(t, d)
    scores = _peer_scores(m2f, peer_wq[0].astype(BF16), peer_subkeys[0].astype(BF16), 512)
    rk, lim, p1, p2 = _peer_stats(scores, 256)
    ffn = _peer_dense(m2t, peer_u[0].astype(BF16), peer_v[0].T.astype(BF16), rk, lim, p1, p2, 512, 1024)
    return _resid_ln(h1, ffn.reshape(nb, seq, d), gt2, ln2_g[0], ln2_b[0], 512)
```

```python
import functools
import math

import numpy as np
import jax
import jax.numpy as jnp
from jax import lax
from jax.experimental import pallas as pl
from jax.experimental.pallas import tpu as pltpu

F32 = jnp.float32
BF16 = jnp.bfloat16

D_MODEL = 2048
DEPTH = 1
CTX_LEN = 256
HY_WIDTH = D_MODEL
HY_BANDS = 16
HY_EMB = 2 * HY_BANDS + 1
HY_FFN = 64
HY_MIN_DECAY = math.log(1e-2) / 1.5
HY_MAX_DECAY = math.log(1e-2) / 0.3
SSD_INNER = 2 * D_MODEL
SSD_HEAD_DIM = 64
SSD_HEADS = SSD_INNER // SSD_HEAD_DIM
SSD_STATE = 128
SSD_GROUPS = 8
SSD_HPG = SSD_HEADS // SSD_GROUPS
SSD_CHUNK = 128
OFF_X = 3 * HY_WIDTH
OFF_B = OFF_X + SSD_INNER
OFF_C = OFF_B + SSD_GROUPS * SSD_STATE
OFF_DT = OFF_C + SSD_GROUPS * SSD_STATE
OFF_Z = OFF_DT + 2 * SSD_HEADS
OFF_GATE = OFF_Z + SSD_INNER
N_COLS = OFF_GATE + 2 * D_MODEL
PEER_HEADS = 8
PEER_NKEYS = 128
PEER_EXPERTS = PEER_NKEYS * PEER_NKEYS
PEER_TOPK = 16
PEER_DKEY = 256
DEEPNORM_ALPHA = (2.0 * DEPTH) ** 0.25

LANES = 128
VMEM_LIMIT = 56 * 1024 * 1024

FFT_N1 = 128
SA_PITCH_PAD = 8
SB_PITCH = FFT_N1 + 8


def _cparams(sem, vmem=VMEM_LIMIT):
    return pltpu.CompilerParams(dimension_semantics=sem, vmem_limit_bytes=vmem)


def _silu(x):
    return x * (1.0 / (1.0 + jnp.exp(-x)))


def _sigmoid(x):
    return 1.0 / (1.0 + jnp.exp(-x))


def _softplus(x):
    return jnp.maximum(x, 0.0) + jnp.log(1.0 + jnp.exp(-jnp.abs(x)))


def _gelu_tanh(x):
    return 0.5 * x * (1.0 + jnp.tanh(math.sqrt(2.0 / math.pi) * (x + 0.044715 * (x * x * x))))


def _normalize_rows(x, eps):
    mu = jnp.mean(x, axis=-1, keepdims=True)
    xc = x - mu
    var = jnp.mean(xc * xc, axis=-1, keepdims=True)
    return xc * lax.rsqrt(var + eps)


def _ada_kernel(c_ref, w_ref, b_ref, o_ref):
    a = _silu(c_ref[...]).astype(BF16)
    o_ref[...] = jnp.dot(a, w_ref[...].astype(BF16), preferred_element_type=F32) + b_ref[...]


def _ada(cc, ada_w, ada_b):
    rows, d = cc.shape
    n = ada_w.shape[1]
    tn = 1024
    return pl.pallas_call(
        _ada_kernel,
        grid=(n // tn,),
        in_specs=[pl.BlockSpec((rows, d), lambda j: (0, 0)),
                  pl.BlockSpec((d, tn), lambda j: (0, j)),
                  pl.BlockSpec((1, tn), lambda j: (0, j))],
        out_specs=pl.BlockSpec((rows, tn), lambda j: (0, j)),
        out_shape=jax.ShapeDtypeStruct((rows, n), F32),
        compiler_params=_cparams(("arbitrary",)),
    )(cc, ada_w, ada_b.reshape(1, n))


def _mod_kernel(x_ref, sc_ref, sh_ref, o_ref):
    xn = _normalize_rows(x_ref[0], 1e-6)
    o_ref[0] = (xn * (1.0 + sc_ref[0]) + sh_ref[0]).astype(o_ref.dtype)


def _modulate(x, scale, shift, tr):
    b, l, d = x.shape
    per_batch = scale.shape[0] == b
    smap = (lambda i, j: (i, 0, 0)) if per_batch else (lambda i, j: (0, 0, 0))
    return pl.pallas_call(
        _mod_kernel,
        grid=(b, l // tr),
        in_specs=[pl.BlockSpec((1, tr, d), lambda i, j: (i, j, 0)),
                  pl.BlockSpec((1, 1, d), smap),
                  pl.BlockSpec((1, 1, d), smap)],
        out_specs=pl.BlockSpec((1, tr, d), lambda i, j: (i, j, 0)),
        out_shape=jax.ShapeDtypeStruct((b, l, d), BF16),
        compiler_params=_cparams(("arbitrary", "arbitrary")),
    )(x, scale, shift)


def _mm_kernel(a_ref, b_ref, o_ref, *, slab):
    acc = jnp.dot(a_ref[...], b_ref[...], preferred_element_type=F32)
    if slab:
        for s in range(o_ref.shape[0]):
            o_ref[s] = acc[:, s * LANES:(s + 1) * LANES].astype(o_ref.dtype)
    else:
        o_ref[...] = acc.astype(o_ref.dtype)


def _matmul(a, b, tm, tn, out_dtype=F32, slab=False, ncols=None):
    m, k = a.shape
    n = b.shape[1] if ncols is None else ncols
    if slab:
        out_shape = jax.ShapeDtypeStruct((n // LANES, m, LANES), out_dtype)
        out_spec = pl.BlockSpec((tn // LANES, tm, LANES), lambda i, j: (j, i, 0))
    else:
        out_shape = jax.ShapeDtypeStruct((m, n), out_dtype)
        out_spec = pl.BlockSpec((tm, tn), lambda i, j: (i, j))
    return pl.pallas_call(
        functools.partial(_mm_kernel, slab=slab),
        grid=(m // tm, n // tn),
        in_specs=[pl.BlockSpec((tm, k), lambda i, j: (i, 0)),
                  pl.BlockSpec((k, tn), lambda i, j: (0, j))],
        out_specs=out_spec,
        out_shape=out_shape,
        compiler_params=_cparams(("arbitrary", "arbitrary")),
    )(a, b)


def _conv3_kernel(x_ref, w_ref, b_ref, o_ref, *, silu):
    for i in range(x_ref.shape[0]):
        x = x_ref[i].astype(F32)
        l = x.shape[0]
        row = lax.broadcasted_iota(jnp.int32, x.shape, 0)
        prev = jnp.where(row == 0, 0.0, pltpu.roll(x, 1, 0))
        nxt = jnp.where(row == l - 1, 0.0, pltpu.roll(x, l - 1, 0))
        w = w_ref[i]
        y = prev * w[0:1] + x * w[1:2] + nxt * w[2:3] + b_ref[i]
        if silu:
            y = _silu(y)
        o_ref[i] = y.astype(o_ref.dtype)


def _conv3(p_slab, slab0, nslab, w, bias, seq, silu):
    rows = p_slab.shape[1]
    nb = rows // seq
    sp = 4
    return pl.pallas_call(
        functools.partial(_conv3_kernel, silu=silu),
        grid=(nslab // sp, nb),
        in_specs=[pl.BlockSpec((sp, seq, LANES), lambda s, b: (slab0 // sp + s, b, 0)),
                  pl.BlockSpec((sp, 3, LANES), lambda s, b: (s, 0, 0)),
                  pl.BlockSpec((sp, 1, LANES), lambda s, b: (s, 0, 0))],
        out_specs=pl.BlockSpec((sp, seq, LANES), lambda s, b: (s, b, 0)),
        out_shape=jax.ShapeDtypeStruct((nslab, rows, LANES), BF16),
        compiler_params=_cparams(("arbitrary", "arbitrary")),
    )(p_slab, w, bias)


def _hyconv_kernel(x_ref, w_ref, b_ref, o_ref, *, seq, nb):
    nj = seq // FFT_N1

    def gather(n1):
        return jnp.concatenate(
            [x_ref[0, pl.ds(n1 + b * seq, nj, stride=FFT_N1), :] for b in range(nb)], axis=0)

    w = w_ref[0]
    w0, w1, w2, bias = w[0:1], w[1:2], w[2:3], b_ref[0]
    jrow = lax.broadcasted_iota(jnp.int32, (nb * nj, LANES), 0) % nj

    def shift_down(g):
        return jnp.where(jrow == 0, 0.0, pltpu.roll(g, 1, 0))

    def shift_up(g):
        return jnp.where(jrow == nj - 1, 0.0, pltpu.roll(g, nb * nj - 1, 0))

    g_first = gather(0)
    g_second = gather(1)
    g_last = gather(FFT_N1 - 1)
    dt = o_ref.dtype
    o_ref[0, 0] = (shift_down(g_last) * w0 + g_first * w1 + g_second * w2 + bias).astype(dt)

    def body(n1, carry):
        g_prev, g_cur = carry
        g_next = gather(n1 + 1)
        o_ref[0, n1] = (g_prev * w0 + g_cur * w1 + g_next * w2 + bias).astype(dt)
        return g_cur, g_next

    g_prev, g_cur = lax.fori_loop(1, FFT_N1 - 1, body, (g_first, g_second), unroll=6)
    o_ref[0, FFT_N1 - 1] = (g_prev * w0 + g_cur * w1 + shift_up(g_first) * w2 + bias).astype(dt)


def _hyconv(p_slab, nslab, w, bias, seq):
    rows = p_slab.shape[1]
    nb = rows // seq
    nj = seq // FFT_N1
    return pl.pallas_call(
        functools.partial(_hyconv_kernel, seq=seq, nb=nb),
        grid=(nslab,),
        in_specs=[pl.BlockSpec((1, rows, LANES), lambda s: (s, 0, 0)),
                  pl.BlockSpec((1, 3, LANES), lambda s: (s, 0, 0)),
                  pl.BlockSpec((1, 1, LANES), lambda s: (s, 0, 0))],
        out_specs=pl.BlockSpec((1, FFT_N1, nb * nj, LANES), lambda s: (s, 0, 0, 0)),
        out_shape=jax.ShapeDtypeStruct((nslab, FFT_N1, nb * nj, LANES), BF16),
        compiler_params=_cparams(("arbitrary",)),
    )(p_slab, w, bias)


@functools.lru_cache(maxsize=None)
def _dft_tables(seq):
    nj = seq // FFT_N1
    n2 = nj + nj // 2
    n = FFT_N1 * n2
    out0 = nj // 2
    n1 = np.arange(FFT_N1, dtype=np.float64)
    k2 = np.arange(n2, dtype=np.float64)
    jn = np.arange(nj, dtype=np.float64)
    ang = -2.0 * np.pi * (k2[None, :, None] * jn[None, None, :] / n2 + n1[:, None, None] * k2[None, :, None] / n)
    cr, ci = np.cos(ang), np.sin(ang)
    fa = np.concatenate([np.concatenate([cr, -ci], axis=2), np.concatenate([ci, cr], axis=2)], axis=1)
    k1 = np.arange(FFT_N1, dtype=np.float64)
    angb = -2.0 * np.pi * np.outer(k1, n1) / FFT_N1
    dr, di = np.cos(angb), np.sin(angb)
    fb = np.block([[dr, -di], [di, dr]])
    fbi = np.block([[dr, di], [-di, dr]])
    angi = 2.0 * np.pi * (n1[:, None, None] * k2[None, None, :] / n + (out0 + jn)[None, :, None] * k2[None, None, :] / n2)
    er, ei = np.cos(angi) / n, np.sin(angi) / n
    fai = np.concatenate([np.concatenate([er, -ei], axis=2), np.concatenate([ei, er], axis=2)], axis=1)
    return (fa.astype(np.float32), fb.astype(np.float32), fbi.astype(np.float32), fai.astype(np.float32), n2)


@functools.lru_cache(maxsize=None)
def _filter_consts(seq):
    t = np.linspace(0.0, 1.0, seq, dtype=np.float32)[:, None].astype(np.float64)
    w = 2.0 * np.pi * np.arange(seq, dtype=np.float32)[:, None].astype(np.float64) / seq
    bands = np.linspace(1e-4, HY_BANDS - 1, HY_BANDS, dtype=np.float32)[None, :].astype(np.float64)
    feats = np.concatenate([t, np.cos(bands * w), -np.sin(bands * w)], axis=-1)
    feats_p = np.zeros((seq, LANES), np.float32)
    feats_p[:, :HY_EMB] = feats
    dist = (np.abs(np.arange(seq) - seq // 2).astype(np.float32) / np.float32(seq / 2.0)).reshape(seq, 1)
    deltas = np.abs(np.linspace(HY_MIN_DECAY, HY_MAX_DECAY, HY_WIDTH, dtype=np.float32))
    return feats_p, dist.astype(np.float32), deltas.astype(np.float32)


def _dot_f32(a, b):
    def split(x):
        hi = x.astype(BF16)
        r1 = x - hi.astype(F32)
        mid = r1.astype(BF16)
        lo = (r1 - mid.astype(F32)).astype(BF16)
        return hi, mid, lo
    a0, a1, a2 = split(a)
    b0, b1, b2 = split(b)
    d = functools.partial(jnp.dot, preferred_element_type=F32)
    return (d(a0, b0) + (d(a0, b1) + d(a1, b0)) + (d(a0, b2) + d(a1, b1) + d(a2, b0)))


def _filter_kernel(feats_ref, w1_ref, b1_ref, f1_ref, w2_ref, b2_ref, f2_ref, w3_ref, dist_ref, delta_ref,
                   fa_ref, fb_ref, o_ref, hid_ref, h_ref, sar_ref, sai_ref, *, seq, n2, pitch):
    nj = seq // FFT_N1

    @pl.when((pl.program_id(0) == 0) & (pl.program_id(1) == 0))
    def _():
        h1 = jnp.sin(f1_ref[...] * (_dot_f32(feats_ref[...], w1_ref[...]) + b1_ref[...]))
        hid_ref[...] = jnp.sin(f2_ref[...] * (_dot_f32(h1, w2_ref[...]) + b2_ref[...])).astype(BF16)

    ns = h_ref.shape[0]
    lanes = lambda f: jnp.concatenate([f(s) for s in range(ns)], axis=1)
    w3 = lanes(lambda s: w3_ref[0, s]).astype(BF16)
    h = jnp.dot(hid_ref[...], w3, preferred_element_type=F32)
    h = h * jnp.exp(-dist_ref[...] * lanes(lambda s: delta_ref[s]))
    h = h / (jnp.sum(jnp.abs(h), axis=0, keepdims=True) + 1e-6)
    for s in range(ns):
        h_ref[s] = h[:, s * LANES:(s + 1) * LANES]

    def stage_a(n1, c):
        g = lanes(lambda s: h_ref[s, pl.ds(n1, nj, stride=FFT_N1), :]).astype(BF16)
        y = jnp.dot(fa_ref[n1], g, preferred_element_type=F32)
        base = pl.multiple_of(n1 * pitch, 8)
        for s in range(ns):
            sar_ref[s, pl.ds(base, n2), :] = y[:n2, s * LANES:(s + 1) * LANES]
            sai_ref[s, pl.ds(base, n2), :] = y[n2:, s * LANES:(s + 1) * LANES]
        return c

    lax.fori_loop(0, FFT_N1, stage_a, 0, unroll=8)

    def stage_b(k2, c):
        yr = lanes(lambda s: sar_ref[s, pl.ds(k2, FFT_N1, stride=pitch), :])
        yi = lanes(lambda s: sai_ref[s, pl.ds(k2, FFT_N1, stride=pitch), :])
        op = jnp.concatenate([yr, yi], axis=0).astype(BF16)
        res = jnp.dot(fb_ref[...], op, preferred_element_type=F32).astype(o_ref.dtype)
        for s in range(ns):
            o_ref[0, s, k2] = res[:, s * LANES:(s + 1) * LANES]
        return c

    lax.fori_loop(0, n2, stage_b, 0, unroll=4)


def _hyena_filter_spectra(p, seq):
    fa, fb, _, _, n2 = _dft_tables(seq)
    nj = seq // FFT_N1
    pitch = n2 + SA_PITCH_PAD
    feats, dist, deltas = _filter_consts(seq)
    nsl = HY_WIDTH // LANES

    def padk(w, rows):
        return jnp.zeros((rows, w.shape[1]), F32).at[:w.shape[0]].set(w)

    def padn(v):
        return jnp.zeros((1, LANES), F32).at[0, :v.shape[0]].set(v)

    w1 = jnp.zeros((LANES, LANES), F32).at[:HY_EMB, :HY_FFN].set(p['hy_ffn_w1'])
    w2 = jnp.zeros((LANES, LANES), F32).at[:HY_FFN, :HY_FFN].set(p['hy_ffn_w2'])
    w3 = padk(p['hy_ffn_w3'], LANES).reshape(LANES, 2, nsl, LANES).transpose(1, 2, 0, 3)
    fa_real = jnp.asarray(fa[:, :, :nj]).astype(BF16)
    const = lambda shape: pl.BlockSpec(shape, lambda o, s: (0,) * len(shape))
    ns = 2
    return pl.pallas_call(
        functools.partial(_filter_kernel, seq=seq, n2=n2, pitch=pitch),
        grid=(2, nsl // ns),
        in_specs=[const((seq, LANES)), const((LANES, LANES)), const((1, LANES)), const((1, LANES)),
                  const((LANES, LANES)), const((1, LANES)), const((1, LANES)),
                  pl.BlockSpec((1, ns, LANES, LANES), lambda o, s: (o, s, 0, 0)),
                  const((seq, 1)),
                  pl.BlockSpec((ns, 1, LANES), lambda o, s: (s, 0, 0)),
                  const((FFT_N1, 2 * n2, nj)), const((2 * FFT_N1, 2 * FFT_N1))],
        out_specs=pl.BlockSpec((1, ns, n2, 2 * FFT_N1, LANES), lambda o, s: (o, s, 0, 0, 0)),
        out_shape=jax.ShapeDtypeStruct((2, nsl, n2, 2 * FFT_N1, LANES), BF16),
        scratch_shapes=[pltpu.VMEM((seq, LANES), BF16), pltpu.VMEM((ns, seq, LANES), F32),
                        pltpu.VMEM((ns, FFT_N1 * pitch, LANES), F32),
                        pltpu.VMEM((ns, FFT_N1 * pitch, LANES), F32)],
        compiler_params=_cparams(("arbitrary", "arbitrary")),
    )(jnp.asarray(feats), w1, padn(p['hy_ffn_b1']), padn(p['hy_freq1']), w2, padn(p['hy_ffn_b2']),
      padn(p['hy_freq2']), w3, jnp.asarray(dist), jnp.asarray(deltas).reshape(nsl, 1, LANES),
      fa_real, jnp.asarray(fb).astype(BF16))


def _longconv_kernel(x_ref, g_ref, skip_ref, h_ref, fa_ref, fb_ref, fbi_ref, fai_ref, o_ref,
                     sar_ref, sai_ref, sbr_ref, sbi_ref, *, seq, n2, pitch, natural_out):
    nj = seq // FFT_N1

    def stage_a(n1, c):
        y = jnp.dot(fa_ref[n1], x_ref[0, n1].astype(BF16), preferred_element_type=F32)
        base = pl.multiple_of(n1 * pitch, 8)
        sar_ref[pl.ds(base, n2), :] = y[:n2]
        sai_ref[pl.ds(base, n2), :] = y[n2:]
        return c

    lax.fori_loop(0, FFT_N1, stage_a, 0, unroll=8)

    def stage_b(kp, c):
        ks = (2 * kp, 2 * kp + 1)
        yr = jnp.concatenate([sar_ref[pl.ds(k, FFT_N1, stride=pitch), :] for k in ks], axis=1)
        yi = jnp.concatenate([sai_ref[pl.ds(k, FFT_N1, stride=pitch), :] for k in ks], axis=1)
        xf = jnp.dot(fb_ref[...], jnp.concatenate([yr, yi], axis=0).astype(BF16), preferred_element_type=F32)
        xr, xi = xf[:FFT_N1], xf[FFT_N1:]
        hf = jnp.concatenate([h_ref[0, k] for k in ks], axis=1).astype(F32)
        hr, hi = hf[:FFT_N1], hf[FFT_N1:]
        z = jnp.concatenate([xr * hr - xi * hi, xr * hi + xi * hr], axis=0).astype(BF16)
        w = jnp.dot(fbi_ref[...], z, preferred_element_type=F32)
        for q, k in enumerate(ks):
            base = pl.multiple_of(k * SB_PITCH, 8)
            sbr_ref[pl.ds(base, FFT_N1), :] = w[:FFT_N1, q * LANES:(q + 1) * LANES]
            sbi_ref[pl.ds(base, FFT_N1), :] = w[FFT_N1:, q * LANES:(q + 1) * LANES]
        return c

    lax.fori_loop(0, n2 // 2, stage_b, 0, unroll=2)

    skip = skip_ref[0]

    def stage_c(n1, c):
        wr = sbr_ref[pl.ds(n1, n2, stride=SB_PITCH), :]
        wi = sbi_ref[pl.ds(n1, n2, stride=SB_PITCH), :]
        y = jnp.dot(fai_ref[n1], jnp.concatenate([wr, wi], axis=0).astype(BF16), preferred_element_type=F32)
        res = g_ref[0, n1].astype(F32) * (y + x_ref[0, n1].astype(F32) * skip)
        if natural_out:
            for b in range(res.shape[0] // nj):
                o_ref[0, pl.ds(n1 + b * seq, nj, stride=FFT_N1), :] = res[b * nj:(b + 1) * nj]
        else:
            o_ref[0, n1] = res.astype(o_ref.dtype)
        return c

    lax.fori_loop(0, FFT_N1, stage_c, 0, unroll=8)


def _longconv(xg, gg, slab_x, slab_g, skip, hspec, order, seq, natural_out):
    fa, fb, fbi, fai, n2 = _dft_tables(seq)
    pitch = n2 + SA_PITCH_PAD
    nsl = HY_WIDTH // LANES
    rows = xg.shape[2]
    nb = rows // (seq // FFT_N1)
    const = lambda shape: pl.BlockSpec(shape, lambda s: (0,) * len(shape))
    if natural_out:
        out_shape = jax.ShapeDtypeStruct((nsl, nb * seq, LANES), F32)
        out_spec = pl.BlockSpec((1, nb * seq, LANES), lambda s: (s, 0, 0))
    else:
        out_shape = jax.ShapeDtypeStruct((nsl, FFT_N1, rows, LANES), BF16)
        out_spec = pl.BlockSpec((1, FFT_N1, rows, LANES), lambda s: (s, 0, 0, 0))
    return pl.pallas_call(
        functools.partial(_longconv_kernel, seq=seq, n2=n2, pitch=pitch, natural_out=natural_out),
        grid=(nsl,),
        in_specs=[pl.BlockSpec((1, FFT_N1, rows, LANES), lambda s: (slab_x + s, 0, 0, 0)),
                  pl.BlockSpec((1, FFT_N1, rows, LANES), lambda s: (slab_g + s, 0, 0, 0)),
                  pl.BlockSpec((1, 1, LANES), lambda s: (s, 0, 0)),
                  pl.BlockSpec((None, 1, n2, 2 * FFT_N1, LANES), lambda s: (order, s, 0, 0, 0)),
                  const(fa.shape), const(fb.shape), const(fbi.shape), const(fai.shape)],
        out_specs=out_spec,
        out_shape=out_shape,
        scratch_shapes=[pltpu.VMEM((FFT_N1 * pitch, LANES), F32), pltpu.VMEM((FFT_N1 * pitch, LANES), F32),
                        pltpu.VMEM((n2 * SB_PITCH, LANES), F32), pltpu.VMEM((n2 * SB_PITCH, LANES), F32)],
        compiler_params=_cparams(("arbitrary",)),
    )(xg, gg, skip.reshape(nsl, 1, LANES), hspec,
      jnp.asarray(fa).astype(BF16), jnp.asarray(fb).astype(BF16),
      jnp.asarray(fbi).astype(BF16), jnp.asarray(fai).astype(BF16))


def _dot_exact_lhs(lhs_bf16, x):
    hi = x.astype(BF16)
    r1 = x - hi.astype(F32)
    mid = r1.astype(BF16)
    lo = (r1 - mid.astype(F32)).astype(BF16)
    d = functools.partial(jnp.dot, preferred_element_type=F32)
    return d(lhs_bf16, hi) + (d(lhs_bf16, mid) + d(lhs_bf16, lo))


def _ssd_chunk_groups(x_ref, b_ref, c_ref, dt_ref, bias_ref, alog_ref, e_ref, y_ref, st_ref, rev):
    gw = SSD_HPG * SSD_HEAD_DIM
    dot = functools.partial(jnp.dot, preferred_element_type=F32)
    off = SSD_HEADS if rev else 0
    dt_all = _softplus(dt_ref[...] + bias_ref[...])
    a_all = dt_all * (-jnp.exp(alog_ref[...]))
    r_i = lax.broadcasted_iota(jnp.int32, (SSD_CHUNK, SSD_CHUNK), 0)
    c_i = lax.broadcasted_iota(jnp.int32, (SSD_CHUNK, SSD_CHUNK), 1)
    valid = (c_i >= r_i) if rev else (c_i <= r_i)
    tri = jnp.where(valid, 1.0, 0.0).astype(BF16)
    acum = _dot_exact_lhs(tri, a_all)
    src_t = (acum - jnp.log(dt_all)).T
    tot = acum[0:1, :] if rev else acum[SSD_CHUNK - 1:SSD_CHUNK, :]
    dtw_b = (dt_all * jnp.exp(tot - acum)).astype(BF16)
    cd8 = jnp.broadcast_to(jnp.exp(tot), (8, LANES))
    cd_hi = cd8.astype(BF16)
    cd_r = cd8 - cd_hi.astype(F32)
    cd_mid = cd_r.astype(BF16)
    cd_lo = (cd_r - cd_mid.astype(F32)).astype(BF16)
    lo_half = c_i < SSD_HEAD_DIM
    zero_b = jnp.zeros((), BF16)

    def group(g):
        bt = b_ref[g].astype(F32).T.astype(BF16)
        cg = c_ref[g].astype(BF16)
        cb = dot(cg, bt)
        for sl in range(gw // LANES):
            col = slice(sl * LANES, (sl + 1) * LANES)
            e_s = e_ref[:, g * gw + sl * LANES:g * gw + (sl + 1) * LANES]
            xb = x_ref[(gw // LANES) * g + sl]
            ms, ea = [], []
            for hh in range(2):
                h = off + SSD_HPG * g + 2 * sl + hh
                a_col = jnp.broadcast_to(acum[:, h:h + 1], (SSD_CHUNK, SSD_CHUNK))
                seg = jnp.where(valid, a_col - src_t[h:h + 1, :], -1e30)
                ms.append((cb * jnp.exp(seg)).astype(BF16))
                ea.append(jnp.exp(a_col))
            rhs = jnp.concatenate([jnp.where(lo_half, xb, zero_b), jnp.where(lo_half, zero_b, xb)], axis=0)
            yd = dot(jnp.concatenate(ms, axis=1), rhs)
            sgs = st_ref[g, :, col]
            yoff = dot(cg, sgs.astype(BF16)) * jnp.where(lo_half, ea[0], ea[1])
            y_ref[g, :, col] = (yd + yoff).astype(y_ref.dtype)
            wm = (xb.astype(F32) * dot(dtw_b, e_s)).astype(BF16)
            cd = (dot(cd_hi, e_s) + (dot(cd_mid, e_s) + dot(cd_lo, e_s)))[0:1]
            st_ref[g, :, col] = sgs * cd + dot(bt, wm)

    return group


def _ssd_kernel(xf_ref, bf_ref, cf_ref, dtf_ref, xr_ref, br_ref, cr_ref, dtr_ref, bias_ref, alog_ref, ef_ref, er_ref,
                initf_ref, initr_ref, yf_ref, yr_ref, finf_ref, finr_ref, stf_ref, str_ref, *, nc):
    ci = pl.program_id(1)

    @pl.when(ci == 0)
    def _():
        stf_ref[...] = initf_ref[0]
        str_ref[...] = initr_ref[0]

    fwd = _ssd_chunk_groups(xf_ref, bf_ref, cf_ref, dtf_ref, bias_ref, alog_ref, ef_ref, yf_ref, stf_ref, False)
    bwd = _ssd_chunk_groups(xr_ref, br_ref, cr_ref, dtr_ref, bias_ref, alog_ref, er_ref, yr_ref, str_ref, True)
    for g in range(SSD_GROUPS):
        fwd(g)
        bwd(g)

    @pl.when(ci == nc - 1)
    def _():
        finf_ref[0] = stf_ref[...]
        finr_ref[0] = str_ref[...]


def _ssd_scan(xbc, slab_x, slab_b, slab_c, dt, dt_bias, a_log, init_f, init_r, seq):
    rows = xbc.shape[1]
    nb = rows // seq
    nc = seq // SSD_CHUNK
    gw = SSD_HPG * SSD_HEAD_DIM
    xs = SSD_INNER // LANES
    pos_f = lambda b, c: b * nc + c
    pos_r = lambda b, c: b * nc + (nc - 1 - c)
    expand = np.zeros((2, 2 * SSD_HEADS, SSD_INNER), np.float32)
    for r in range(2):
        for h in range(SSD_HEADS):
            expand[r, r * SSD_HEADS + h, h * SSD_HEAD_DIM:(h + 1) * SSD_HEAD_DIM] = 1.0

    def data_specs(pos):
        return [pl.BlockSpec((xs, SSD_CHUNK, LANES), lambda b, c: (slab_x // xs, pos(b, c), 0)),
                pl.BlockSpec((SSD_GROUPS, SSD_CHUNK, LANES), lambda b, c: (slab_b // SSD_GROUPS, pos(b, c), 0)),
                pl.BlockSpec((SSD_GROUPS, SSD_CHUNK, LANES), lambda b, c: (slab_c // SSD_GROUPS, pos(b, c), 0)),
                pl.BlockSpec((SSD_CHUNK, LANES), lambda b, c: (pos(b, c), 0))]

    vec = pl.BlockSpec((1, LANES), lambda b, c: (0, 0))
    emat = pl.BlockSpec((2 * SSD_HEADS, SSD_INNER), lambda b, c: (0, 0))
    state = pl.BlockSpec((1, SSD_GROUPS, SSD_STATE, gw), lambda b, c: (b, 0, 0, 0))
    y_sds = jax.ShapeDtypeStruct((SSD_GROUPS, rows, gw), BF16)
    s_sds = jax.ShapeDtypeStruct((nb, SSD_GROUPS, SSD_STATE, gw), F32)
    e_b = jnp.asarray(expand).astype(BF16)
    return pl.pallas_call(
        functools.partial(_ssd_kernel, nc=nc),
        grid=(nb, nc),
        in_specs=data_specs(pos_f) + data_specs(pos_r) + [vec, vec, emat, emat, state, state],
        out_specs=[pl.BlockSpec((SSD_GROUPS, SSD_CHUNK, gw), lambda b, c: (0, pos_f(b, c), 0)),
                   pl.BlockSpec((SSD_GROUPS, SSD_CHUNK, gw), lambda b, c: (0, pos_r(b, c), 0)),
                   state, state],
        out_shape=[y_sds, y_sds, s_sds, s_sds],
        scratch_shapes=[pltpu.VMEM((SSD_GROUPS, SSD_STATE, gw), F32), pltpu.VMEM((SSD_GROUPS, SSD_STATE, gw), F32)],
        compiler_params=_cparams(("arbitrary", "arbitrary")),
    )(xbc, xbc, xbc, dt, xbc, xbc, xbc, dt, dt_bias.reshape(1, LANES), a_log.reshape(1, LANES),
      e_b[0], e_b[1], init_f, init_r)


def _ssdnorm_kernel(yf_ref, yb_ref, x_ref, z_ref, d_ref, w_ref, o_ref):
    nsl = x_ref.shape[0]
    x = jnp.concatenate([x_ref[i] for i in range(nsl)], axis=1).astype(F32)
    z = jnp.concatenate([z_ref[i] for i in range(nsl)], axis=1).astype(F32)
    y = yf_ref[0].astype(F32) + yb_ref[0].astype(F32) + x * d_ref[0]
    gx = y * _silu(z)
    ms = jnp.mean(gx * gx, axis=-1, keepdims=True)
    o_ref[...] = (gx * lax.rsqrt(ms + 1e-5) * w_ref[0]).astype(o_ref.dtype)


def _ssd_norm(yf, yb, xbc, proj, slab_z, d_ch, w, tr):
    g, rows, gw = yf.shape
    nsl = gw // LANES
    return pl.pallas_call(
        _ssdnorm_kernel,
        grid=(g, rows // tr),
        in_specs=[pl.BlockSpec((1, tr, gw), lambda q, i: (q, i, 0)),
                  pl.BlockSpec((1, tr, gw), lambda q, i: (q, i, 0)),
                  pl.BlockSpec((nsl, tr, LANES), lambda q, i: (q, i, 0)),
                  pl.BlockSpec((nsl, tr, LANES), lambda q, i: (slab_z // nsl + q, i, 0)),
                  pl.BlockSpec((1, 1, gw), lambda q, i: (q, 0, 0)),
                  pl.BlockSpec((1, 1, gw), lambda q, i: (q, 0, 0))],
        out_specs=pl.BlockSpec((tr, gw), lambda q, i: (i, q)),
        out_shape=jax.ShapeDtypeStruct((rows, g * gw), BF16),
        compiler_params=_cparams(("arbitrary", "arbitrary")),
    )(yf, yb, xbc, proj, d_ch.reshape(g, 1, gw), w.reshape(g, 1, gw))


def _merge_kernel(yhy_ref, yss_ref, whh_ref, wss_ref, gh_ref, gs_ref, o_ref):
    a1 = jnp.concatenate([yhy_ref[i] for i in range(yhy_ref.shape[0])], axis=1).astype(BF16)
    p1 = jnp.dot(a1, whh_ref[...], preferred_element_type=F32)
    p2 = jnp.dot(yss_ref[...], wss_ref[...], preferred_element_type=F32)
    gh = jnp.concatenate([gh_ref[i] for i in range(gh_ref.shape[0])], axis=1).astype(F32)
    gs = jnp.concatenate([gs_ref[i] for i in range(gs_ref.shape[0])], axis=1).astype(F32)
    o_ref[...] = (_sigmoid(gh) * p1 + _sigmoid(gs) * p2).astype(o_ref.dtype)


def _branch_merge(yhy, yss, w_hy, w_ssd, proj, slab_gh, slab_gs, tm, tn):
    nsl_k = yhy.shape[0]
    rows = yhy.shape[1]
    n = w_hy.shape[1]
    ts = tn // LANES
    return pl.pallas_call(
        _merge_kernel,
        grid=(rows // tm, n // tn),
        in_specs=[pl.BlockSpec((nsl_k, tm, LANES), lambda i, j: (0, i, 0)),
                  pl.BlockSpec((tm, yss.shape[1]), lambda i, j: (i, 0)),
                  pl.BlockSpec((w_hy.shape[0], tn), lambda i, j: (0, j)),
                  pl.BlockSpec((w_ssd.shape[0], tn), lambda i, j: (0, j)),
                  pl.BlockSpec((ts, tm, LANES), lambda i, j: (slab_gh // ts + j, i, 0)),
                  pl.BlockSpec((ts, tm, LANES), lambda i, j: (slab_gs // ts + j, i, 0))],
        out_specs=pl.BlockSpec((tm, tn), lambda i, j: (i, j)),
        out_shape=jax.ShapeDtypeStruct((rows, n), BF16),
        compiler_params=_cparams(("arbitrary", "arbitrary")),
    )(yhy, yss, w_hy, w_ssd, proj, proj)


def _outproj_ln_kernel(a_ref, w_ref, x_ref, gt_ref, g_ref, b_ref, sc_ref, sh_ref, h_ref, m_ref, mt_ref):
    y = jnp.dot(a_ref[0], w_ref[...], preferred_element_type=F32)
    h = _normalize_rows(DEEPNORM_ALPHA * x_ref[0] + gt_ref[0] * y, 1e-6) * g_ref[...] + b_ref[...]
    h_ref[0] = h
    m = _normalize_rows(h, 1e-6) * (1.0 + sc_ref[0]) + sh_ref[0]
    m_ref[0] = m.astype(m_ref.dtype)
    mt_ref[...] = m.T.astype(mt_ref.dtype)


def _outproj_ln(merged, w_out, x, gate, ln_g, ln_b, scale2, shift2, tm):
    b, l, d = x.shape
    nt = l // tm
    vec = pl.BlockSpec((1, 1, d), lambda i, j: (i, 0, 0))
    cvec = pl.BlockSpec((1, d), lambda i, j: (0, 0))
    tile = pl.BlockSpec((1, tm, d), lambda i, j: (i, j, 0))
    return pl.pallas_call(
        _outproj_ln_kernel,
        grid=(b, nt),
        in_specs=[tile, pl.BlockSpec((d, d), lambda i, j: (0, 0)), tile, vec, cvec, cvec, vec, vec],
        out_specs=[tile, tile, pl.BlockSpec((d, tm), lambda i, j: (0, i * nt + j))],
        out_shape=[jax.ShapeDtypeStruct((b, l, d), F32), jax.ShapeDtypeStruct((b, l, d), BF16),
                   jax.ShapeDtypeStruct((d, b * l), BF16)],
        compiler_params=_cparams(("arbitrary", "arbitrary")),
    )(merged, w_out, x, gate, ln_g.reshape(1, d), ln_b.reshape(1, d), scale2, shift2)


def _resid_ln_kernel(x_ref, y_ref, gt_ref, g_ref, b_ref, o_ref):
    o_ref[0] = _normalize_rows(DEEPNORM_ALPHA * x_ref[0] + gt_ref[0] * y_ref[0], 1e-6) * g_ref[...] + b_ref[...]


def _resid_ln(x, y, gate, ln_g, ln_b, tm):
    b, l, d = x.shape
    tile = pl.BlockSpec((1, tm, d), lambda i, j: (i, j, 0))
    return pl.pallas_call(
        _resid_ln_kernel,
        grid=(b, l // tm),
        in_specs=[tile, tile, pl.BlockSpec((1, 1, d), lambda i, j: (i, 0, 0)),
                  pl.BlockSpec((1, d), lambda i, j: (0, 0)), pl.BlockSpec((1, d), lambda i, j: (0, 0))],
        out_specs=tile,
        out_shape=jax.ShapeDtypeStruct((b, l, d), F32),
        compiler_params=_cparams(("arbitrary", "arbitrary")),
    )(x, y, gate, ln_g.reshape(1, d), ln_b.reshape(1, d))


def _peer_scores_kernel(a_ref, wq_ref, k_ref, s_ref):
    q = jnp.dot(a_ref[...], wq_ref[...], preferred_element_type=F32)
    hd = PEER_DKEY // 2
    nt = (((1,), (1,)), ((), ()))
    for h in range(PEER_HEADS):
        qn = _normalize_rows(q[:, h * PEER_DKEY:(h + 1) * PEER_DKEY], 1e-6).astype(BF16)
        s_ref[h, 0] = lax.dot_general(k_ref[h, 0], qn[:, :hd], nt, preferred_element_type=F32)
        s_ref[h, 1] = lax.dot_general(k_ref[h, 1], qn[:, hd:], nt, preferred_element_type=F32)


def _peer_scores(m2, wq, subkeys, tm):
    t, d = m2.shape
    return pl.pallas_call(
        _peer_scores_kernel,
        grid=(t // tm,),
        in_specs=[pl.BlockSpec((tm, d), lambda i: (i, 0)),
                  pl.BlockSpec(wq.shape, lambda i: (0, 0)),
                  pl.BlockSpec(subkeys.shape, lambda i: (0, 0, 0, 0))],
        out_specs=pl.BlockSpec((PEER_HEADS, 2, PEER_NKEYS, tm), lambda i: (0, 0, 0, i)),
        out_shape=jax.ShapeDtypeStruct((PEER_HEADS, 2, PEER_NKEYS, t), F32),
        compiler_params=_cparams(("arbitrary",)),
    )(m2, wq, subkeys)


def _top_values(x, k):
    out = []
    for _ in range(k):
        m = jnp.max(x, axis=0, keepdims=True)
        out.append(m)
        x = jnp.where(x == m, -jnp.inf, x)
    return out


def _top_values_ranked(x, k):
    out = []
    big = 2.0 ** 100
    for a in range(k):
        m = jnp.max(x, axis=0, keepdims=True)
        out.append(m)
        x = jnp.where(x == m, -big * (1.0 + a / 64.0), x)
    rank = jnp.where(x <= -big, (x * (-1.0 / big) - 1.0) * 64.0, float(k))
    return out, rank


def _peer_stats_kernel(s_ref, rk_ref, lim_ref, p1_ref, p2_ref):
    k = PEER_TOPK
    for h in range(PEER_HEADS):
        s1 = s_ref[h, 0]
        s2 = s_ref[h, 1]
        v1, r1 = _top_values_ranked(s1, k)
        v2, rk = _top_values_ranked(s2, k)
        v2m = jnp.concatenate(v2, axis=0)
        grid = [v1[a] + v2m[:k // (a + 1)] for a in range(k)]
        n_cand = sum(g.shape[0] for g in grid)
        pad = jnp.full((-n_cand % 8, s1.shape[1]), -jnp.inf, F32)
        best = _top_values(jnp.concatenate(grid + [pad], axis=0), k)
        z = best[0] * 0.0
        for c in best:
            z = z + jnp.exp(c - best[0])
        lim = jnp.zeros_like(s1)
        for a in range(k):
            n_sel = jnp.sum(jnp.where(grid[a] >= best[k - 1], 1.0, 0.0), axis=0, keepdims=True)
            lim = jnp.where(r1 == float(a), n_sel, lim)
        lim_ref[h] = lim
        rk_ref[h] = rk.astype(rk_ref.dtype)
        p1_ref[h] = jnp.exp(s1 - v1[0])
        p2_ref[h] = (0.5 * jnp.exp(s2 - v2[0]) / z).astype(p2_ref.dtype)


def _peer_stats(scores, tm):
    t = scores.shape[-1]
    hk = pl.BlockSpec((PEER_HEADS, PEER_NKEYS, tm), lambda i: (0, 0, i))
    f32 = jax.ShapeDtypeStruct((PEER_HEADS, PEER_NKEYS, t), F32)
    b16 = jax.ShapeDtypeStruct((PEER_HEADS, PEER_NKEYS, t), BF16)
    return pl.pallas_call(
        _peer_stats_kernel,
        grid=(t // tm,),
        in_specs=[pl.BlockSpec((PEER_HEADS, 2, PEER_NKEYS, tm), lambda i: (0, 0, 0, i))],
        out_specs=[hk, hk, hk, hk],
        out_shape=[b16, f32, f32, b16],
        compiler_params=_cparams(("arbitrary",)),
    )(scores)


def _peer_dense_kernel(ht_ref, u_ref, vt_ref, rk_in, lim_ref, p1_ref, p2_in, o_ref, acc_ref, g_ref, rk_ref, p2_ref,
                       *, ne1):
    j = pl.program_id(1)

    @pl.when(j == 0)
    def _():
        acc_ref[...] = jnp.zeros_like(acc_ref)
        rk_ref[...] = rk_in[...]
        p2_ref[...] = p2_in[...]

    zero = jnp.zeros((), BF16)
    for r in range(ne1):
        e1 = j * ne1 + r
        rows = slice(r * PEER_NKEYS, (r + 1) * PEER_NKEYS)
        lim_rows = [lim_ref[h, pl.ds(e1, 1), :].astype(BF16) for h in range(PEER_HEADS)]
        p1_rows = [p1_ref[h, pl.ds(e1, 1), :].astype(BF16) for h in range(PEER_HEADS)]
        for tt in range(g_ref.shape[1] // LANES):
            ls = slice(tt * LANES, (tt + 1) * LANES)
            gacc = None
            for h in range(PEER_HEADS):
                hs = slice(h * PEER_NKEYS, (h + 1) * PEER_NKEYS)
                w = jnp.where(rk_ref[hs, ls] < lim_rows[h][:, ls], p2_ref[hs, ls] * p1_rows[h][:, ls], zero)
                gacc = w if gacc is None else gacc + w
            g_ref[rows, ls] = gacc
    act = jnp.dot(u_ref[...], ht_ref[...], preferred_element_type=F32)
    gelu2 = act * (1.0 + jnp.tanh(math.sqrt(2.0 / math.pi) * (act + 0.044715 * (act * act * act))))
    pt = gelu2.astype(BF16) * g_ref[...]
    acc_ref[...] += jnp.dot(vt_ref[...], pt, preferred_element_type=F32)

    @pl.when(j == pl.num_programs(1) - 1)
    def _():
        o_ref[...] = acc_ref[...].T


def _peer_dense(m2t, u_b, vt_b, rk, lim, p1, p2, tm, te):
    d, t = m2t.shape
    ne = u_b.shape[0]
    hk = pl.BlockSpec((PEER_HEADS, PEER_NKEYS, tm), lambda i, j: (0, 0, i))
    flat = pl.BlockSpec((PEER_HEADS * PEER_NKEYS, tm), lambda i, j: (0, i))
    return pl.pallas_call(
        functools.partial(_peer_dense_kernel, ne1=te // PEER_NKEYS),
        grid=(t // tm, ne // te),
        in_specs=[pl.BlockSpec((d, tm), lambda i, j: (0, i)),
                  pl.BlockSpec((te, d), lambda i, j: (j, 0)),
                  pl.BlockSpec((d, te), lambda i, j: (0, j)),
                  flat, hk, hk, flat],
        out_specs=pl.BlockSpec((tm, d), lambda i, j: (i, 0)),
        out_shape=jax.ShapeDtypeStruct((t, d), F32),
        scratch_shapes=[pltpu.VMEM((d, tm), F32), pltpu.VMEM((te, tm), BF16),
                        pltpu.VMEM((PEER_HEADS * PEER_NKEYS, tm), BF16),
                        pltpu.VMEM((PEER_HEADS * PEER_NKEYS, tm), BF16)],
        compiler_params=_cparams(("arbitrary", "arbitrary")),
    )(m2t, u_b, vt_b, rk.reshape(PEER_HEADS * PEER_NKEYS, t), lim, p1, p2.reshape(PEER_HEADS * PEER_NKEYS, t))


SL_X = 0
SL_B = SL_X + SSD_INNER // LANES
SL_C = SL_B + SSD_GROUPS * SSD_STATE // LANES
SL_Z = SL_C + SSD_GROUPS * SSD_STATE // LANES
SL_GH = SL_Z + SSD_INNER // LANES
SL_GS = SL_GH + D_MODEL // LANES


def kernel(x, c, ctx, c_ctx, ada_w, ada_b, w_in, hy_conv_w, hy_conv_b, hy_ffn_w1, hy_ffn_b1, hy_freq1,
           hy_ffn_w2, hy_ffn_b2, hy_freq2, hy_ffn_w3, hy_skip, ssd_conv_w, ssd_conv_b, ssd_a_log,
           ssd_dt_bias, ssd_d, ssd_norm_w, w_branch_hy, w_branch_ssd, w_out, ln1_g, ln1_b, peer_wq,
           peer_subkeys, peer_u, peer_v, ln2_g, ln2_b):
    nb, seq, d = x.shape
    lc = ctx.shape[1]
    t = nb * seq
    p = dict(hy_ffn_w1=hy_ffn_w1[0], hy_ffn_b1=hy_ffn_b1[0], hy_freq1=hy_freq1[0], hy_ffn_w2=hy_ffn_w2[0],
             hy_ffn_b2=hy_ffn_b2[0], hy_freq2=hy_freq2[0], hy_ffn_w3=hy_ffn_w3[0])

    cc = jnp.zeros((8, d), F32).at[:nb].set(c).at[nb].set(c_ctx)
    mod = _ada(cc, ada_w[0], ada_b[0])
    sh1, sc1, gt1, sh2, sc2, gt2 = [mod[:nb, i * d:(i + 1) * d].reshape(nb, 1, d) for i in range(6)]
    csh1 = mod[nb:nb + 1, 0:d].reshape(1, 1, d)
    csc1 = mod[nb:nb + 1, d:2 * d].reshape(1, 1, d)

    w = w_in[0]
    w_hy = w[:, :OFF_X].astype(BF16)
    w_rest = jnp.concatenate([w[:, OFF_X:OFF_DT], w[:, OFF_Z:]], axis=1).astype(BF16)
    w_dt = w[:, OFF_DT:OFF_Z].astype(BF16)

    m_ctx = _modulate(ctx, csc1, csh1, 256).reshape(nb * lc, d)
    pc = _matmul(m_ctx, w_rest, 512, 1024, slab=True, ncols=OFF_C - OFF_X)
    dtc = _matmul(m_ctx, w_dt, 512, LANES)
    nxb = (OFF_C - OFF_X) // LANES
    cw = ssd_conv_w[0].reshape(3, -1, LANES).transpose(1, 0, 2)
    cbias = ssd_conv_b[0].reshape(-1, 1, LANES)
    xbc_ctx = _conv3(pc, 0, nxb, cw[:nxb], cbias[:nxb], lc, True)
    gw = SSD_HPG * SSD_HEAD_DIM
    zero_state = jnp.zeros((nb, SSD_GROUPS, SSD_STATE, gw), F32)
    dt_bias = ssd_dt_bias[0]
    a_log = ssd_a_log[0]
    xs_sl = SSD_INNER // LANES
    _, _, s_f, s_b = _ssd_scan(xbc_ctx, 0, xs_sl, xs_sl, dtc, dt_bias, a_log, zero_state, zero_state, lc)

    m1 = _modulate(x, sc1, sh1, 512).reshape(t, d)
    proj_hy = _matmul(m1, w_hy, 1024, 2048, slab=True)
    proj = _matmul(m1, w_rest, 1024, 2048, out_dtype=BF16, slab=True)
    dtp = _matmul(m1, w_dt, 1024, LANES)

    hw = hy_conv_w[0].reshape(3, -1, LANES).transpose(1, 0, 2)
    hb = hy_conv_b[0].reshape(-1, 1, LANES)
    nh = HY_WIDTH // LANES
    ug = _hyconv(proj_hy, 3 * nh, hw, hb, seq)
    hspec = _hyena_filter_spectra(p, seq)
    zg = _longconv(ug, ug, 0, nh, hy_skip[0, 0], hspec, 0, seq, False)
    y_hy = _longconv(zg, ug, 0, 2 * nh, hy_skip[0, 1], hspec, 1, seq, True)

    xbc = _conv3(proj, SL_X, SL_Z - SL_X, cw, cbias, seq, True)
    y_f, y_b, _, _ = _ssd_scan(xbc, 0, xs_sl, xs_sl + SSD_GROUPS, dtp, dt_bias, a_log, s_f, s_b, seq)
    d_ch = jnp.repeat(ssd_d[0, 0] + ssd_d[0, 1], SSD_HEAD_DIM)
    y_ss = _ssd_norm(y_f, y_b, xbc, proj, SL_Z, d_ch, ssd_norm_w[0], 1024)

    merged = _branch_merge(y_hy, y_ss, w_branch_hy[0].astype(BF16), w_branch_ssd[0].astype(BF16),
                           proj, SL_GH, SL_GS, 512, 1024)
    h1, m2, m2t = _outproj_ln(merged.reshape(nb, seq, d), w_out[0].astype(BF16), x, gt1, ln1_g[0], ln1_b[0],
                              sc2, sh2, 512)

    m2f = m2.reshape(t, d)
    scores = _peer_scores(m2f, peer_wq[0].astype(BF16), peer_subkeys[0].astype(BF16), 512)
    rk, lim, p1, p2 = _peer_stats(scores, 256)
    ffn = _peer_dense(m2t, peer_u[0].astype(BF16), peer_v[0].T.astype(BF16), rk, lim, p1, p2, 512, 1024)
    return _resid_ln(h1, ffn.reshape(nb, seq, d), gt2, ln2_g[0], ln2_b[0], 512)
```

```python
import functools
import math

import numpy as np
import jax
import jax.numpy as jnp
from jax import lax
from jax.experimental import pallas as pl
from jax.experimental.pallas import tpu as pltpu

F32 = jnp.float32
BF16 = jnp.bfloat16

D_MODEL = 2048
DEPTH = 1
CTX_LEN = 256
HY_WIDTH = D_MODEL
HY_BANDS = 16
HY_EMB = 2 * HY_BANDS + 1
HY_FFN = 64
HY_MIN_DECAY = math.log(1e-2) / 1.5
HY_MAX_DECAY = math.log(1e-2) / 0.3
SSD_INNER = 2 * D_MODEL
SSD_HEAD_DIM = 64
SSD_HEADS = SSD_INNER // SSD_HEAD_DIM
SSD_STATE = 128
SSD_GROUPS = 8
SSD_HPG = SSD_HEADS // SSD_GROUPS
SSD_CHUNK = 128
OFF_X = 3 * HY_WIDTH
OFF_B = OFF_X + SSD_INNER
OFF_C = OFF_B + SSD_GROUPS * SSD_STATE
OFF_DT = OFF_C + SSD_GROUPS * SSD_STATE
OFF_Z = OFF_DT + 2 * SSD_HEADS
OFF_GATE = OFF_Z + SSD_INNER
N_COLS = OFF_GATE + 2 * D_MODEL
PEER_HEADS = 8
PEER_NKEYS = 128
PEER_EXPERTS = PEER_NKEYS * PEER_NKEYS
PEER_TOPK = 16
PEER_DKEY = 256
DEEPNORM_ALPHA = (2.0 * DEPTH) ** 0.25

LANES = 128
VMEM_LIMIT = 56 * 1024 * 1024

FFT_N1 = 128
SA_PITCH_PAD = 8
SB_PITCH = FFT_N1 + 8


def _cparams(sem, vmem=VMEM_LIMIT):
    return pltpu.CompilerParams(dimension_semantics=sem, vmem_limit_bytes=vmem)


def _silu(x):
    return x * (1.0 / (1.0 + jnp.exp(-x)))


def _sigmoid(x):
    return 1.0 / (1.0 + jnp.exp(-x))


def _softplus(x):
    return jnp.maximum(x, 0.0) + jnp.log(1.0 + jnp.exp(-jnp.abs(x)))


def _gelu_tanh(x):
    return 0.5 * x * (1.0 + jnp.tanh(math.sqrt(2.0 / math.pi) * (x + 0.044715 * (x * x * x))))


def _normalize_rows(x, eps):
    mu = jnp.mean(x, axis=-1, keepdims=True)
    xc = x - mu
    var = jnp.mean(xc * xc, axis=-1, keepdims=True)
    return xc * lax.rsqrt(var + eps)


def _ada_kernel(c_ref, w_ref, b_ref, o_ref):
    a = _silu(c_ref[...]).astype(BF16)
    o_ref[...] = jnp.dot(a, w_ref[...].astype(BF16), preferred_element_type=F32) + b_ref[...]


def _ada(cc, ada_w, ada_b):
    rows, d = cc.shape
    n = ada_w.shape[1]
    tn = 1024
    return pl.pallas_call(
        _ada_kernel,
        grid=(n // tn,),
        in_specs=[pl.BlockSpec((rows, d), lambda j: (0, 0)),
                  pl.BlockSpec((d, tn), lambda j: (0, j)),
                  pl.BlockSpec((1, tn), lambda j: (0, j))],
        out_specs=pl.BlockSpec((rows, tn), lambda j: (0, j)),
        out_shape=jax.ShapeDtypeStruct((rows, n), F32),
        compiler_params=_cparams(("arbitrary",)),
    )(cc, ada_w, ada_b.reshape(1, n))


def _mod_kernel(x_ref, sc_ref, sh_ref, o_ref):
    xn = _normalize_rows(x_ref[0], 1e-6)
    o_ref[0] = (xn * (1.0 + sc_ref[0]) + sh_ref[0]).astype(o_ref.dtype)


def _modulate(x, scale, shift, tr):
    b, l, d = x.shape
    per_batch = scale.shape[0] == b
    smap = (lambda i, j: (i, 0, 0)) if per_batch else (lambda i, j: (0, 0, 0))
    return pl.pallas_call(
        _mod_kernel,
        grid=(b, l // tr),
        in_specs=[pl.BlockSpec((1, tr, d), lambda i, j: (i, j, 0)),
                  pl.BlockSpec((1, 1, d), smap),
                  pl.BlockSpec((1, 1, d), smap)],
        out_specs=pl.BlockSpec((1, tr, d), lambda i, j: (i, j, 0)),
        out_shape=jax.ShapeDtypeStruct((b, l, d), BF16),
        compiler_params=_cparams(("arbitrary", "arbitrary")),
    )(x, scale, shift)


def _mm_kernel(a_ref, b_ref, o_ref, *, slab):
    acc = jnp.dot(a_ref[...], b_ref[...], preferred_element_type=F32)
    if slab:
        for s in range(o_ref.shape[0]):
            o_ref[s] = acc[:, s * LANES:(s + 1) * LANES].astype(o_ref.dtype)
    else:
        o_ref[...] = acc.astype(o_ref.dtype)


def _matmul(a, b, tm, tn, out_dtype=F32, slab=False, ncols=None):
    m, k = a.shape
    n = b.shape[1] if ncols is None else ncols
    if slab:
        out_shape = jax.ShapeDtypeStruct((n // LANES, m, LANES), out_dtype)
        out_spec = pl.BlockSpec((tn // LANES, tm, LANES), lambda i, j: (j, i, 0))
    else:
        out_shape = jax.ShapeDtypeStruct((m, n), out_dtype)
        out_spec = pl.BlockSpec((tm, tn), lambda i, j: (i, j))
    return pl.pallas_call(
        functools.partial(_mm_kernel, slab=slab),
        grid=(m // tm, n // tn),
        in_specs=[pl.BlockSpec((tm, k), lambda i, j: (i, 0)),
                  pl.BlockSpec((k, tn), lambda i, j: (0, j))],
        out_specs=out_spec,
        out_shape=out_shape,
        compiler_params=_cparams(("arbitrary", "arbitrary")),
    )(a, b)


def _conv3_kernel(x_ref, w_ref, b_ref, o_ref, *, silu):
    for i in range(x_ref.shape[0]):
        x = x_ref[i].astype(F32)
        l = x.shape[0]
        row = lax.broadcasted_iota(jnp.int32, x.shape, 0)
        prev = jnp.where(row == 0, 0.0, pltpu.roll(x, 1, 0))
        nxt = jnp.where(row == l - 1, 0.0, pltpu.roll(x, l - 1, 0))
        w = w_ref[i]
        y = prev * w[0:1] + x * w[1:2] + nxt * w[2:3] + b_ref[i]
        if silu:
            y = _silu(y)
        o_ref[i] = y.astype(o_ref.dtype)


def _conv3(p_slab, slab0, nslab, w, bias, seq, silu):
    rows = p_slab.shape[1]
    nb = rows // seq
    sp = 4
    return pl.pallas_call(
        functools.partial(_conv3_kernel, silu=silu),
        grid=(nslab // sp, nb),
        in_specs=[pl.BlockSpec((sp, seq, LANES), lambda s, b: (slab0 // sp + s, b, 0)),
                  pl.BlockSpec((sp, 3, LANES), lambda s, b: (s, 0, 0)),
                  pl.BlockSpec((sp, 1, LANES), lambda s, b: (s, 0, 0))],
        out_specs=pl.BlockSpec((sp, seq, LANES), lambda s, b: (s, b, 0)),
        out_shape=jax.ShapeDtypeStruct((nslab, rows, LANES), BF16),
        compiler_params=_cparams(("arbitrary", "arbitrary")),
    )(p_slab, w, bias)


def _hyconv_kernel(x_ref, w_ref, b_ref, o_ref, *, seq, nb):
    nj = seq // FFT_N1

    def gather(n1):
        return jnp.concatenate(
            [x_ref[0, pl.ds(n1 + b * seq, nj, stride=FFT_N1), :] for b in range(nb)], axis=0)

    w = w_ref[0]
    w0, w1, w2, bias = w[0:1], w[1:2], w[2:3], b_ref[0]
    jrow = lax.broadcasted_iota(jnp.int32, (nb * nj, LANES), 0) % nj

    def shift_down(g):
        return jnp.where(jrow == 0, 0.0, pltpu.roll(g, 1, 0))

    def shift_up(g):
        return jnp.where(jrow == nj - 1, 0.0, pltpu.roll(g, nb * nj - 1, 0))

    g_first = gather(0)
    g_second = gather(1)
    g_last = gather(FFT_N1 - 1)
    dt = o_ref.dtype
    o_ref[0, 0] = (shift_down(g_last) * w0 + g_first * w1 + g_second * w2 + bias).astype(dt)

    def body(n1, carry):
        g_prev, g_cur = carry
        g_next = gather(n1 + 1)
        o_ref[0, n1] = (g_prev * w0 + g_cur * w1 + g_next * w2 + bias).astype(dt)
        return g_cur, g_next

    g_prev, g_cur = lax.fori_loop(1, FFT_N1 - 1, body, (g_first, g_second), unroll=6)
    o_ref[0, FFT_N1 - 1] = (g_prev * w0 + g_cur * w1 + shift_up(g_first) * w2 + bias).astype(dt)


def _hyconv(p_slab, nslab, w, bias, seq):
    rows = p_slab.shape[1]
    nb = rows // seq
    nj = seq // FFT_N1
    return pl.pallas_call(
        functools.partial(_hyconv_kernel, seq=seq, nb=nb),
        grid=(nslab,),
        in_specs=[pl.BlockSpec((1, rows, LANES), lambda s: (s, 0, 0)),
                  pl.BlockSpec((1, 3, LANES), lambda s: (s, 0, 0)),
                  pl.BlockSpec((1, 1, LANES), lambda s: (s, 0, 0))],
        out_specs=pl.BlockSpec((1, FFT_N1, nb * nj, LANES), lambda s: (s, 0, 0, 0)),
        out_shape=jax.ShapeDtypeStruct((nslab, FFT_N1, nb * nj, LANES), BF16),
        compiler_params=_cparams(("arbitrary",)),
    )(p_slab, w, bias)


@functools.lru_cache(maxsize=None)
def _dft_tables(seq):
    nj = seq // FFT_N1
    n2 = nj + nj // 2
    n = FFT_N1 * n2
    out0 = nj // 2
    n1 = np.arange(FFT_N1, dtype=np.float64)
    k2 = np.arange(n2, dtype=np.float64)
    jn = np.arange(nj, dtype=np.float64)
    ang = -2.0 * np.pi * (k2[None, :, None] * jn[None, None, :] / n2 + n1[:, None, None] * k2[None, :, None] / n)
    cr, ci = np.cos(ang), np.sin(ang)
    fa = np.concatenate([np.concatenate([cr, -ci], axis=2), np.concatenate([ci, cr], axis=2)], axis=1)
    k1 = np.arange(FFT_N1, dtype=np.float64)
    angb = -2.0 * np.pi * np.outer(k1, n1) / FFT_N1
    dr, di = np.cos(angb), np.sin(angb)
    fb = np.block([[dr, -di], [di, dr]])
    fbi = np.block([[dr, di], [-di, dr]])
    angi = 2.0 * np.pi * (n1[:, None, None] * k2[None, None, :] / n + (out0 + jn)[None, :, None] * k2[None, None, :] / n2)
    er, ei = np.cos(angi) / n, np.sin(angi) / n
    fai = np.concatenate([np.concatenate([er, -ei], axis=2), np.concatenate([ei, er], axis=2)], axis=1)
    return (fa.astype(np.float32), fb.astype(np.float32), fbi.astype(np.float32), fai.astype(np.float32), n2)


@functools.lru_cache(maxsize=None)
def _filter_consts(seq):
    t = np.linspace(0.0, 1.0, seq, dtype=np.float32)[:, None].astype(np.float64)
    w = 2.0 * np.pi * np.arange(seq, dtype=np.float32)[:, None].astype(np.float64) / seq
    bands = np.linspace(1e-4, HY_BANDS - 1, HY_BANDS, dtype=np.float32)[None, :].astype(np.float64)
    feats = np.concatenate([t, np.cos(bands * w), -np.sin(bands * w)], axis=-1)
    feats_p = np.zeros((seq, LANES), np.float32)
    feats_p[:, :HY_EMB] = feats
    dist = (np.abs(np.arange(seq) - seq // 2).astype(np.float32) / np.float32(seq / 2.0)).reshape(seq, 1)
    deltas = np.abs(np.linspace(HY_MIN_DECAY, HY_MAX_DECAY, HY_WIDTH, dtype=np.float32))
    return feats_p, dist.astype(np.float32), deltas.astype(np.float32)


def _dot_f32(a, b):
    def split(x):
        hi = x.astype(BF16)
        r1 = x - hi.astype(F32)
        mid = r1.astype(BF16)
        lo = (r1 - mid.astype(F32)).astype(BF16)
        return hi, mid, lo
    a0, a1, a2 = split(a)
    b0, b1, b2 = split(b)
    d = functools.partial(jnp.dot, preferred_element_type=F32)
    return (d(a0, b0) + (d(a0, b1) + d(a1, b0)) + (d(a0, b2) + d(a1, b1) + d(a2, b0)))


def _filter_kernel(feats_ref, w1_ref, b1_ref, f1_ref, w2_ref, b2_ref, f2_ref, w3_ref, dist_ref, delta_ref,
                   fa_ref, fb_ref, o_ref, hid_ref, h_ref, sar_ref, sai_ref, *, seq, n2, pitch):
    nj = seq // FFT_N1

    @pl.when((pl.program_id(0) == 0) & (pl.program_id(1) == 0))
    def _():
        h1 = jnp.sin(f1_ref[...] * (_dot_f32(feats_ref[...], w1_ref[...]) + b1_ref[...]))
        hid_ref[...] = jnp.sin(f2_ref[...] * (_dot_f32(h1, w2_ref[...]) + b2_ref[...])).astype(BF16)

    ns = h_ref.shape[0]
    lanes = lambda f: jnp.concatenate([f(s) for s in range(ns)], axis=1)
    w3 = lanes(lambda s: w3_ref[0, s]).astype(BF16)
    h = jnp.dot(hid_ref[...], w3, preferred_element_type=F32)
    h = h * jnp.exp(-dist_ref[...] * lanes(lambda s: delta_ref[s]))
    h = h / (jnp.sum(jnp.abs(h), axis=0, keepdims=True) + 1e-6)
    for s in range(ns):
        h_ref[s] = h[:, s * LANES:(s + 1) * LANES]

    def stage_a(n1, c):
        g = lanes(lambda s: h_ref[s, pl.ds(n1, nj, stride=FFT_N1), :]).astype(BF16)
        y = jnp.dot(fa_ref[n1], g, preferred_element_type=F32)
        base = pl.multiple_of(n1 * pitch, 8)
        for s in range(ns):
            sar_ref[s, pl.ds(base, n2), :] = y[:n2, s * LANES:(s + 1) * LANES]
            sai_ref[s, pl.ds(base, n2), :] = y[n2:, s * LANES:(s + 1) * LANES]
        return c

    lax.fori_loop(0, FFT_N1, stage_a, 0, unroll=8)

    def stage_b(k2, c):
        yr = lanes(lambda s: sar_ref[s, pl.ds(k2, FFT_N1, stride=pitch), :])
        yi = lanes(lambda s: sai_ref[s, pl.ds(k2, FFT_N1, stride=pitch), :])
        op = jnp.concatenate([yr, yi], axis=0).astype(BF16)
        res = jnp.dot(fb_ref[...], op, preferred_element_type=F32).astype(o_ref.dtype)
        for s in range(ns):
            o_ref[0, s, k2] = res[:, s * LANES:(s + 1) * LANES]
        return c

    lax.fori_loop(0, n2, stage_b, 0, unroll=4)


def _hyena_filter_spectra(p, seq):
    fa, fb, _, _, n2 = _dft_tables(seq)
    nj = seq // FFT_N1
    pitch = n2 + SA_PITCH_PAD
    feats, dist, deltas = _filter_consts(seq)
    nsl = HY_WIDTH // LANES

    def padk(w, rows):
        return jnp.zeros((rows, w.shape[1]), F32).at[:w.shape[0]].set(w)

    def padn(v):
        return jnp.zeros((1, LANES), F32).at[0, :v.shape[0]].set(v)

    w1 = jnp.zeros((LANES, LANES), F32).at[:HY_EMB, :HY_FFN].set(p['hy_ffn_w1'])
    w2 = jnp.zeros((LANES, LANES), F32).at[:HY_FFN, :HY_FFN].set(p['hy_ffn_w2'])
    w3 = padk(p['hy_ffn_w3'], LANES).reshape(LANES, 2, nsl, LANES).transpose(1, 2, 0, 3)
    fa_real = jnp.asarray(fa[:, :, :nj]).astype(BF16)
    const = lambda shape: pl.BlockSpec(shape, lambda o, s: (0,) * len(shape))
    ns = 2
    return pl.pallas_call(
        functools.partial(_filter_kernel, seq=seq, n2=n2, pitch=pitch),
        grid=(2, nsl // ns),
        in_specs=[const((seq, LANES)), const((LANES, LANES)), const((1, LANES)), const((1, LANES)),
                  const((LANES, LANES)), const((1, LANES)), const((1, LANES)),
                  pl.BlockSpec((1, ns, LANES, LANES), lambda o, s: (o, s, 0, 0)),
                  const((seq, 1)),
                  pl.BlockSpec((ns, 1, LANES), lambda o, s: (s, 0, 0)),
                  const((FFT_N1, 2 * n2, nj)), const((2 * FFT_N1, 2 * FFT_N1))],
        out_specs=pl.BlockSpec((1, ns, n2, 2 * FFT_N1, LANES), lambda o, s: (o, s, 0, 0, 0)),
        out_shape=jax.ShapeDtypeStruct((2, nsl, n2, 2 * FFT_N1, LANES), BF16),
        scratch_shapes=[pltpu.VMEM((seq, LANES), BF16), pltpu.VMEM((ns, seq, LANES), F32),
                        pltpu.VMEM((ns, FFT_N1 * pitch, LANES), F32),
                        pltpu.VMEM((ns, FFT_N1 * pitch, LANES), F32)],
        compiler_params=_cparams(("arbitrary", "arbitrary")),
    )(jnp.asarray(feats), w1, padn(p['hy_ffn_b1']), padn(p['hy_freq1']), w2, padn(p['hy_ffn_b2']),
      padn(p['hy_freq2']), w3, jnp.asarray(dist), jnp.asarray(deltas).reshape(nsl, 1, LANES),
      fa_real, jnp.asarray(fb).astype(BF16))


def _longconv_kernel(x_ref, g_ref, skip_ref, h_ref, fa_ref, fb_ref, fbi_ref, fai_ref, o_ref,
                     sar_ref, sai_ref, sbr_ref, sbi_ref, *, seq, n2, pitch, natural_out):
    nj = seq // FFT_N1

    def stage_a(n1, c):
        y = jnp.dot(fa_ref[n1], x_ref[0, n1].astype(BF16), preferred_element_type=F32)
        base = pl.multiple_of(n1 * pitch, 8)
        sar_ref[pl.ds(base, n2), :] = y[:n2]
        sai_ref[pl.ds(base, n2), :] = y[n2:]
        return c

    lax.fori_loop(0, FFT_N1, stage_a, 0, unroll=8)

    def stage_b(kp, c):
        ks = (2 * kp, 2 * kp + 1)
        yr = jnp.concatenate([sar_ref[pl.ds(k, FFT_N1, stride=pitch), :] for k in ks], axis=1)
        yi = jnp.concatenate([sai_ref[pl.ds(k, FFT_N1, stride=pitch), :] for k in ks], axis=1)
        xf = jnp.dot(fb_ref[...], jnp.concatenate([yr, yi], axis=0).astype(BF16), preferred_element_type=F32)
        xr, xi = xf[:FFT_N1], xf[FFT_N1:]
        hf = jnp.concatenate([h_ref[0, k] for k in ks], axis=1).astype(F32)
        hr, hi = hf[:FFT_N1], hf[FFT_N1:]
        z = jnp.concatenate([xr * hr - xi * hi, xr * hi + xi * hr], axis=0).astype(BF16)
        w = jnp.dot(fbi_ref[...], z, preferred_element_type=F32)
        for q, k in enumerate(ks):
            base = pl.multiple_of(k * SB_PITCH, 8)
            sbr_ref[pl.ds(base, FFT_N1), :] = w[:FFT_N1, q * LANES:(q + 1) * LANES]
            sbi_ref[pl.ds(base, FFT_N1), :] = w[FFT_N1:, q * LANES:(q + 1) * LANES]
        return c

    lax.fori_loop(0, n2 // 2, stage_b, 0, unroll=2)

    skip = skip_ref[0]

    def stage_c(n1, c):
        wr = sbr_ref[pl.ds(n1, n2, stride=SB_PITCH), :]
        wi = sbi_ref[pl.ds(n1, n2, stride=SB_PITCH), :]
        y = jnp.dot(fai_ref[n1], jnp.concatenate([wr, wi], axis=0).astype(BF16), preferred_element_type=F32)
        res = g_ref[0, n1].astype(F32) * (y + x_ref[0, n1].astype(F32) * skip)
        if natural_out:
            for b in range(res.shape[0] // nj):
                o_ref[0, pl.ds(n1 + b * seq, nj, stride=FFT_N1), :] = res[b * nj:(b + 1) * nj]
        else:
            o_ref[0, n1] = res.astype(o_ref.dtype)
        return c

    lax.fori_loop(0, FFT_N1, stage_c, 0, unroll=8)


def _longconv(xg, gg, slab_x, slab_g, skip, hspec, order, seq, natural_out):
    fa, fb, fbi, fai, n2 = _dft_tables(seq)
    pitch = n2 + SA_PITCH_PAD
    nsl = HY_WIDTH // LANES
    rows = xg.shape[2]
    nb = rows // (seq // FFT_N1)
    const = lambda shape: pl.BlockSpec(shape, lambda s: (0,) * len(shape))
    if natural_out:
        out_shape = jax.ShapeDtypeStruct((nsl, nb * seq, LANES), F32)
        out_spec = pl.BlockSpec((1, nb * seq, LANES), lambda s: (s, 0, 0))
    else:
        out_shape = jax.ShapeDtypeStruct((nsl, FFT_N1, rows, LANES), BF16)
        out_spec = pl.BlockSpec((1, FFT_N1, rows, LANES), lambda s: (s, 0, 0, 0))
    return pl.pallas_call(
        functools.partial(_longconv_kernel, seq=seq, n2=n2, pitch=pitch, natural_out=natural_out),
        grid=(nsl,),
        in_specs=[pl.BlockSpec((1, FFT_N1, rows, LANES), lambda s: (slab_x + s, 0, 0, 0)),
                  pl.BlockSpec((1, FFT_N1, rows, LANES), lambda s: (slab_g + s, 0, 0, 0)),
                  pl.BlockSpec((1, 1, LANES), lambda s: (s, 0, 0)),
                  pl.BlockSpec((None, 1, n2, 2 * FFT_N1, LANES), lambda s: (order, s, 0, 0, 0)),
                  const(fa.shape), const(fb.shape), const(fbi.shape), const(fai.shape)],
        out_specs=out_spec,
        out_shape=out_shape,
        scratch_shapes=[pltpu.VMEM((FFT_N1 * pitch, LANES), F32), pltpu.VMEM((FFT_N1 * pitch, LANES), F32),
                        pltpu.VMEM((n2 * SB_PITCH, LANES), F32), pltpu.VMEM((n2 * SB_PITCH, LANES), F32)],
        compiler_params=_cparams(("arbitrary",)),
    )(xg, gg, skip.reshape(nsl, 1, LANES), hspec,
      jnp.asarray(fa).astype(BF16), jnp.asarray(fb).astype(BF16),
      jnp.asarray(fbi).astype(BF16), jnp.asarray(fai).astype(BF16))


def _dot_exact_lhs(lhs_bf16, x):
    hi = x.astype(BF16)
    r1 = x - hi.astype(F32)
    mid = r1.astype(BF16)
    lo = (r1 - mid.astype(F32)).astype(BF16)
    d = functools.partial(jnp.dot, preferred_element_type=F32)
    return d(lhs_bf16, hi) + (d(lhs_bf16, mid) + d(lhs_bf16, lo))


def _ssd_chunk_groups(x_ref, b_ref, c_ref, dt_ref, bias_ref, alog_ref, e_ref, y_ref, st_ref, rev):
    gw = SSD_HPG * SSD_HEAD_DIM
    dot = functools.partial(jnp.dot, preferred_element_type=F32)
    off = SSD_HEADS if rev else 0
    dt_all = _softplus(dt_ref[...] + bias_ref[...])
    a_all = dt_all * (-jnp.exp(alog_ref[...]))
    r_i = lax.broadcasted_iota(jnp.int32, (SSD_CHUNK, SSD_CHUNK), 0)
    c_i = lax.broadcasted_iota(jnp.int32, (SSD_CHUNK, SSD_CHUNK), 1)
    valid = (c_i >= r_i) if rev else (c_i <= r_i)
    tri = jnp.where(valid, 1.0, 0.0).astype(BF16)
    acum = _dot_exact_lhs(tri, a_all)
    src_t = (acum - jnp.log(dt_all)).T
    tot = acum[0:1, :] if rev else acum[SSD_CHUNK - 1:SSD_CHUNK, :]
    dtw_b = (dt_all * jnp.exp(tot - acum)).astype(BF16)
    cd8 = jnp.broadcast_to(jnp.exp(tot), (8, LANES))
    cd_hi = cd8.astype(BF16)
    cd_r = cd8 - cd_hi.astype(F32)
    cd_mid = cd_r.astype(BF16)
    cd_lo = (cd_r - cd_mid.astype(F32)).astype(BF16)
    lo_half = c_i < SSD_HEAD_DIM
    zero_b = jnp.zeros((), BF16)

    def group(g):
        bt = b_ref[g].astype(F32).T.astype(BF16)
        cg = c_ref[g].astype(BF16)
        cb = dot(cg, bt)
        for sl in range(gw // LANES):
            col = slice(sl * LANES, (sl + 1) * LANES)
            e_s = e_ref[:, g * gw + sl * LANES:g * gw + (sl + 1) * LANES]
            xb = x_ref[(gw // LANES) * g + sl]
            ms, ea = [], []
            for hh in range(2):
                h = off + SSD_HPG * g + 2 * sl + hh
                a_col = jnp.broadcast_to(acum[:, h:h + 1], (SSD_CHUNK, SSD_CHUNK))
                seg = jnp.where(valid, a_col - src_t[h:h + 1, :], -1e30)
                ms.append((cb * jnp.exp(seg)).astype(BF16))
                ea.append(jnp.exp(a_col))
            rhs = jnp.concatenate([jnp.where(lo_half, xb, zero_b), jnp.where(lo_half, zero_b, xb)], axis=0)
            yd = dot(jnp.concatenate(ms, axis=1), rhs)
            sgs = st_ref[g, :, col]
            yoff = dot(cg, sgs.astype(BF16)) * jnp.where(lo_half, ea[0], ea[1])
            y_ref[g, :, col] = (yd + yoff).astype(y_ref.dtype)
            wm = (xb.astype(F32) * dot(dtw_b, e_s)).astype(BF16)
            cd = (dot(cd_hi, e_s) + (dot(cd_mid, e_s) + dot(cd_lo, e_s)))[0:1]
            st_ref[g, :, col] = sgs * cd + dot(bt, wm)

    return group


def _ssd_kernel(xf_ref, bf_ref, cf_ref, dtf_ref, xr_ref, br_ref, cr_ref, dtr_ref, bias_ref, alog_ref, ef_ref, er_ref,
                initf_ref, initr_ref, yf_ref, yr_ref, finf_ref, finr_ref, stf_ref, str_ref, *, nc):
    ci = pl.program_id(1)

    @pl.when(ci == 0)
    def _():
        stf_ref[...] = initf_ref[0]
        str_ref[...] = initr_ref[0]

    fwd = _ssd_chunk_groups(xf_ref, bf_ref, cf_ref, dtf_ref, bias_ref, alog_ref, ef_ref, yf_ref, stf_ref, False)
    bwd = _ssd_chunk_groups(xr_ref, br_ref, cr_ref, dtr_ref, bias_ref, alog_ref, er_ref, yr_ref, str_ref, True)
    for g in range(SSD_GROUPS):
        fwd(g)
        bwd(g)

    @pl.when(ci == nc - 1)
    def _():
        finf_ref[0] = stf_ref[...]
        finr_ref[0] = str_ref[...]


def _ssd_scan(xbc, slab_x, slab_b, slab_c, dt, dt_bias, a_log, init_f, init_r, seq):
    rows = xbc.shape[1]
    nb = rows // seq
    nc = seq // SSD_CHUNK
    gw = SSD_HPG * SSD_HEAD_DIM
    xs = SSD_INNER // LANES
    pos_f = lambda b, c: b * nc + c
    pos_r = lambda b, c: b * nc + (nc - 1 - c)
    expand = np.zeros((2, 2 * SSD_HEADS, SSD_INNER), np.float32)
    for r in range(2):
        for h in range(SSD_HEADS):
            expand[r, r * SSD_HEADS + h, h * SSD_HEAD_DIM:(h + 1) * SSD_HEAD_DIM] = 1.0

    def data_specs(pos):
        return [pl.BlockSpec((xs, SSD_CHUNK, LANES), lambda b, c: (slab_x // xs, pos(b, c), 0)),
                pl.BlockSpec((SSD_GROUPS, SSD_CHUNK, LANES), lambda b, c: (slab_b // SSD_GROUPS, pos(b, c), 0)),
                pl.BlockSpec((SSD_GROUPS, SSD_CHUNK, LANES), lambda b, c: (slab_c // SSD_GROUPS, pos(b, c), 0)),
                pl.BlockSpec((SSD_CHUNK, LANES), lambda b, c: (pos(b, c), 0))]

    vec = pl.BlockSpec((1, LANES), lambda b, c: (0, 0))
    emat = pl.BlockSpec((2 * SSD_HEADS, SSD_INNER), lambda b, c: (0, 0))
    state = pl.BlockSpec((1, SSD_GROUPS, SSD_STATE, gw), lambda b, c: (b, 0, 0, 0))
    y_sds = jax.ShapeDtypeStruct((SSD_GROUPS, rows, gw), BF16)
    s_sds = jax.ShapeDtypeStruct((nb, SSD_GROUPS, SSD_STATE, gw), F32)
    e_b = jnp.asarray(expand).astype(BF16)
    return pl.pallas_call(
        functools.partial(_ssd_kernel, nc=nc),
        grid=(nb, nc),
        in_specs=data_specs(pos_f) + data_specs(pos_r) + [vec, vec, emat, emat, state, state],
        out_specs=[pl.BlockSpec((SSD_GROUPS, SSD_CHUNK, gw), lambda b, c: (0, pos_f(b, c), 0)),
                   pl.BlockSpec((SSD_GROUPS, SSD_CHUNK, gw), lambda b, c: (0, pos_r(b, c), 0)),
                   state, state],
        out_shape=[y_sds, y_sds, s_sds, s_sds],
        scratch_shapes=[pltpu.VMEM((SSD_GROUPS, SSD_STATE, gw), F32), pltpu.VMEM((SSD_GROUPS, SSD_STATE, gw), F32)],
        compiler_params=_cparams(("arbitrary", "arbitrary")),
    )(xbc, xbc, xbc, dt, xbc, xbc, xbc, dt, dt_bias.reshape(1, LANES), a_log.reshape(1, LANES),
      e_b[0], e_b[1], init_f, init_r)


def _ssdnorm_kernel(yf_ref, yb_ref, x_ref, z_ref, d_ref, w_ref, o_ref):
    nsl = x_ref.shape[0]
    x = jnp.concatenate([x_ref[i] for i in range(nsl)], axis=1).astype(F32)
    z = jnp.concatenate([z_ref[i] for i in range(nsl)], axis=1).astype(F32)
    y = yf_ref[0].astype(F32) + yb_ref[0].astype(F32) + x * d_ref[0]
    gx = y * _silu(z)
    ms = jnp.mean(gx * gx, axis=-1, keepdims=True)
    o_ref[...] = (gx * lax.rsqrt(ms + 1e-5) * w_ref[0]).astype(o_ref.dtype)


def _ssd_norm(yf, yb, xbc, proj, slab_z, d_ch, w, tr):
    g, rows, gw = yf.shape
    nsl = gw // LANES
    return pl.pallas_call(
        _ssdnorm_kernel,
        grid=(g, rows // tr),
        in_specs=[pl.BlockSpec((1, tr, gw), lambda q, i: (q, i, 0)),
                  pl.BlockSpec((1, tr, gw), lambda q, i: (q, i, 0)),
                  pl.BlockSpec((nsl, tr, LANES), lambda q, i: (q, i, 0)),
                  pl.BlockSpec((nsl, tr, LANES), lambda q, i: (slab_z // nsl + q, i, 0)),
                  pl.BlockSpec((1, 1, gw), lambda q, i: (q, 0, 0)),
                  pl.BlockSpec((1, 1, gw), lambda q, i: (q, 0, 0))],
        out_specs=pl.BlockSpec((tr, gw), lambda q, i: (i, q)),
        out_shape=jax.ShapeDtypeStruct((rows, g * gw), BF16),
        compiler_params=_cparams(("arbitrary", "arbitrary")),
    )(yf, yb, xbc, proj, d_ch.reshape(g, 1, gw), w.reshape(g, 1, gw))


def _merge_kernel(yhy_ref, yss_ref, whh_ref, wss_ref, gh_ref, gs_ref, o_ref):
    a1 = jnp.concatenate([yhy_ref[i] for i in range(yhy_ref.shape[0])], axis=1).astype(BF16)
    p1 = jnp.dot(a1, whh_ref[...], preferred_element_type=F32)
    p2 = jnp.dot(yss_ref[...], wss_ref[...], preferred_element_type=F32)
    gh = jnp.concatenate([gh_ref[i] for i in range(gh_ref.shape[0])], axis=1).astype(F32)
    gs = jnp.concatenate([gs_ref[i] for i in range(gs_ref.shape[0])], axis=1).astype(F32)
    o_ref[...] = (_sigmoid(gh) * p1 + _sigmoid(gs) * p2).astype(o_ref.dtype)


def _branch_merge(yhy, yss, w_hy, w_ssd, proj, slab_gh, slab_gs, tm, tn):
    nsl_k = yhy.shape[0]
    rows = yhy.shape[1]
    n = w_hy.shape[1]
    ts = tn // LANES
    return pl.pallas_call(
        _merge_kernel,
        grid=(rows // tm, n // tn),
        in_specs=[pl.BlockSpec((nsl_k, tm, LANES), lambda i, j: (0, i, 0)),
                  pl.BlockSpec((tm, yss.shape[1]), lambda i, j: (i, 0)),
                  pl.BlockSpec((w_hy.shape[0], tn), lambda i, j: (0, j)),
                  pl.BlockSpec((w_ssd.shape[0], tn), lambda i, j: (0, j)),
                  pl.BlockSpec((ts, tm, LANES), lambda i, j: (slab_gh // ts + j, i, 0)),
                  pl.BlockSpec((ts, tm, LANES), lambda i, j: (slab_gs // ts + j, i, 0))],
        out_specs=pl.BlockSpec((tm, tn), lambda i, j: (i, j)),
        out_shape=jax.ShapeDtypeStruct((rows, n), BF16),
        compiler_params=_cparams(("arbitrary", "arbitrary")),
    )(yhy, yss, w_hy, w_ssd, proj, proj)


def _outproj_ln_kernel(a_ref, w_ref, x_ref, gt_ref, g_ref, b_ref, sc_ref, sh_ref, h_ref, m_ref, mt_ref):
    y = jnp.dot(a_ref[0], w_ref[...], preferred_element_type=F32)
    h = _normalize_rows(DEEPNORM_ALPHA * x_ref[0] + gt_ref[0] * y, 1e-6) * g_ref[...] + b_ref[...]
    h_ref[0] = h
    m = _normalize_rows(h, 1e-6) * (1.0 + sc_ref[0]) + sh_ref[0]
    m_ref[0] = m.astype(m_ref.dtype)
    mt_ref[...] = m.T.astype(mt_ref.dtype)


def _outproj_ln(merged, w_out, x, gate, ln_g, ln_b, scale2, shift2, tm):
    b, l, d = x.shape
    nt = l // tm
    vec = pl.BlockSpec((1, 1, d), lambda i, j: (i, 0, 0))
    cvec = pl.BlockSpec((1, d), lambda i, j: (0, 0))
    tile = pl.BlockSpec((1, tm, d), lambda i, j: (i, j, 0))
    return pl.pallas_call(
        _outproj_ln_kernel,
        grid=(b, nt),
        in_specs=[tile, pl.BlockSpec((d, d), lambda i, j: (0, 0)), tile, vec, cvec, cvec, vec, vec],
        out_specs=[tile, tile, pl.BlockSpec((d, tm), lambda i, j: (0, i * nt + j))],
        out_shape=[jax.ShapeDtypeStruct((b, l, d), F32), jax.ShapeDtypeStruct((b, l, d), BF16),
                   jax.ShapeDtypeStruct((d, b * l), BF16)],
        compiler_params=_cparams(("arbitrary", "arbitrary")),
    )(merged, w_out, x, gate, ln_g.reshape(1, d), ln_b.reshape(1, d), scale2, shift2)


def _resid_ln_kernel(x_ref, y_ref, gt_ref, g_ref, b_ref, o_ref):
    o_ref[0] = _normalize_rows(DEEPNORM_ALPHA * x_ref[0] + gt_ref[0] * y_ref[0], 1e-6) * g_ref[...] + b_ref[...]


def _resid_ln(x, y, gate, ln_g, ln_b, tm):
    b, l, d = x.shape
    tile = pl.BlockSpec((1, tm, d), lambda i, j: (i, j, 0))
    return pl.pallas_call(
        _resid_ln_kernel,
        grid=(b, l // tm),
        in_specs=[tile, tile, pl.BlockSpec((1, 1, d), lambda i, j: (i, 0, 0)),
                  pl.BlockSpec((1, d), lambda i, j: (0, 0)), pl.BlockSpec((1, d), lambda i, j: (0, 0))],
        out_specs=tile,
        out_shape=jax.ShapeDtypeStruct((b, l, d), F32),
        compiler_params=_cparams(("arbitrary", "arbitrary")),
    )(x, y, gate, ln_g.reshape(1, d), ln_b.reshape(1, d))


def _peer_scores_kernel(a_ref, wq_ref, k_ref, s_ref):
    q = jnp.dot(a_ref[...], wq_ref[...], preferred_element_type=F32)
    hd = PEER_DKEY // 2
    nt = (((1,), (1,)), ((), ()))
    for h in range(PEER_HEADS):
        qn = _normalize_rows(q[:, h * PEER_DKEY:(h + 1) * PEER_DKEY], 1e-6).astype(BF16)
        s_ref[h, 0] = lax.dot_general(k_ref[h, 0], qn[:, :hd], nt, preferred_element_type=F32)
        s_ref[h, 1] = lax.dot_general(k_ref[h, 1], qn[:, hd:], nt, preferred_element_type=F32)


def _peer_scores(m2, wq, subkeys, tm):
    t, d = m2.shape
    return pl.pallas_call(
        _peer_scores_kernel,
        grid=(t // tm,),
        in_specs=[pl.BlockSpec((tm, d), lambda i: (i, 0)),
                  pl.BlockSpec(wq.shape, lambda i: (0, 0)),
                  pl.BlockSpec(subkeys.shape, lambda i: (0, 0, 0, 0))],
        out_specs=pl.BlockSpec((PEER_HEADS, 2, PEER_NKEYS, tm), lambda i: (0, 0, 0, i)),
        out_shape=jax.ShapeDtypeStruct((PEER_HEADS, 2, PEER_NKEYS, t), F32),
        compiler_params=_cparams(("arbitrary",)),
    )(m2, wq, subkeys)


def _top_values(x, k):
    out = []
    for _ in range(k):
        m = jnp.max(x, axis=0, keepdims=True)
        out.append(m)
        x = jnp.where(x == m, -jnp.inf, x)
    return out


def _top_values_ranked(x, k):
    out = []
    big = 2.0 ** 100
    for a in range(k):
        m = jnp.max(x, axis=0, keepdims=True)
        out.append(m)
        x = jnp.where(x == m, -big * (1.0 + a / 64.0), x)
    rank = jnp.where(x <= -big, (x * (-1.0 / big) - 1.0) * 64.0, float(k))
    return out, rank


def _peer_stats_kernel(s_ref, rk_ref, lim_ref, p1_ref, p2_ref):
    k = PEER_TOPK
    for h in range(PEER_HEADS):
        s1 = s_ref[h, 0]
        s2 = s_ref[h, 1]
        v1, r1 = _top_values_ranked(s1, k)
        v2, rk = _top_values_ranked(s2, k)
        v2m = jnp.concatenate(v2, axis=0)
        grid = [v1[a] + v2m[:k // (a + 1)] for a in range(k)]
        n_cand = sum(g.shape[0] for g in grid)
        pad = jnp.full((-n_cand % 8, s1.shape[1]), -jnp.inf, F32)
        best = _top_values(jnp.concatenate(grid + [pad], axis=0), k)
        z = best[0] * 0.0
        for c in best:
            z = z + jnp.exp(c - best[0])
        lim = jnp.zeros_like(s1)
        for a in range(k):
            n_sel = jnp.sum(jnp.where(grid[a] >= best[k - 1], 1.0, 0.0), axis=0, keepdims=True)
            lim = jnp.where(r1 == float(a), n_sel, lim)
        lim_ref[h] = lim
        rk_ref[h] = rk.astype(rk_ref.dtype)
        p1_ref[h] = jnp.exp(s1 - v1[0])
        p2_ref[h] = (0.5 * jnp.exp(s2 - v2[0]) / z).astype(p2_ref.dtype)


def _peer_select_kernel(a_ref, wq_ref, k_ref, rk_ref, lim_ref, p1_ref, p2_ref, s_ref):
    _peer_scores_kernel(a_ref, wq_ref, k_ref, s_ref)
    _peer_stats_kernel(s_ref, rk_ref, lim_ref, p1_ref, p2_ref)


def _peer_select(m2, wq, subkeys, tm):
    t, d = m2.shape
    hk = pl.BlockSpec((PEER_HEADS, PEER_NKEYS, tm), lambda i: (0, 0, i))
    f32 = jax.ShapeDtypeStruct((PEER_HEADS, PEER_NKEYS, t), F32)
    b16 = jax.ShapeDtypeStruct((PEER_HEADS, PEER_NKEYS, t), BF16)
    return pl.pallas_call(
        _peer_select_kernel,
        grid=(t // tm,),
        in_specs=[pl.BlockSpec((tm, d), lambda i: (i, 0)),
                  pl.BlockSpec(wq.shape, lambda i: (0, 0)),
                  pl.BlockSpec(subkeys.shape, lambda i: (0, 0, 0, 0))],
        out_specs=[hk, hk, hk, hk],
        out_shape=[b16, f32, f32, b16],
        scratch_shapes=[pltpu.VMEM((PEER_HEADS, 2, PEER_NKEYS, tm), F32)],
        compiler_params=_cparams(("arbitrary",)),
    )(m2, wq, subkeys)


def _peer_stats(scores, tm):
    t = scores.shape[-1]
    hk = pl.BlockSpec((PEER_HEADS, PEER_NKEYS, tm), lambda i: (0, 0, i))
    f32 = jax.ShapeDtypeStruct((PEER_HEADS, PEER_NKEYS, t), F32)
    b16 = jax.ShapeDtypeStruct((PEER_HEADS, PEER_NKEYS, t), BF16)
    return pl.pallas_call(
        _peer_stats_kernel,
        grid=(t // tm,),
        in_specs=[pl.BlockSpec((PEER_HEADS, 2, PEER_NKEYS, tm), lambda i: (0, 0, 0, i))],
        out_specs=[hk, hk, hk, hk],
        out_shape=[b16, f32, f32, b16],
        compiler_params=_cparams(("arbitrary",)),
    )(scores)


def _peer_dense_kernel(ht_ref, u_ref, vt_ref, rk_in, lim_ref, p1_ref, p2_in, o_ref, acc_ref, g_ref, rk_ref, p2_ref,
                       *, ne1):
    j = pl.program_id(1)

    @pl.when(j == 0)
    def _():
        acc_ref[...] = jnp.zeros_like(acc_ref)
        rk_ref[...] = rk_in[...]
        p2_ref[...] = p2_in[...]

    zero = jnp.zeros((), BF16)
    for r in range(ne1):
        e1 = j * ne1 + r
        rows = slice(r * PEER_NKEYS, (r + 1) * PEER_NKEYS)
        lim_rows = [lim_ref[h, pl.ds(e1, 1), :].astype(BF16) for h in range(PEER_HEADS)]
        p1_rows = [p1_ref[h, pl.ds(e1, 1), :].astype(BF16) for h in range(PEER_HEADS)]
        for tt in range(g_ref.shape[1] // LANES):
            ls = slice(tt * LANES, (tt + 1) * LANES)
            gacc = None
            for h in range(PEER_HEADS):
                hs = slice(h * PEER_NKEYS, (h + 1) * PEER_NKEYS)
                w = jnp.where(rk_ref[hs, ls] < lim_rows[h][:, ls], p2_ref[hs, ls] * p1_rows[h][:, ls], zero)
                gacc = w if gacc is None else gacc + w
            g_ref[rows, ls] = gacc
    act = jnp.dot(u_ref[...], ht_ref[...], preferred_element_type=F32)
    gelu2 = act * (1.0 + jnp.tanh(math.sqrt(2.0 / math.pi) * (act + 0.044715 * (act * act * act))))
    pt = gelu2.astype(BF16) * g_ref[...]
    acc_ref[...] += jnp.dot(vt_ref[...], pt, preferred_element_type=F32)

    @pl.when(j == pl.num_programs(1) - 1)
    def _():
        o_ref[...] = acc_ref[...].T


def _peer_dense(m2t, u_b, vt_b, rk, lim, p1, p2, tm, te):
    d, t = m2t.shape
    ne = u_b.shape[0]
    hk = pl.BlockSpec((PEER_HEADS, PEER_NKEYS, tm), lambda i, j: (0, 0, i))
    flat = pl.BlockSpec((PEER_HEADS * PEER_NKEYS, tm), lambda i, j: (0, i))
    return pl.pallas_call(
        functools.partial(_peer_dense_kernel, ne1=te // PEER_NKEYS),
        grid=(t // tm, ne // te),
        in_specs=[pl.BlockSpec((d, tm), lambda i, j: (0, i)),
                  pl.BlockSpec((te, d), lambda i, j: (j, 0)),
                  pl.BlockSpec((d, te), lambda i, j: (0, j)),
                  flat, hk, hk, flat],
        out_specs=pl.BlockSpec((tm, d), lambda i, j: (i, 0)),
        out_shape=jax.ShapeDtypeStruct((t, d), F32),
        scratch_shapes=[pltpu.VMEM((d, tm), F32), pltpu.VMEM((te, tm), BF16),
                        pltpu.VMEM((PEER_HEADS * PEER_NKEYS, tm), BF16),
                        pltpu.VMEM((PEER_HEADS * PEER_NKEYS, tm), BF16)],
        compiler_params=_cparams(("arbitrary", "arbitrary")),
    )(m2t, u_b, vt_b, rk.reshape(PEER_HEADS * PEER_NKEYS, t), lim, p1, p2.reshape(PEER_HEADS * PEER_NKEYS, t))


SL_X = 0
SL_B = SL_X + SSD_INNER // LANES
SL_C = SL_B + SSD_GROUPS * SSD_STATE // LANES
SL_Z = SL_C + SSD_GROUPS * SSD_STATE // LANES
SL_GH = SL_Z + SSD_INNER // LANES
SL_GS = SL_GH + D_MODEL // LANES


def kernel(x, c, ctx, c_ctx, ada_w, ada_b, w_in, hy_conv_w, hy_conv_b, hy_ffn_w1, hy_ffn_b1, hy_freq1,
           hy_ffn_w2, hy_ffn_b2, hy_freq2, hy_ffn_w3, hy_skip, ssd_conv_w, ssd_conv_b, ssd_a_log,
           ssd_dt_bias, ssd_d, ssd_norm_w, w_branch_hy, w_branch_ssd, w_out, ln1_g, ln1_b, peer_wq,
           peer_subkeys, peer_u, peer_v, ln2_g, ln2_b):
    nb, seq, d = x.shape
    lc = ctx.shape[1]
    t = nb * seq
    p = dict(hy_ffn_w1=hy_ffn_w1[0], hy_ffn_b1=hy_ffn_b1[0], hy_freq1=hy_freq1[0], hy_ffn_w2=hy_ffn_w2[0],
             hy_ffn_b2=hy_ffn_b2[0], hy_freq2=hy_freq2[0], hy_ffn_w3=hy_ffn_w3[0])

    cc = jnp.zeros((8, d), F32).at[:nb].set(c).at[nb].set(c_ctx)
    mod = _ada(cc, ada_w[0], ada_b[0])
    sh1, sc1, gt1, sh2, sc2, gt2 = [mod[:nb, i * d:(i + 1) * d].reshape(nb, 1, d) for i in range(6)]
    csh1 = mod[nb:nb + 1, 0:d].reshape(1, 1, d)
    csc1 = mod[nb:nb + 1, d:2 * d].reshape(1, 1, d)

    w = w_in[0]
    w_hy = w[:, :OFF_X].astype(BF16)
    w_rest = jnp.concatenate([w[:, OFF_X:OFF_DT], w[:, OFF_Z:]], axis=1).astype(BF16)
    w_dt = w[:, OFF_DT:OFF_Z].astype(BF16)

    m_ctx = _modulate(ctx, csc1, csh1, 256).reshape(nb * lc, d)
    pc = _matmul(m_ctx, w_rest, 512, 1024, slab=True, ncols=OFF_C - OFF_X)
    dtc = _matmul(m_ctx, w_dt, 512, LANES)
    nxb = (OFF_C - OFF_X) // LANES
    cw = ssd_conv_w[0].reshape(3, -1, LANES).transpose(1, 0, 2)
    cbias = ssd_conv_b[0].reshape(-1, 1, LANES)
    xbc_ctx = _conv3(pc, 0, nxb, cw[:nxb], cbias[:nxb], lc, True)
    gw = SSD_HPG * SSD_HEAD_DIM
    zero_state = jnp.zeros((nb, SSD_GROUPS, SSD_STATE, gw), F32)
    dt_bias = ssd_dt_bias[0]
    a_log = ssd_a_log[0]
    xs_sl = SSD_INNER // LANES
    _, _, s_f, s_b = _ssd_scan(xbc_ctx, 0, xs_sl, xs_sl, dtc, dt_bias, a_log, zero_state, zero_state, lc)

    m1 = _modulate(x, sc1, sh1, 512).reshape(t, d)
    proj_hy = _matmul(m1, w_hy, 1024, 2048, slab=True)
    proj = _matmul(m1, w_rest, 1024, 2048, out_dtype=BF16, slab=True)
    dtp = _matmul(m1, w_dt, 1024, LANES)

    hw = hy_conv_w[0].reshape(3, -1, LANES).transpose(1, 0, 2)
    hb = hy_conv_b[0].reshape(-1, 1, LANES)
    nh = HY_WIDTH // LANES
    ug = _hyconv(proj_hy, 3 * nh, hw, hb, seq)
    hspec = _hyena_filter_spectra(p, seq)
    zg = _longconv(ug, ug, 0, nh, hy_skip[0, 0], hspec, 0, seq, False)
    y_hy = _longconv(zg, ug, 0, 2 * nh, hy_skip[0, 1], hspec, 1, seq, True)

    xbc = _conv3(proj, SL_X, SL_Z - SL_X, cw, cbias, seq, True)
    y_f, y_b, _, _ = _ssd_scan(xbc, 0, xs_sl, xs_sl + SSD_GROUPS, dtp, dt_bias, a_log, s_f, s_b, seq)
    d_ch = jnp.repeat(ssd_d[0, 0] + ssd_d[0, 1], SSD_HEAD_DIM)
    y_ss = _ssd_norm(y_f, y_b, xbc, proj, SL_Z, d_ch, ssd_norm_w[0], 1024)

    merged = _branch_merge(y_hy, y_ss, w_branch_hy[0].astype(BF16), w_branch_ssd[0].astype(BF16),
                           proj, SL_GH, SL_GS, 512, 1024)
    h1, m2, m2t = _outproj_ln(merged.reshape(nb, seq, d), w_out[0].astype(BF16), x, gt1, ln1_g[0], ln1_b[0],
                              sc2, sh2, 512)

    m2f = m2.reshape(t, d)
    rk, lim, p1, p2 = _peer_select(m2f, peer_wq[0].astype(BF16), peer_subkeys[0].astype(BF16), 256)
    ffn = _peer_dense(m2t, peer_u[0].astype(BF16), peer_v[0].T.astype(BF16), rk, lim, p1, p2, 512, 1024)
    return _resid_ln(h1, ffn.reshape(nb, seq, d), gt2, ln2_g[0], ln2_b[0], 512)
```
